```python
import jax
import jax.numpy as jnp
from jax import lax
import numpy as np


D_MODEL = 1024
BATCH = 4
SEQ = 4096
DEPTH = 1

GLA_HEADS = 4
GLA_DK = D_MODEL // 8
GLA_DV = D_MODEL // 4
GLA_GATE_RANK = 16
GLA_TAU = 16.0
GLA_CHUNK = 64
GLA_K_WIDTH = GLA_HEADS * GLA_DK
GLA_V_WIDTH = GLA_HEADS * GLA_DV

ATT_GROUPS = ((128, 1), (512, 4), (2048, 16))
ATT_HEADS_PER_GROUP = 8
ATT_HEAD_DIM = 64
ATT_BLOCK = 128
ROT_DIM = ATT_HEAD_DIM // 4
ROPE_THETA = 500000.0
ATT_QKV_WIDTH = len(ATT_GROUPS) * ATT_HEADS_PER_GROUP * ATT_HEAD_DIM
ATT_OUT_WIDTH = ATT_HEADS_PER_GROUP * ATT_HEAD_DIM

IN_SPLIT_SIZES = (GLA_K_WIDTH, GLA_K_WIDTH, GLA_V_WIDTH, GLA_V_WIDTH, GLA_GATE_RANK,
                  ATT_QKV_WIDTH, ATT_QKV_WIDTH, ATT_QKV_WIDTH, D_MODEL, D_MODEL)
IN_WIDTH = sum(IN_SPLIT_SIZES)

N_EXPERTS = 32
TOP_K = 4
EXPERT_FF = D_MODEL
SWIGLU_LIMIT = 7.0
SWIGLU_ALPHA = 1.702
MOE_BLOCK = 128

EPS = 1e-5

kernel_name = 'hybrid_gla_dilated_attn_moe'


def rmsnorm(x, g):
    xf = x.astype(jnp.float32)
    y = xf * lax.rsqrt(jnp.mean(xf * xf, axis=-1, keepdims=True) + EPS)
    return (y * g.astype(jnp.float32)).astype(x.dtype)


def partial_rotary(t, pos):
    half = ROT_DIM // 2
    inv_freq = ROPE_THETA ** (-jnp.arange(half, dtype=jnp.float32) * (2.0 / ROT_DIM))
    ang = pos[:, None] * inv_freq[None, :]
    cos = jnp.cos(ang)[None, :, None, :]
    sin = jnp.sin(ang)[None, :, None, :]
    tf = t.astype(jnp.float32)
    r1 = tf[..., :half]
    r2 = tf[..., half:ROT_DIM]
    out = jnp.concatenate([r1 * cos - r2 * sin, r2 * cos + r1 * sin, tf[..., ROT_DIM:]], axis=-1)
    return out.astype(t.dtype)


def gla_chunked(q, k, v, log_decay):
    B, S, H, dk = q.shape
    C = GLA_CHUNK
    nc = S // C

    def chunks(t):
        return t.astype(jnp.float32).reshape(B, nc, C, H, t.shape[-1]).transpose(0, 3, 1, 2, 4)

    qc = chunks(q) * (dk ** -0.5)
    kc, vc, gc = chunks(k), chunks(v), chunks(log_decay)
    b = jnp.cumsum(gc, axis=3)
    b_last = b[:, :, :, -1:, :]
    qe = qc * jnp.exp(b)
    ke = kc * jnp.exp(-b)
    kd = kc * jnp.exp(b_last - b)
    causal = jnp.tril(jnp.ones((C, C), dtype=bool))
    A = jnp.where(causal, jnp.einsum('bhnid,bhnjd->bhnij', qe, ke), 0.0)
    o_intra = jnp.einsum('bhnij,bhnjv->bhniv', A, vc)
    U = jnp.einsum('bhnjd,bhnjv->bhndv', kd, vc)
    decay = jnp.exp(b_last[:, :, :, 0, :])

    def step(state, inp):
        dec, u = inp
        return dec[..., None] * state + u, state

    state0 = jnp.zeros((B, H, dk, vc.shape[-1]), jnp.float32)
    _, prev = lax.scan(step, state0, (decay.transpose(2, 0, 1, 3), U.transpose(2, 0, 1, 3, 4)))
    prev = prev.transpose(1, 2, 0, 3, 4)
    o_inter = jnp.einsum('bhnid,bhndv->bhniv', qe, prev)
    o = o_intra + o_inter
    return o.transpose(0, 2, 3, 1, 4).reshape(B, S, H, vc.shape[-1])


def dilated_group(q, k, v, window, dilation):
    B, S, H, hd = q.shape
    L = ATT_BLOCK
    steps = window // dilation
    unit = dilation * L
    Sp = -(-S // unit) * unit
    n = Sp // dilation
    nb = n // L

    def to_blocks(t):
        t = jnp.pad(t, ((0, 0), (0, Sp - S), (0, 0), (0, 0)))
        t = t.reshape(B, n, dilation, H, hd).transpose(0, 2, 3, 1, 4)
        return t.reshape(B, dilation, H, nb, L, hd)

    def with_prev(t):
        prev = jnp.concatenate([jnp.zeros_like(t[:, :, :, :1]), t[:, :, :, :-1]], axis=3)
        return jnp.concatenate([prev, t], axis=4)

    qb = to_blocks(q)
    kw = with_prev(to_blocks(k))
    vw = with_prev(to_blocks(v))
    s = jnp.einsum('brhnqe,brhnke->brhnqk', qb, kw, preferred_element_type=jnp.float32) * (hd ** -0.5)
    qi = jnp.arange(L)[:, None]
    ki = jnp.arange(2 * L)[None, :]
    dist = qi + L - ki
    band = (dist >= 0) & (dist <= steps)
    has_prev = (jnp.arange(nb)[:, None, None] > 0) | (ki >= L)[None]
    mask = band[None] & has_prev
    s = jnp.where(mask, s, -jnp.inf)
    mx = jnp.max(s, axis=-1, keepdims=True)
    e = jnp.exp(s - mx)
    den = jnp.sum(e, axis=-1, keepdims=True)
    o = jnp.einsum('brhnqk,brhnke->brhnqe', e / den, vw.astype(jnp.float32))
    lse = (mx + jnp.log(den))[..., 0]
    o = o.reshape(B, dilation, H, n, hd).transpose(0, 3, 1, 2, 4).reshape(B, Sp, H, hd)[:, :S]
    lse = lse.reshape(B, dilation, H, n).transpose(0, 3, 1, 2).reshape(B, Sp, H)[:, :S]
    return o, lse


def hybrid_mixer(xn, w_in, gla_gate_up, gla_gate_bias, gla_norm_g, w_branch_gla, w_branch_att, w_mix_out):
    B, S, D = xn.shape
    proj = xn @ w_in
    points = np.cumsum(IN_SPLIT_SIZES)[:-1].tolist()
    gq, gk, gv, gr, glr, aq, ak, av, gate_gla, gate_att = jnp.split(proj, points, axis=-1)

    log_decay = jax.nn.log_sigmoid((glr @ gla_gate_up + gla_gate_bias).astype(jnp.float32)) / GLA_TAU
    o = gla_chunked(gq.reshape(B, S, GLA_HEADS, GLA_DK), gk.reshape(B, S, GLA_HEADS, GLA_DK),
                    gv.reshape(B, S, GLA_HEADS, GLA_DV), log_decay.reshape(B, S, GLA_HEADS, GLA_DK))
    o = o * lax.rsqrt(jnp.mean(o * o, axis=-1, keepdims=True) + EPS) * gla_norm_g.astype(jnp.float32)
    o = o * jax.nn.silu(gr.reshape(B, S, GLA_HEADS, GLA_DV).astype(jnp.float32))
    y_gla = o.reshape(B, S, GLA_V_WIDTH).astype(xn.dtype)

    n_heads = len(ATT_GROUPS) * ATT_HEADS_PER_GROUP
    pos = jnp.arange(S, dtype=jnp.float32)
    aq = partial_rotary(aq.reshape(B, S, n_heads, ATT_HEAD_DIM), pos)
    ak = partial_rotary(ak.reshape(B, S, n_heads, ATT_HEAD_DIM), pos)
    av = av.reshape(B, S, n_heads, ATT_HEAD_DIM)
    outs, lses = [], []
    for g, (window, dilation) in enumerate(ATT_GROUPS):
        hs = slice(g * ATT_HEADS_PER_GROUP, (g + 1) * ATT_HEADS_PER_GROUP)
        o_g, lse_g = dilated_group(aq[:, :, hs], ak[:, :, hs], av[:, :, hs], window, dilation)
        outs.append(o_g)
        lses.append(lse_g)
    alpha = jax.nn.softmax(jnp.stack(lses), axis=0)
    y_att = jnp.sum(alpha[..., None] * jnp.stack(outs), axis=0)
    y_att = y_att.reshape(B, S, ATT_OUT_WIDTH).astype(xn.dtype)

    merged = (jax.nn.sigmoid(gate_gla) * (y_gla @ w_branch_gla)
              + jax.nn.sigmoid(gate_att) * (y_att @ w_branch_att))
    return merged @ w_mix_out


def moe(xn, router_w, router_b, w_gate_up, b_gate_up, w_down, b_down):
    N, D = xn.shape
    logits = (xn @ router_w + router_b).astype(jnp.float32)
    top_v, top_e = lax.top_k(logits, TOP_K)
    gates = jax.nn.softmax(top_v, axis=-1)
    M = N * TOP_K
    e_flat = top_e.reshape(M)
    g_flat = gates.reshape(M)
    tok = jnp.arange(M, dtype=jnp.int32) // TOP_K
    order = jnp.argsort(e_flat)
    e_s, tok_s, g_s = e_flat[order], tok[order], g_flat[order]
    counts = jnp.bincount(e_flat, length=N_EXPERTS)
    starts = jnp.cumsum(counts) - counts
    padded = (counts + MOE_BLOCK - 1) // MOE_BLOCK * MOE_BLOCK
    pstarts = jnp.cumsum(padded) - padded
    dest = pstarts[e_s] + jnp.arange(M, dtype=jnp.int32) - starts[e_s]
    n_blocks = -(-M // MOE_BLOCK) + N_EXPERTS
    P = n_blocks * MOE_BLOCK
    x_pad = jnp.zeros((P, D), xn.dtype).at[dest].set(xn[tok_s])
    block_end = jnp.cumsum(padded) // MOE_BLOCK
    block_expert = jnp.minimum(jnp.searchsorted(block_end, jnp.arange(n_blocks), side='right'), N_EXPERTS - 1)

    def expert_block(args):
        xb, e = args
        h = xb @ w_gate_up[e] + b_gate_up[e]
        gate = jnp.minimum(h[:, ::2], SWIGLU_LIMIT)
        up = jnp.clip(h[:, 1::2], -SWIGLU_LIMIT, SWIGLU_LIMIT)
        act = (up + 1.0) * (gate * jax.nn.sigmoid(SWIGLU_ALPHA * gate))
        return act @ w_down[e] + b_down[e]

    y_pad = lax.map(expert_block, (x_pad.reshape(n_blocks, MOE_BLOCK, D), block_expert)).reshape(P, D)
    contrib = y_pad[dest] * g_s[:, None].astype(y_pad.dtype)
    return jax.ops.segment_sum(contrib, tok_s, num_segments=N)


def setup_inputs(seed: int = 0) -> dict:
    key = jax.random.key(seed)
    ks = jax.random.split(key, 17)

    def nrm(k, shape, scale):
        return jax.random.normal(k, shape, jnp.float32) * scale

    return {
        'x': nrm(ks[0], (BATCH, SEQ, D_MODEL), 1.0),
        'norm_mix_g': 1.0 + nrm(ks[1], (DEPTH, D_MODEL), 0.02),
        'w_in': nrm(ks[2], (DEPTH, D_MODEL, IN_WIDTH), D_MODEL ** -0.5),
        'gla_gate_up': nrm(ks[3], (DEPTH, GLA_GATE_RANK, GLA_K_WIDTH), GLA_GATE_RANK ** -0.5),
        'gla_gate_bias': nrm(ks[4], (DEPTH, GLA_K_WIDTH), 0.1),
        'gla_norm_g': 1.0 + nrm(ks[5], (DEPTH, GLA_DV), 0.02),
        'w_branch_gla': nrm(ks[6], (DEPTH, GLA_V_WIDTH, D_MODEL), GLA_V_WIDTH ** -0.5),
        'w_branch_att': nrm(ks[7], (DEPTH, ATT_OUT_WIDTH, D_MODEL), ATT_OUT_WIDTH ** -0.5),
        'w_mix_out': nrm(ks[8], (DEPTH, D_MODEL, D_MODEL), D_MODEL ** -0.5),
        'norm_ffn_g': 1.0 + nrm(ks[9], (DEPTH, D_MODEL), 0.02),
        'router_w': nrm(ks[10], (DEPTH, D_MODEL, N_EXPERTS), D_MODEL ** -0.5),
        'router_b': nrm(ks[11], (DEPTH, N_EXPERTS), 0.01),
        'expert_w_gate_up': nrm(ks[12], (DEPTH, N_EXPERTS, D_MODEL, 2 * EXPERT_FF), D_MODEL ** -0.5),
        'expert_b_gate_up': nrm(ks[13], (DEPTH, N_EXPERTS, 2 * EXPERT_FF), 0.01),
        'expert_w_down': nrm(ks[14], (DEPTH, N_EXPERTS, EXPERT_FF, D_MODEL), EXPERT_FF ** -0.5),
        'expert_b_down': nrm(ks[15], (DEPTH, N_EXPERTS, D_MODEL), 0.01),
        'norm_final_g': 1.0 + nrm(ks[16], (D_MODEL,), 0.02),
    }


def reference(x, norm_mix_g, w_in, gla_gate_up, gla_gate_bias, gla_norm_g, w_branch_gla, w_branch_att,
              w_mix_out, norm_ffn_g, router_w, router_b, expert_w_gate_up, expert_b_gate_up,
              expert_w_down, expert_b_down, norm_final_g):
    B, S, D = x.shape
    h = x
    for l in range(DEPTH):
        h = h + hybrid_mixer(rmsnorm(h, norm_mix_g[l]), w_in[l], gla_gate_up[l], gla_gate_bias[l],
                             gla_norm_g[l], w_branch_gla[l], w_branch_att[l], w_mix_out[l])
        hn = rmsnorm(h, norm_ffn_g[l]).reshape(B * S, D)
        h = h + moe(hn, router_w[l], router_b[l], expert_w_gate_up[l], expert_b_gate_up[l],
                    expert_w_down[l], expert_b_down[l]).reshape(B, S, D)
    return rmsnorm(h, norm_final_g)
```

```python
import functools
import math

import jax
import jax.numpy as jnp
import numpy as np
from jax import lax
from jax.experimental import pallas as pl
from jax.experimental.pallas import tpu as pltpu

GLA_HEADS = 4
GLA_GATE_RANK = 16
GLA_TAU = 16.0
GLA_CHUNK = 64
ATT_GROUPS = ((128, 1), (512, 4), (2048, 16))
ATT_HEADS_PER_GROUP = 8
ATT_HEAD_DIM = 64
ATT_BLOCK = 128
ROT_DIM = ATT_HEAD_DIM // 4
ROPE_THETA = 500000.0
N_EXPERTS = 32
TOP_K = 4
SWIGLU_LIMIT = 7.0
SWIGLU_ALPHA = 1.702
EPS = 1e-5

LANES = 128
MXU_DIM = 256
VMEM_LIMIT = 56 * 1024 * 1024

F32 = jnp.float32
BF16 = jnp.bfloat16


def _cparams(sem):
    return pltpu.CompilerParams(dimension_semantics=sem, vmem_limit_bytes=VMEM_LIMIT)


def _dot(a, b):
    return jnp.dot(a, b, preferred_element_type=F32)


def _dot_nt(a, b):
    return lax.dot_general(a, b, (((1,), (1,)), ((), ())), preferred_element_type=F32)


def _dot_tn(a, b):
    return lax.dot_general(a, b, (((0,), (0,)), ((), ())), preferred_element_type=F32)


def _split2(a):
    hi = a.astype(BF16)
    lo = (a - hi.astype(F32)).astype(BF16)
    return hi, lo


def _inproj_kernel(x_ref, g_ref, w_ref, wglr_ref, cos_ref, sa_ref, sb_ref, proj_ref, glr_ref, xn_ref,
                   *, rot_lo, k_lo, rot_hi, q_scale):
    j = pl.program_id(1)

    @pl.when(j == 0)
    def _():
        x = x_ref[...]
        y = x * lax.rsqrt(jnp.mean(x * x, axis=-1, keepdims=True) + EPS) * g_ref[...]
        xn_ref[...] = y.astype(BF16)
        yh, yl = _split2(y)
        wh, wl = _split2(wglr_ref[...])
        glr_ref[...] = _dot(yh, wh) + _dot(yh, wl) + _dot(yl, wh)

    acc = _dot(xn_ref[...], w_ref[...])
    is_rot = jnp.logical_and(j >= rot_lo, j < rot_hi)

    @pl.when(is_rot)
    def _():
        scale = jnp.where(j < k_lo, q_scale, 1.0).astype(F32)
        c = cos_ref[...]
        sa = sa_ref[...]
        sb = sb_ref[...]
        half = ROT_DIM // 2
        tn = acc.shape[1]
        for s in range(tn // LANES):
            t = acc[:, s * LANES:(s + 1) * LANES]
            out = t * c + pltpu.roll(t, LANES - half, 1) * sa + pltpu.roll(t, half, 1) * sb
            proj_ref[:, s * LANES:(s + 1) * LANES] = (out * scale).astype(BF16)

    @pl.when(jnp.logical_not(is_rot))
    def _():
        proj_ref[...] = acc.astype(BF16)


def _rope_tables(seq):
    half = ROT_DIM // 2
    inv_freq = ROPE_THETA ** (-np.arange(half, dtype=np.float32) * np.float32(2.0 / ROT_DIM))
    pos = jnp.arange(seq, dtype=F32)
    ang = pos[:, None] * jnp.asarray(inv_freq, F32)[None, :]
    cos = jnp.cos(ang)
    sin = jnp.sin(ang)
    lane = np.arange(LANES) % ATT_HEAD_DIM
    idx = np.where(lane < ROT_DIM, lane % half, 0)
    in_rot = jnp.asarray(lane < ROT_DIM)
    first = jnp.asarray(lane < half)
    second = jnp.asarray((lane >= half) & (lane < ROT_DIM))
    cos_t = jnp.where(in_rot[None, :], cos[:, idx], 1.0)
    sin_g = sin[:, idx]
    sa = jnp.where(first[None, :], -sin_g, 0.0)
    sb = jnp.where(second[None, :], sin_g, 0.0)
    return cos_t.astype(F32), sa.astype(F32), sb.astype(F32)


def _in_projection(x2, norm_g, w_main, w_glr, seq, layout):
    n, d = x2.shape
    width = w_main.shape[1]
    tm = min(2048, seq)
    tn = 512
    assert n % tm == 0 and seq % tm == 0 and width % tn == 0
    cos_t, sa, sb = _rope_tables(seq)
    spt = seq // tm
    kern = functools.partial(_inproj_kernel, rot_lo=layout['aq'] // tn, k_lo=layout['ak'] // tn,
                             rot_hi=layout['av'] // tn, q_scale=ATT_HEAD_DIM ** -0.5)
    return pl.pallas_call(
        kern,
        grid=(n // tm, width // tn),
        in_specs=[
            pl.BlockSpec((tm, d), lambda i, j: (i, 0)),
            pl.BlockSpec((1, d), lambda i, j: (0, 0)),
            pl.BlockSpec((d, tn), lambda i, j: (0, j)),
            pl.BlockSpec((d, GLA_GATE_RANK), lambda i, j: (0, 0)),
            pl.BlockSpec((tm, LANES), lambda i, j: (i % spt, 0)),
            pl.BlockSpec((tm, LANES), lambda i, j: (i % spt, 0)),
            pl.BlockSpec((tm, LANES), lambda i, j: (i % spt, 0)),
        ],
        out_specs=[
            pl.BlockSpec((tm, tn), lambda i, j: (i, j)),
            pl.BlockSpec((tm, GLA_GATE_RANK), lambda i, j: (i, 0)),
        ],
        out_shape=[
            jax.ShapeDtypeStruct((n, width), BF16),
            jax.ShapeDtypeStruct((n, GLA_GATE_RANK), F32),
        ],
        scratch_shapes=[pltpu.VMEM((tm, d), BF16)],
        compiler_params=_cparams(("arbitrary", "arbitrary")),
        name="in_projection",
    )(x2, norm_g.reshape(1, d), w_main, w_glr, cos_t, sa, sb)


def _gla_kernel(q_ref, k_ref, v_ref, r_ref, glr_ref, up_ref, bias_ref, ng_ref, tri_ref, y_ref, state_ref,
                *, dk, dv, ts):
    t = pl.program_id(2)

    @pl.when(t == 0)
    def _():
        state_ref[...] = jnp.zeros_like(state_ref)

    c = GLA_CHUNK
    gh, gl = _split2(glr_ref[...])
    uh, ul = _split2(up_ref[...])
    z = _dot(gh, uh) + _dot(gh, ul) + _dot(gl, uh) + bias_ref[...]
    logdec = (jnp.minimum(z, 0.0) - jnp.log1p(jnp.exp(-jnp.abs(z)))) * (1.0 / GLA_TAU)
    ldh, ldl = _split2(logdec)
    tri = tri_ref[...]
    row = lax.broadcasted_iota(jnp.int32, (c, c), 0)
    col = lax.broadcasted_iota(jnp.int32, (c, c), 1)
    causal = col <= row
    scale = dk ** -0.5
    ng = ng_ref[...]
    state = state_ref[...]
    for ci in range(ts // c):
        sl = slice(ci * c, (ci + 1) * c)
        b = _dot(tri, ldh[sl]) + _dot(tri, ldl[sl])
        bl = b[c - 1:c, :]
        eb = jnp.exp(b)
        enb = jnp.exp(-b)
        ebl = jnp.exp(bl)
        qf = q_ref[sl, :].astype(F32)
        kf = k_ref[sl, :].astype(F32)
        qe = (qf * scale * eb).astype(BF16)
        ke = (kf * enb).astype(BF16)
        kd = (kf * enb * ebl).astype(BF16)
        vb = v_ref[sl, :]
        a = jnp.where(causal, _dot_nt(qe, ke), 0.0).astype(BF16)
        o = _dot(a, vb) + _dot(qe, state.astype(BF16))
        dec = jnp.transpose(jnp.broadcast_to(ebl, (dk, dk)))
        dec_full = jnp.concatenate([dec] * (dv // dk), axis=1)
        state = dec_full * state + _dot_tn(kd, vb)
        o = o * lax.rsqrt(jnp.mean(o * o, axis=-1, keepdims=True) + EPS) * ng
        rf = r_ref[sl, :].astype(F32)
        o = o * (rf * jax.nn.sigmoid(rf))
        y_ref[sl, :] = o.astype(BF16)
    state_ref[...] = state


def _gla(proj, glr, gate_up, gate_bias, norm_g, batch, seq, layout):
    n = proj.shape[0]
    dk = (layout['gk'] - layout['gq']) // GLA_HEADS
    dv = (layout['gr'] - layout['gv']) // GLA_HEADS
    ts = min(512, seq)
    assert dk == LANES and dv % dk == 0 and seq % ts == 0 and ts % GLA_CHUNK == 0
    spt = seq // ts
    qb, kb = layout['gq'] // dk, layout['gk'] // dk
    vb, rb = layout['gv'] // dv, layout['gr'] // dv
    tri = jnp.asarray(np.tril(np.ones((GLA_CHUNK, GLA_CHUNK), np.float32)), BF16)
    kern = functools.partial(_gla_kernel, dk=dk, dv=dv, ts=ts)
    row = lambda b, h, t: b * spt + t
    return pl.pallas_call(
        kern,
        grid=(batch, GLA_HEADS, spt),
        in_specs=[
            pl.BlockSpec((ts, dk), lambda b, h, t: (row(b, h, t), qb + h)),
            pl.BlockSpec((ts, dk), lambda b, h, t: (row(b, h, t), kb + h)),
            pl.BlockSpec((ts, dv), lambda b, h, t: (row(b, h, t), vb + h)),
            pl.BlockSpec((ts, dv), lambda b, h, t: (row(b, h, t), rb + h)),
            pl.BlockSpec((ts, GLA_GATE_RANK), lambda b, h, t: (row(b, h, t), 0)),
            pl.BlockSpec((GLA_GATE_RANK, dk), lambda b, h, t: (0, h)),
            pl.BlockSpec((1, dk), lambda b, h, t: (0, h)),
            pl.BlockSpec((1, dv), lambda b, h, t: (0, 0)),
            pl.BlockSpec((GLA_CHUNK, GLA_CHUNK), lambda b, h, t: (0, 0)),
        ],
        out_specs=pl.BlockSpec((ts, dv), lambda b, h, t: (row(b, h, t), h)),
        out_shape=jax.ShapeDtypeStruct((n, GLA_HEADS * dv), BF16),
        scratch_shapes=[pltpu.VMEM((dk, dv), F32)],
        compiler_params=_cparams(("arbitrary", "arbitrary", "arbitrary")),
        name="gla",
    )(proj, proj, proj, proj, glr, gate_up, gate_bias.reshape(1, -1), norm_g.reshape(1, -1), tri)


def _att_kernel(q_ref, k_ref, v_ref, o_ref, lse_ref, *, nb):
    blk = ATT_BLOCK
    hd = ATT_HEAD_DIM
    row = lax.broadcasted_iota(jnp.int32, (blk, blk), 0)
    col = lax.broadcasted_iota(jnp.int32, (blk, blk), 1)
    m_prev = col >= row
    m_cur = col <= row
    lane = lax.broadcasted_iota(jnp.int32, (blk, LANES), 1)
    head0 = lane < hd
    ones = jnp.ones((blk, LANES), BF16)
    neg = -jnp.inf

    def body(j, carry):
        r0 = pl.multiple_of(j * blk, blk)
        p0 = pl.multiple_of(jnp.maximum(j - 1, 0) * blk, blk)
        q2 = q_ref[0, pl.ds(r0, blk), :]
        kc = k_ref[0, pl.ds(r0, blk), :]
        vc = v_ref[0, pl.ds(r0, blk), :]
        kp = k_ref[0, pl.ds(p0, blk), :]
        vp = v_ref[0, pl.ds(p0, blk), :]
        mp = jnp.logical_and(m_prev, j > 0)
        outs, lses = [], []
        for h in range(2):
            qh = jnp.where(head0 if h == 0 else jnp.logical_not(head0), q2, jnp.zeros_like(q2))
            sp = jnp.where(mp, _dot_nt(qh, kp), neg)
            sc = jnp.where(m_cur, _dot_nt(qh, kc), neg)
            mx = jnp.maximum(jnp.max(sp, axis=-1, keepdims=True), jnp.max(sc, axis=-1, keepdims=True))
            ep = jnp.exp(sp - mx).astype(BF16)
            ec = jnp.exp(sc - mx).astype(BF16)
            den = _dot(ep, ones) + _dot(ec, ones)
            pv = _dot(ep, vp) + _dot(ec, vc)
            outs.append(pv / den)
            lses.append(mx + jnp.log(den))
        o_ref[0, pl.ds(r0, blk), :] = jnp.where(head0, outs[0], outs[1]).astype(o_ref.dtype)
        lse_ref[0, pl.ds(r0, blk), :] = jnp.where(head0, lses[0], lses[1])
        return carry

    lax.fori_loop(0, nb, body, 0)


def _dilated_attention(proj, batch, seq, layout, group, dilation):
    n, width = proj.shape
    hw = ATT_HEADS_PER_GROUP * ATT_HEAD_DIM
    sub = seq // dilation
    assert seq % (dilation * ATT_BLOCK) == 0 and width % LANES == 0 and hw % LANES == 0
    nb = sub // ATT_BLOCK
    wb = width // LANES
    pairs = hw // LANES
    view = proj.reshape(batch, sub, dilation * width)
    qb = (layout['aq'] + group * hw) // LANES
    kb = (layout['ak'] + group * hw) // LANES
    vb = (layout['av'] + group * hw) // LANES
    kern = functools.partial(_att_kernel, nb=nb)

    def in_spec(base):
        return pl.BlockSpec((1, sub, LANES), lambda b, r, p: (b, 0, r * wb + base + p))

    out_spec = pl.BlockSpec((1, sub, LANES), lambda b, r, p: (b, 0, r * pairs + p))
    o, lse = pl.pallas_call(
        kern,
        grid=(batch, dilation, pairs),
        in_specs=[in_spec(qb), in_spec(kb), in_spec(vb)],
        out_specs=[out_spec, out_spec],
        out_shape=[
            jax.ShapeDtypeStruct((batch, sub, dilation * hw), BF16),
            jax.ShapeDtypeStruct((batch, sub, dilation * hw), F32),
        ],
        compiler_params=_cparams(("arbitrary", "arbitrary", "arbitrary")),
        name=f"dilated_attention_d{dilation}",
    )(view, view, view)
    return o.reshape(n, hw), lse.reshape(n, hw)


def _merge_kernel(x_ref, yg_ref, o1_ref, o2_ref, o3_ref, l1_ref, l2_ref, l3_ref,
                  gg0_ref, gg1_ref, ga0_ref, ga1_ref, wbg_ref, wba_ref, wmo_ref, nf_ref, rw_ref, rb_ref, triu_ref,
                  h_ref, hn_ref, te_ref, tg_ref, rk_ref, cum_ref, cnt_ref, carry_ref):
    i = pl.program_id(0)
    ne = N_EXPERTS

    @pl.when(i == 0)
    def _():
        carry_ref[...] = jnp.zeros_like(carry_ref)

    l1, l2, l3 = l1_ref[...], l2_ref[...], l3_ref[...]
    m = jnp.maximum(jnp.maximum(l1, l2), l3)
    w1, w2, w3 = jnp.exp(l1 - m), jnp.exp(l2 - m), jnp.exp(l3 - m)
    y_att = (w1 * o1_ref[...].astype(F32) + w2 * o2_ref[...].astype(F32) + w3 * o3_ref[...].astype(F32)) \
        / (w1 + w2 + w3)

    m_gla = _dot(yg_ref[...], wbg_ref[...])
    m_att = _dot(y_att.astype(BF16), wba_ref[...])
    gg = jnp.concatenate([gg0_ref[...], gg1_ref[...]], axis=1).astype(F32)
    ga = jnp.concatenate([ga0_ref[...], ga1_ref[...]], axis=1).astype(F32)
    merged = jax.nn.sigmoid(gg) * m_gla + jax.nn.sigmoid(ga) * m_att
    h = x_ref[...] + _dot(merged.astype(BF16), wmo_ref[...])
    h_ref[...] = h
    hn = h * lax.rsqrt(jnp.mean(h * h, axis=-1, keepdims=True) + EPS) * nf_ref[...]
    hn_ref[...] = hn.astype(BF16)

    hh, hl = _split2(hn)
    rh, rl = _split2(rw_ref[...])
    logit = _dot_nt(rh, hh) + _dot_nt(rh, hl) + _dot_nt(rl, hh) + rb_ref[...]
    tm = logit.shape[1]
    eid = lax.broadcasted_iota(jnp.int32, (ne, tm), 0)
    member = jnp.zeros((ne, tm), jnp.bool_)
    vals, idxs = [], []
    for _ in range(TOP_K):
        mx = jnp.max(logit, axis=0, keepdims=True)
        idx = jnp.min(jnp.where(logit == mx, eid, ne), axis=0, keepdims=True)
        sel = eid == idx
        member = jnp.logical_or(member, sel)
        logit = jnp.where(sel, -jnp.inf, logit)
        vals.append(mx)
        idxs.append(idx)
    ex = [jnp.exp(v - vals[0]) for v in vals]
    tot = ex[0] + ex[1] + ex[2] + ex[3]
    memf = jnp.where(member, 1.0, 0.0)
    carry = carry_ref[...]
    pos = _dot(memf.astype(BF16), triu_ref[...]) + carry[:, 0:1]
    for k in range(TOP_K):
        te_ref[k:k + 1, :] = idxs[k]
        tg_ref[k:k + 1, :] = ex[k] / tot
        rk_ref[k:k + 1, :] = jnp.sum(jnp.where(eid == idxs[k], pos, 0.0), axis=0, keepdims=True).astype(jnp.int32)
    cum_ref[0] = carry.astype(jnp.int32)
    carry = carry + jnp.sum(memf, axis=1, keepdims=True)
    carry_ref[...] = carry
    cnt_ref[...] = carry.astype(jnp.int32)


def _merge(x2, y_gla, att, proj, layout, wbg, wba, wmo, norm_ffn_g, router_w, router_b, tm):
    n, d = x2.shape
    hw = ATT_HEADS_PER_GROUP * ATT_HEAD_DIM
    half = d // 2
    assert n % tm == 0 and layout['gate_gla'] % half == 0 and layout['gate_att'] % half == 0
    nt = n // tm
    ggb = layout['gate_gla'] // half
    gab = layout['gate_att'] // half
    triu = jnp.asarray(np.triu(np.ones((tm, tm), np.float32), 1), BF16)
    row = lambda i: (i, 0)
    const = lambda i: (0, 0)
    (o1, l1), (o2, l2), (o3, l3) = att
    in_specs = [
        pl.BlockSpec((tm, d), row),
        pl.BlockSpec((tm, y_gla.shape[1]), row),
        pl.BlockSpec((tm, hw), row), pl.BlockSpec((tm, hw), row), pl.BlockSpec((tm, hw), row),
        pl.BlockSpec((tm, hw), row), pl.BlockSpec((tm, hw), row), pl.BlockSpec((tm, hw), row),
        pl.BlockSpec((tm, half), lambda i: (i, ggb)), pl.BlockSpec((tm, half), lambda i: (i, ggb + 1)),
        pl.BlockSpec((tm, half), lambda i: (i, gab)), pl.BlockSpec((tm, half), lambda i: (i, gab + 1)),
        pl.BlockSpec(wbg.shape, const), pl.BlockSpec(wba.shape, const), pl.BlockSpec(wmo.shape, const),
        pl.BlockSpec((1, d), const),
        pl.BlockSpec((N_EXPERTS, d), const),
        pl.BlockSpec((N_EXPERTS, 1), const),
        pl.BlockSpec((tm, tm), const),
    ]
    out_specs = [
        pl.BlockSpec((tm, d), row),
        pl.BlockSpec((tm, d), row),
        pl.BlockSpec((TOP_K, tm), lambda i: (0, i)),
        pl.BlockSpec((TOP_K, tm), lambda i: (0, i)),
        pl.BlockSpec((TOP_K, tm), lambda i: (0, i)),
        pl.BlockSpec((1, N_EXPERTS, LANES), lambda i: (i, 0, 0)),
        pl.BlockSpec((N_EXPERTS, LANES), const),
    ]
    out_shape = [
        jax.ShapeDtypeStruct((n, d), F32),
        jax.ShapeDtypeStruct((n, d), BF16),
        jax.ShapeDtypeStruct((TOP_K, n), jnp.int32),
        jax.ShapeDtypeStruct((TOP_K, n), F32),
        jax.ShapeDtypeStruct((TOP_K, n), jnp.int32),
        jax.ShapeDtypeStruct((nt, N_EXPERTS, LANES), jnp.int32),
        jax.ShapeDtypeStruct((N_EXPERTS, LANES), jnp.int32),
    ]
    return pl.pallas_call(
        _merge_kernel,
        grid=(nt,),
        in_specs=in_specs,
        out_specs=out_specs,
        out_shape=out_shape,
        scratch_shapes=[pltpu.VMEM((N_EXPERTS, LANES), F32)],
        compiler_params=_cparams(("arbitrary",)),
        name="merge_router",
    )(x2, y_gla, o1, o2, o3, l1, l2, l3, proj, proj, proj, proj, wbg, wba, wmo,
      norm_ffn_g.reshape(1, d), router_w.T, router_b.reshape(N_EXPERTS, 1), triu)


def _dispatch_kernel(tlo_ref, thi_ref, hn_ref, dest_ref, out_ref, *, bm, tt):
    b = pl.program_id(0)
    row_id = b * bm + lax.broadcasted_iota(jnp.int32, (bm, tt), 0)
    out_ref[...] = jnp.zeros_like(out_ref)

    def body(t, carry):
        d4 = dest_ref[t]
        hit = row_id == d4[0:1, :]
        for k in range(1, TOP_K):
            hit = jnp.logical_or(hit, row_id == d4[k:k + 1, :])
        onehot = jnp.where(hit, 1.0, 0.0).astype(BF16)
        out_ref[...] += _dot(onehot, hn_ref[t]).astype(BF16)
        return carry

    lax.fori_loop(tlo_ref[b], thi_ref[b], body, 0)


def _dispatch(hn, dest, tlo, thi, nblk, bm, tt):
    n, d = hn.shape
    nt = n // tt
    kern = functools.partial(_dispatch_kernel, bm=bm, tt=tt)
    grid_spec = pltpu.PrefetchScalarGridSpec(
        num_scalar_prefetch=2,
        grid=(nblk,),
        in_specs=[
            pl.BlockSpec(memory_space=pltpu.VMEM),
            pl.BlockSpec(memory_space=pltpu.VMEM),
        ],
        out_specs=pl.BlockSpec((bm, d), lambda b, tlo, thi: (b, 0)),
    )
    return pl.pallas_call(
        kern,
        grid_spec=grid_spec,
        out_shape=jax.ShapeDtypeStruct((nblk * bm, d), BF16),
        compiler_params=_cparams(("arbitrary",)),
        name="moe_dispatch",
    )(tlo, thi, hn.reshape(nt, tt, d), dest.reshape(TOP_K, nt, tt).transpose(1, 0, 2))


def _expert_kernel(bexp_ref, nact_ref, x_ref, wgu_ref, wd_ref, bgu_ref, bd_ref, perm_ref, y_ref, wgu_s, wd_s):
    i = pl.program_id(0)
    e = bexp_ref[i]
    prev = bexp_ref[jnp.maximum(i - 1, 0)]
    active = i < nact_ref[0]
    grp = MXU_DIM
    ff2 = wgu_s.shape[1]
    halfg = grp // 2

    @pl.when(jnp.logical_and(active, jnp.logical_or(i == 0, e != prev)))
    def _():
        perm = perm_ref[...]
        for g in range(ff2 // grp):
            cols = slice(g * grp, (g + 1) * grp)
            wgu_s[:, cols] = _dot(wgu_ref[0, :, cols].astype(BF16), perm).astype(BF16)
        wd_s[...] = wd_ref[0].astype(BF16)

    @pl.when(active)
    def _():
        x = x_ref[...]
        acts = []
        for g in range(ff2 // grp):
            cols = slice(g * grp, (g + 1) * grp)
            hg = _dot(x, wgu_s[:, cols]) + bgu_ref[0, :, cols]
            gate = jnp.minimum(hg[:, :halfg], SWIGLU_LIMIT)
            up = jnp.clip(hg[:, halfg:], -SWIGLU_LIMIT, SWIGLU_LIMIT)
            acts.append(((up + 1.0) * (gate * jax.nn.sigmoid(SWIGLU_ALPHA * gate))).astype(BF16))
        act = jnp.concatenate(acts, axis=1)
        y_ref[...] = (_dot(act, wd_s[...]) + bd_ref[0]).astype(y_ref.dtype)

    @pl.when(jnp.logical_not(active))
    def _():
        y_ref[...] = jnp.zeros_like(y_ref)


def _gate_up_permutation():
    grp = MXU_DIM
    halfg = grp // 2
    out = np.arange(grp)
    src = np.where(out < halfg, 2 * out, 2 * (out - halfg) + 1)
    p = np.zeros((grp, grp), np.float32)
    p[src, out] = 1.0
    return jnp.asarray(p, BF16)


def _experts(x_pad, bexp, nact, w_gate_up, b_gate_up, w_down, b_down, bm):
    p_rows, d = x_pad.shape
    ne, _, ff2 = w_gate_up.shape
    ff = ff2 // 2
    grp = MXU_DIM
    assert ff2 % grp == 0
    nblk = p_rows // bm
    bgu = b_gate_up.reshape(ne, ff2 // grp, grp // 2, 2).transpose(0, 1, 3, 2).reshape(ne, 1, ff2)
    grid_spec = pltpu.PrefetchScalarGridSpec(
        num_scalar_prefetch=2,
        grid=(nblk,),
        in_specs=[
            pl.BlockSpec((bm, d), lambda i, be, na: (i, 0)),
            pl.BlockSpec((1, d, ff2), lambda i, be, na: (be[i], 0, 0)),
            pl.BlockSpec((1, ff, d), lambda i, be, na: (be[i], 0, 0)),
            pl.BlockSpec((1, 1, ff2), lambda i, be, na: (be[i], 0, 0)),
            pl.BlockSpec((1, 1, d), lambda i, be, na: (be[i], 0, 0)),
            pl.BlockSpec((grp, grp), lambda i, be, na: (0, 0)),
        ],
        out_specs=pl.BlockSpec((bm, d), lambda i, be, na: (i, 0)),
        scratch_shapes=[pltpu.VMEM((d, ff2), BF16), pltpu.VMEM((ff, d), BF16)],
    )
    return pl.pallas_call(
        _expert_kernel,
        grid_spec=grid_spec,
        out_shape=jax.ShapeDtypeStruct((p_rows, d), BF16),
        compiler_params=_cparams(("arbitrary",)),
        name="moe_experts",
    )(bexp, nact, x_pad, w_gate_up, w_down, bgu, b_down.reshape(ne, 1, d), _gate_up_permutation())


COMBINE_CHUNK = 32


def _combine_kernel(lo_ref, hi_ref, h_ref, dest_ref, gate_ref, g_ref, y_hbm, out_ref, ybuf, base_ref, sem,
                    *, tt, max_slots):
    i = pl.program_id(0)
    ch = COMBINE_CHUNK
    kg = MXU_DIM
    spg = kg // ch

    def window_copy(chunk, slot):
        return pltpu.make_async_copy(y_hbm.at[pl.ds(chunk * ch, ch), :], ybuf.at[pl.ds(slot * ch, ch), :], sem)

    def per_expert(e, slot):
        lo = lo_ref[i * N_EXPERTS + e]
        hi = hi_ref[i * N_EXPERTS + e]
        c0 = lo // ch
        c1 = jnp.where(hi > lo, (hi + ch - 1) // ch, c0)

        def per_chunk(c, s):
            window_copy(c, s).start()
            base_ref[s] = c * ch
            return s + 1

        return lax.fori_loop(c0, c1, per_chunk, slot)

    nslots = lax.fori_loop(0, N_EXPERTS, per_expert, 0)
    ngroups = (nslots + spg - 1) // spg

    def clear(s, carry):
        ybuf[pl.ds(s * ch, ch), :] = jnp.zeros((ch, ybuf.shape[1]), ybuf.dtype)
        base_ref[s] = -(1 << 20)
        return carry

    lax.fori_loop(nslots, ngroups * spg, clear, 0)

    def wait_one(s, carry):
        window_copy(0, s).wait()
        return carry

    lax.fori_loop(0, nslots, wait_one, 0)

    dest = dest_ref[0]
    gate = gate_ref[0]
    sub = lax.broadcasted_iota(jnp.int32, (ch, tt), 0)

    def per_group(g, acc):
        parts = []
        for s in range(spg):
            rid = base_ref[g * spg + s] + sub
            w = jnp.zeros((ch, tt), F32)
            for k in range(TOP_K):
                w = w + jnp.where(rid == dest[k:k + 1, :], gate[k:k + 1, :], 0.0)
            parts.append(w)
        gt = jnp.concatenate(parts, axis=0).astype(BF16)
        r0 = pl.multiple_of(g * kg, kg)
        return acc + _dot_tn(gt, ybuf[pl.ds(r0, kg), :])

    moe = lax.fori_loop(0, ngroups, per_group, jnp.zeros((tt, out_ref.shape[1]), F32))
    h = h_ref[...] + moe
    out_ref[...] = h * lax.rsqrt(jnp.mean(h * h, axis=-1, keepdims=True) + EPS) * g_ref[...]


def _combine(h, y_pad, dest, gates, lo_tab, hi_tab, norm_g, tt):
    n, d = h.shape
    nt = n // tt
    max_slots = (TOP_K * tt) // COMBINE_CHUNK + 2 * N_EXPERTS
    max_slots = -(-max_slots // (MXU_DIM // COMBINE_CHUNK)) * (MXU_DIM // COMBINE_CHUNK)
    kern = functools.partial(_combine_kernel, tt=tt, max_slots=max_slots)
    grid_spec = pltpu.PrefetchScalarGridSpec(
        num_scalar_prefetch=2,
        grid=(nt,),
        in_specs=[
            pl.BlockSpec((tt, d), lambda i, lo, hi: (i, 0)),
            pl.BlockSpec((1, TOP_K, tt), lambda i, lo, hi: (i, 0, 0)),
            pl.BlockSpec((1, TOP_K, tt), lambda i, lo, hi: (i, 0, 0)),
            pl.BlockSpec((1, d), lambda i, lo, hi: (0, 0)),
            pl.BlockSpec(memory_space=pl.ANY),
        ],
        out_specs=pl.BlockSpec((tt, d), lambda i, lo, hi: (i, 0)),
        scratch_shapes=[
            pltpu.VMEM((max_slots * COMBINE_CHUNK, d), BF16),
            pltpu.SMEM((max_slots,), jnp.int32),
            pltpu.SemaphoreType.DMA(()),
        ],
    )
    return pl.pallas_call(
        kern,
        grid_spec=grid_spec,
        out_shape=jax.ShapeDtypeStruct((n, d), F32),
        compiler_params=_cparams(("arbitrary",)),
        name="moe_combine",
    )(lo_tab, hi_tab, h, dest.reshape(TOP_K, nt, tt).transpose(1, 0, 2),
      gates.reshape(TOP_K, nt, tt).transpose(1, 0, 2), norm_g.reshape(1, d), y_pad)


def _routing_tables(top_e, rank, cum, counts, bm, nblk):
    nt = cum.shape[0]
    padded = (counts + bm - 1) // bm * bm
    pend = jnp.cumsum(padded)
    pstart = pend - padded
    dest = pstart[top_e] + rank
    blk = jnp.arange(nblk, dtype=jnp.int32)
    bexp_raw = jnp.sum((pend // bm)[None, :] <= blk[:, None], axis=1).astype(jnp.int32)
    nact = (pend[-1] // bm).astype(jnp.int32)
    last = jnp.minimum(bexp_raw[jnp.maximum(nact - 1, 0)], N_EXPERTS - 1)
    bexp = jnp.where(blk < nact, jnp.minimum(bexp_raw, N_EXPERTS - 1), last)
    cum_full = jnp.concatenate([cum, counts[None, :]], axis=0)
    j0 = blk * bm - pstart[bexp]
    j1 = jnp.minimum(j0 + bm, counts[bexp])
    cum_e = cum_full[:, bexp]
    tlo = jnp.sum(cum_e[1:] <= j0[None, :], axis=0).astype(jnp.int32)
    thi = jnp.sum(cum_e[:-1] < j1[None, :], axis=0).astype(jnp.int32)
    valid = blk < nact
    tlo = jnp.where(valid, jnp.minimum(tlo, thi), 0)
    thi = jnp.where(valid, thi, 0)
    lo_tab = (pstart[None, :] + cum_full[:-1]).reshape(-1).astype(jnp.int32)
    hi_tab = (pstart[None, :] + cum_full[1:]).reshape(-1).astype(jnp.int32)
    return dest.astype(jnp.int32), bexp, nact.reshape(1), tlo, thi, lo_tab, hi_tab


MOE_BLOCK_ROWS = 256
TOKEN_TILE = 256


def kernel(x, norm_mix_g, w_in, gla_gate_up, gla_gate_bias, gla_norm_g, w_branch_gla, w_branch_att, w_mix_out,
           norm_ffn_g, router_w, router_b, expert_w_gate_up, expert_b_gate_up, expert_w_down, expert_b_down,
           norm_final_g):
    batch, seq, d = x.shape
    assert norm_mix_g.shape[0] == 1, "single layer"
    n = batch * seq
    gla_k = gla_gate_up.shape[2]
    gla_v = w_branch_gla.shape[1]
    att_w = len(ATT_GROUPS) * ATT_HEADS_PER_GROUP * ATT_HEAD_DIM
    sizes = (('gq', gla_k), ('gk', gla_k), ('gv', gla_v), ('gr', gla_v), ('aq', att_w), ('ak', att_w),
             ('av', att_w), ('gate_gla', d), ('gate_att', d))
    layout, off = {}, 0
    for name, size in sizes:
        layout[name] = off
        off += size
    glr_at = layout['aq']
    w = w_in[0]
    w_main = jnp.concatenate([w[:, :glr_at], w[:, glr_at + GLA_GATE_RANK:]], axis=1).astype(BF16)
    w_glr = w[:, glr_at:glr_at + GLA_GATE_RANK]
    assert w_main.shape[1] == off

    x2 = x.reshape(n, d)
    proj, glr = _in_projection(x2, norm_mix_g[0], w_main, w_glr, seq, layout)
    y_gla = _gla(proj, glr, gla_gate_up[0], gla_gate_bias[0], gla_norm_g[0], batch, seq, layout)
    att = [_dilated_attention(proj, batch, seq, layout, g, dil) for g, (_, dil) in enumerate(ATT_GROUPS)]
    tt = min(TOKEN_TILE, n)
    h, hn, top_e, gates, rank, cum, cnt = _merge(
        x2, y_gla, att, proj, layout, w_branch_gla[0].astype(BF16), w_branch_att[0].astype(BF16),
        w_mix_out[0].astype(BF16), norm_ffn_g[0], router_w[0], router_b[0], tt)

    bm = MOE_BLOCK_ROWS
    nblk = (n * TOP_K) // bm + N_EXPERTS
    dest, bexp, nact, tlo, thi, lo_tab, hi_tab = _routing_tables(top_e, rank, cum[:, :, 0], cnt[:, 0], bm, nblk)
    x_pad = _dispatch(hn, dest, tlo, thi, nblk, bm, tt)
    y_pad = _experts(x_pad, bexp, nact, expert_w_gate_up[0], expert_b_gate_up[0], expert_w_down[0],
                     expert_b_down[0], bm)
    out = _combine(h, y_pad, dest, gates, lo_tab, hi_tab, norm_final_g, tt)
    return out.reshape(batch, seq, d)
```

```python
import functools
import math

import jax
import jax.numpy as jnp
import numpy as np
from jax import lax
from jax.experimental import pallas as pl
from jax.experimental.pallas import tpu as pltpu

GLA_HEADS = 4
GLA_GATE_RANK = 16
GLA_TAU = 16.0
GLA_CHUNK = 64
ATT_GROUPS = ((128, 1), (512, 4), (2048, 16))
ATT_HEADS_PER_GROUP = 8
ATT_HEAD_DIM = 64
ATT_BLOCK = 128
ROT_DIM = ATT_HEAD_DIM // 4
ROPE_THETA = 500000.0
N_EXPERTS = 32
TOP_K = 4
SWIGLU_LIMIT = 7.0
SWIGLU_ALPHA = 1.702
EPS = 1e-5

LANES = 128
MXU_DIM = 256
VMEM_LIMIT = 56 * 1024 * 1024

F32 = jnp.float32
BF16 = jnp.bfloat16


def _cparams(sem):
    return pltpu.CompilerParams(dimension_semantics=sem, vmem_limit_bytes=VMEM_LIMIT)


def _dot(a, b):
    return jnp.dot(a, b, preferred_element_type=F32)


def _dot_nt(a, b):
    return lax.dot_general(a, b, (((1,), (1,)), ((), ())), preferred_element_type=F32)


def _dot_tn(a, b):
    return lax.dot_general(a, b, (((0,), (0,)), ((), ())), preferred_element_type=F32)


def _split2(a):
    hi = a.astype(BF16)
    lo = (a - hi.astype(F32)).astype(BF16)
    return hi, lo


def _inproj_kernel(x_ref, g_ref, w_ref, wglr_ref, cos_ref, sa_ref, sb_ref, proj_ref, glr_ref, *rest,
                   n_plain, dilations, q_scale):
    dil_refs, (xn_ref, stage_ref) = rest[:len(dilations)], rest[len(dilations):]
    j = pl.program_id(1)

    @pl.when(j == 0)
    def _():
        x = x_ref[...]
        y = x * lax.rsqrt(jnp.mean(x * x, axis=-1, keepdims=True) + EPS) * g_ref[...]
        xn_ref[...] = y.astype(BF16)
        yh, yl = _split2(y)
        wh, wl = _split2(wglr_ref[...])
        glr_ref[...] = _dot(yh, wh) + _dot(yh, wl) + _dot(yl, wh)

    acc = _dot(xn_ref[...], w_ref[...])
    tm, tn = acc.shape
    half = ROT_DIM // 2

    def rotary(scale):
        c, sa, sb = cos_ref[...], sa_ref[...], sb_ref[...]
        for s in range(tn // LANES):
            t = acc[:, s * LANES:(s + 1) * LANES]
            out = t * c + pltpu.roll(t, LANES - half, 1) * sa + pltpu.roll(t, half, 1) * sb
            yield s, out if scale == 1.0 else out * scale

    q0 = n_plain - 3

    @pl.when(jnp.logical_and(j < n_plain, jnp.logical_and(j != q0, j != q0 + 1)))
    def _():
        proj_ref[...] = acc.astype(BF16)

    for kind, scale in ((0, q_scale), (1, 1.0)):
        @pl.when(j == q0 + kind)
        def _(scale=scale):
            for s, val in rotary(scale):
                proj_ref[:, s * LANES:(s + 1) * LANES] = val.astype(BF16)

    for gi, dil in enumerate(dilations):
        base = n_plain + 3 * gi
        out_ref = dil_refs[gi]

        def scatter(dil=dil, out_ref=out_ref):
            for r in range(dil):
                for s in range(tn // LANES):
                    out_ref[0, r, :, s * LANES:(s + 1) * LANES] = \
                        stage_ref[s, pl.ds(r, tm // dil, stride=dil), :].astype(BF16)

        for kind, scale in ((0, q_scale), (1, 1.0)):
            @pl.when(j == base + kind)
            def _(scale=scale, scatter=scatter):
                for s, val in rotary(scale):
                    stage_ref[s] = val
                scatter()

        @pl.when(j == base + 2)
        def _(scatter=scatter):
            for s in range(tn // LANES):
                stage_ref[s] = acc[:, s * LANES:(s + 1) * LANES]
            scatter()


def _rope_tables(seq):
    half = ROT_DIM // 2
    inv_freq = ROPE_THETA ** (-np.arange(half, dtype=np.float32) * np.float32(2.0 / ROT_DIM))
    pos = jnp.arange(seq, dtype=F32)
    ang = pos[:, None] * jnp.asarray(inv_freq, F32)[None, :]
    cos = jnp.cos(ang)
    sin = jnp.sin(ang)
    lane = np.arange(LANES) % ATT_HEAD_DIM
    idx = np.where(lane < ROT_DIM, lane % half, 0)
    in_rot = jnp.asarray(lane < ROT_DIM)
    first = jnp.asarray(lane < half)
    second = jnp.asarray((lane >= half) & (lane < ROT_DIM))
    cos_t = jnp.where(in_rot[None, :], cos[:, idx], 1.0)
    sin_g = sin[:, idx]
    sa = jnp.where(first[None, :], -sin_g, 0.0)
    sb = jnp.where(second[None, :], sin_g, 0.0)
    return cos_t.astype(F32), sa.astype(F32), sb.astype(F32)


def _in_projection(x2, norm_g, w_main, w_glr, batch, seq, n_plain, dilations):
    n, d = x2.shape
    tn = ATT_HEADS_PER_GROUP * ATT_HEAD_DIM
    tm = min(2048, seq)
    ntile = w_main.shape[1] // tn
    assert n % tm == 0 and seq % tm == 0 and ntile == n_plain + 3 * len(dilations)
    assert all(tm % (dil * 16) == 0 for dil in dilations)
    cos_t, sa, sb = _rope_tables(seq)
    spt = seq // tm
    kern = functools.partial(_inproj_kernel, n_plain=n_plain, dilations=tuple(dilations),
                             q_scale=ATT_HEAD_DIM ** -0.5)
    out_specs = [
        pl.BlockSpec((tm, tn), lambda i, j: (i, jnp.minimum(j, n_plain - 1))),
        pl.BlockSpec((tm, GLA_GATE_RANK), lambda i, j: (i, 0)),
    ]
    out_shape = [
        jax.ShapeDtypeStruct((n, n_plain * tn), BF16),
        jax.ShapeDtypeStruct((n, GLA_GATE_RANK), F32),
    ]
    for gi, dil in enumerate(dilations):
        base = n_plain + 3 * gi
        out_specs.append(pl.BlockSpec(
            (1, dil, tm // dil, tn),
            lambda i, j, base=base: (i // spt, 0, i % spt, jnp.clip(j - base, 0, 2))))
        out_shape.append(jax.ShapeDtypeStruct((batch, dil, seq // dil, 3 * tn), BF16))
    return pl.pallas_call(
        kern,
        grid=(n // tm, ntile),
        in_specs=[
            pl.BlockSpec((tm, d), lambda i, j: (i, 0), pipeline_mode=pl.Buffered(1)),
            pl.BlockSpec((1, d), lambda i, j: (0, 0)),
            pl.BlockSpec((d, tn), lambda i, j: (0, j)),
            pl.BlockSpec((d, GLA_GATE_RANK), lambda i, j: (0, 0)),
            pl.BlockSpec((tm, LANES), lambda i, j: (i % spt, 0), pipeline_mode=pl.Buffered(1)),
            pl.BlockSpec((tm, LANES), lambda i, j: (i % spt, 0), pipeline_mode=pl.Buffered(1)),
            pl.BlockSpec((tm, LANES), lambda i, j: (i % spt, 0), pipeline_mode=pl.Buffered(1)),
        ],
        out_specs=out_specs,
        out_shape=out_shape,
        scratch_shapes=[pltpu.VMEM((tm, d), BF16), pltpu.VMEM((tn // LANES, tm, LANES), F32)],
        compiler_params=_cparams(("arbitrary", "arbitrary")),
        name="in_projection",
    )(x2, norm_g.reshape(1, d), w_main, w_glr, cos_t, sa, sb)


def _gla_kernel(q_ref, k_ref, v_ref, r_ref, glr_ref, up_ref, bias_ref, ng_ref, tri_ref, y_ref, state_ref,
                *, dk, dv, ts):
    t = pl.program_id(2)

    @pl.when(t == 0)
    def _():
        state_ref[...] = jnp.zeros_like(state_ref)

    c = GLA_CHUNK
    gh, gl = _split2(glr_ref[...])
    uh, ul = _split2(up_ref[...])
    z = _dot(gh, uh) + _dot(gh, ul) + _dot(gl, uh) + bias_ref[...]
    logdec = (jnp.minimum(z, 0.0) - jnp.log1p(jnp.exp(-jnp.abs(z)))) * (1.0 / GLA_TAU)
    ldh, ldl = _split2(logdec)
    tri = tri_ref[...]
    row = lax.broadcasted_iota(jnp.int32, (c, c), 0)
    col = lax.broadcasted_iota(jnp.int32, (c, c), 1)
    causal = col <= row
    scale = dk ** -0.5
    ng = ng_ref[...]
    state = state_ref[...]
    for ci in range(ts // c):
        sl = slice(ci * c, (ci + 1) * c)
        b = _dot(tri, ldh[sl]) + _dot(tri, ldl[sl])
        bl = b[c - 1:c, :]
        eb = jnp.exp(b)
        enb = jnp.exp(-b)
        ebl = jnp.exp(bl)
        qf = q_ref[sl, :].astype(F32)
        kf = k_ref[sl, :].astype(F32)
        qe = (qf * scale * eb).astype(BF16)
        ke = (kf * enb).astype(BF16)
        kd = (kf * enb * ebl).astype(BF16)
        vb = v_ref[sl, :]
        a = jnp.where(causal, _dot_nt(qe, ke), 0.0).astype(BF16)
        o = _dot(a, vb) + _dot(qe, state.astype(BF16))
        dec = jnp.transpose(jnp.broadcast_to(ebl, (dk, dk)))
        dec_full = jnp.concatenate([dec] * (dv // dk), axis=1)
        state = dec_full * state + _dot_tn(kd, vb)
        o = o * lax.rsqrt(jnp.mean(o * o, axis=-1, keepdims=True) + EPS) * ng
        rf = r_ref[sl, :].astype(F32)
        o = o * (rf * jax.nn.sigmoid(rf))
        y_ref[sl, :] = o.astype(BF16)
    state_ref[...] = state


def _gla(proj, glr, gate_up, gate_bias, norm_g, batch, seq, layout):
    n = proj.shape[0]
    dk = (layout['gk'] - layout['gq']) // GLA_HEADS
    dv = (layout['gr'] - layout['gv']) // GLA_HEADS
    ts = min(512, seq)
    assert dk == LANES and dv % dk == 0 and seq % ts == 0 and ts % GLA_CHUNK == 0
    spt = seq // ts
    qb, kb = layout['gq'] // dk, layout['gk'] // dk
    vb, rb = layout['gv'] // dv, layout['gr'] // dv
    tri = jnp.asarray(np.tril(np.ones((GLA_CHUNK, GLA_CHUNK), np.float32)), BF16)
    kern = functools.partial(_gla_kernel, dk=dk, dv=dv, ts=ts)
    row = lambda b, h, t: b * spt + t
    return pl.pallas_call(
        kern,
        grid=(batch, GLA_HEADS, spt),
        in_specs=[
            pl.BlockSpec((ts, dk), lambda b, h, t: (row(b, h, t), qb + h)),
            pl.BlockSpec((ts, dk), lambda b, h, t: (row(b, h, t), kb + h)),
            pl.BlockSpec((ts, dv), lambda b, h, t: (row(b, h, t), vb + h)),
            pl.BlockSpec((ts, dv), lambda b, h, t: (row(b, h, t), rb + h)),
            pl.BlockSpec((ts, GLA_GATE_RANK), lambda b, h, t: (row(b, h, t), 0)),
            pl.BlockSpec((GLA_GATE_RANK, dk), lambda b, h, t: (0, h)),
            pl.BlockSpec((1, dk), lambda b, h, t: (0, h)),
            pl.BlockSpec((1, dv), lambda b, h, t: (0, 0)),
            pl.BlockSpec((GLA_CHUNK, GLA_CHUNK), lambda b, h, t: (0, 0)),
        ],
        out_specs=pl.BlockSpec((ts, dv), lambda b, h, t: (row(b, h, t), h)),
        out_shape=jax.ShapeDtypeStruct((n, GLA_HEADS * dv), BF16),
        scratch_shapes=[pltpu.VMEM((dk, dv), F32)],
        compiler_params=_cparams(("arbitrary", "arbitrary", "arbitrary")),
        name="gla",
    )(proj, proj, proj, proj, glr, gate_up, gate_bias.reshape(1, -1), norm_g.reshape(1, -1), tri)


def _att_kernel(q_ref, k_ref, v_ref, o_ref, lse_ref, *, nb, pairs, unroll):
    blk = ATT_BLOCK
    hd = ATT_HEAD_DIM
    row = lax.broadcasted_iota(jnp.int32, (2 * blk, 2 * blk), 0) % blk
    col = lax.broadcasted_iota(jnp.int32, (2 * blk, 2 * blk), 1)
    dist = row + blk - col
    band = jnp.logical_and(dist >= 0, dist <= blk)
    causal = band[:, blk:]
    head0 = lax.broadcasted_iota(jnp.int32, (blk, LANES), 1) < hd
    neg = -jnp.inf

    def block(j, first):
        r0 = pl.multiple_of(j * blk, blk)
        for p in range(pairs):
            cs = slice(p * LANES, (p + 1) * LANES)
            q2 = q_ref[0, 0, pl.ds(r0, blk), cs]
            zero = jnp.zeros_like(q2)
            qs = jnp.concatenate([jnp.where(head0, q2, zero), jnp.where(head0, zero, q2)], axis=0)
            if first:
                kk = k_ref[0, 0, pl.ds(r0, blk), cs]
                vv = v_ref[0, 0, pl.ds(r0, blk), cs]
                s = jnp.where(causal, _dot_nt(qs, kk), neg)
            else:
                k0 = pl.multiple_of(r0 - blk, blk)
                kk = k_ref[0, 0, pl.ds(k0, 2 * blk), cs]
                vv = v_ref[0, 0, pl.ds(k0, 2 * blk), cs]
                s = jnp.where(band, _dot_nt(qs, kk), neg)
            mx = jnp.max(s, axis=-1, keepdims=True)
            e = jnp.exp(s - mx).astype(BF16)
            one = jnp.ones_like(vv)
            hv = lax.broadcasted_iota(jnp.int32, vv.shape, 1) < hd
            out0 = _dot(e[:blk], jnp.where(hv, vv, one))
            out1 = _dot(e[blk:], jnp.where(hv, one, vv))
            num = jnp.where(head0, out0, out1)
            den = pltpu.roll(jnp.where(head0, out1, out0), hd, 1)
            o_ref[0, 0, pl.ds(r0, blk), cs] = (num / den).astype(o_ref.dtype)
            mxb = jnp.where(head0, mx[:blk], mx[blk:])
            lse_ref[0, 0, pl.ds(r0, blk), cs] = mxb + jnp.log(den)

    block(0, True)

    def body(j, carry):
        block(j, False)
        return carry

    lax.fori_loop(1, nb, body, 0, unroll=unroll)


def _dilated_attention(qkv, dilation, col0):
    batch, dil, sub, width = qkv.shape
    hw = ATT_HEADS_PER_GROUP * ATT_HEAD_DIM
    assert dil == dilation and sub % ATT_BLOCK == 0 and col0 % hw == 0
    nb = sub // ATT_BLOCK
    pairs = 1 if sub >= 4096 else hw // LANES
    unroll = max(1, min(4 // pairs, nb - 1))
    wpb = pairs * LANES
    steps = hw // wpb
    kern = functools.partial(_att_kernel, nb=nb, pairs=pairs, unroll=unroll)

    def in_spec(kind):
        base = (col0 + kind * hw) // wpb
        return pl.BlockSpec((1, 1, sub, wpb), lambda b, r, p: (b, r, 0, base + p))

    out_spec = pl.BlockSpec((1, 1, sub, wpb), lambda b, r, p: (b, r, 0, p))
    return pl.pallas_call(
        kern,
        grid=(batch, dil, steps),
        in_specs=[in_spec(0), in_spec(1), in_spec(2)],
        out_specs=[out_spec, out_spec],
        out_shape=[
            jax.ShapeDtypeStruct((batch, dil, sub, hw), BF16),
            jax.ShapeDtypeStruct((batch, dil, sub, hw), F32),
        ],
        compiler_params=_cparams(("arbitrary", "arbitrary", "arbitrary")),
        name=f"dilated_attention_d{dilation}",
    )(qkv, qkv, qkv)


def _merge_kernel(x_ref, yg_ref, o1_ref, o2_ref, o3_ref, l1_ref, l2_ref, l3_ref,
                  gg0_ref, gg1_ref, ga0_ref, ga1_ref, wbg_ref, wba_ref, wmo_ref, nf_ref, rw_ref, rb_ref, triu_ref,
                  h_ref, hn_ref, te_ref, tg_ref, rk_ref, cum_ref, cnt_ref, carry_ref, os_ref, ls_ref):
    i = pl.program_id(0)
    ne = N_EXPERTS

    @pl.when(i == 0)
    def _():
        carry_ref[...] = jnp.zeros_like(carry_ref)

    def token_order(src_ref, stage_ref):
        dil, sub, w = src_ref.shape[1:]
        if dil == 1:
            return src_ref[0, 0].astype(F32)
        for r in range(dil):
            val = src_ref[0, r].astype(F32)
            for s in range(w // LANES):
                stage_ref[s, pl.ds(r, sub, stride=dil), :] = val[:, s * LANES:(s + 1) * LANES]
        return jnp.concatenate([stage_ref[s] for s in range(w // LANES)], axis=1)

    l1, o1 = token_order(l1_ref, None), token_order(o1_ref, None)
    l2, o2 = token_order(l2_ref, ls_ref.at[0]), token_order(o2_ref, os_ref.at[0])
    l3, o3 = token_order(l3_ref, ls_ref.at[1]), token_order(o3_ref, os_ref.at[1])
    m = jnp.maximum(jnp.maximum(l1, l2), l3)
    w1, w2, w3 = jnp.exp(l1 - m), jnp.exp(l2 - m), jnp.exp(l3 - m)
    y_att = (w1 * o1 + w2 * o2 + w3 * o3) / (w1 + w2 + w3)

    m_gla = _dot(yg_ref[...], wbg_ref[...])
    m_att = _dot(y_att.astype(BF16), wba_ref[...])
    gg = jnp.concatenate([gg0_ref[...], gg1_ref[...]], axis=1).astype(F32)
    ga = jnp.concatenate([ga0_ref[...], ga1_ref[...]], axis=1).astype(F32)
    merged = jax.nn.sigmoid(gg) * m_gla + jax.nn.sigmoid(ga) * m_att
    h = x_ref[...] + _dot(merged.astype(BF16), wmo_ref[...])
    h_ref[...] = h
    hn = h * lax.rsqrt(jnp.mean(h * h, axis=-1, keepdims=True) + EPS) * nf_ref[...]
    hn_ref[...] = hn.astype(BF16)

    hh, hl = _split2(hn)
    rh, rl = _split2(rw_ref[...])
    logit = _dot_nt(rh, hh) + _dot_nt(rh, hl) + _dot_nt(rl, hh) + rb_ref[...]
    tm = logit.shape[1]
    eid = lax.broadcasted_iota(jnp.int32, (ne, tm), 0)
    member = jnp.zeros((ne, tm), jnp.bool_)
    vals, idxs = [], []
    for _ in range(TOP_K):
        mx = jnp.max(logit, axis=0, keepdims=True)
        idx = jnp.min(jnp.where(logit == mx, eid, ne), axis=0, keepdims=True)
        sel = eid == idx
        member = jnp.logical_or(member, sel)
        logit = jnp.where(sel, -jnp.inf, logit)
        vals.append(mx)
        idxs.append(idx)
    ex = [jnp.exp(v - vals[0]) for v in vals]
    tot = ex[0] + ex[1] + ex[2] + ex[3]
    memf = jnp.where(member, 1.0, 0.0)
    carry = carry_ref[...]
    pos = _dot(memf.astype(BF16), triu_ref[...]) + carry[:, 0:1]
    for k in range(TOP_K):
        te_ref[k:k + 1, :] = idxs[k]
        tg_ref[k:k + 1, :] = ex[k] / tot
        rk_ref[k:k + 1, :] = jnp.sum(jnp.where(eid == idxs[k], pos, 0.0), axis=0, keepdims=True).astype(jnp.int32)
    cum_ref[0] = carry.astype(jnp.int32)
    carry = carry + jnp.sum(memf, axis=1, keepdims=True)
    carry_ref[...] = carry
    cnt_ref[...] = carry.astype(jnp.int32)


def _merge(x2, y_gla, att, proj, layout, wbg, wba, wmo, norm_ffn_g, router_w, router_b, seq, tm):
    n, d = x2.shape
    hw = ATT_HEADS_PER_GROUP * ATT_HEAD_DIM
    half = d // 2
    assert n % tm == 0 and seq % tm == 0
    assert layout['gate_gla'] % half == 0 and layout['gate_att'] % half == 0
    assert all(tm % (o.shape[1] * 16) == 0 for o, _ in att) and att[0][0].shape[1] == 1
    nt = n // tm
    spt = seq // tm
    ggb = layout['gate_gla'] // half
    gab = layout['gate_att'] // half
    triu = jnp.asarray(np.triu(np.ones((tm, tm), np.float32), 1), BF16)
    row = lambda i: (i, 0)
    const = lambda i: (0, 0)

    def att_spec(a):
        dil = a.shape[1]
        return pl.BlockSpec((1, dil, tm // dil, hw), lambda i: (i // spt, 0, i % spt, 0))

    (o1, l1), (o2, l2), (o3, l3) = att
    in_specs = [
        pl.BlockSpec((tm, d), row),
        pl.BlockSpec((tm, y_gla.shape[1]), row),
        att_spec(o1), att_spec(o2), att_spec(o3), att_spec(l1), att_spec(l2), att_spec(l3),
        pl.BlockSpec((tm, half), lambda i: (i, ggb)), pl.BlockSpec((tm, half), lambda i: (i, ggb + 1)),
        pl.BlockSpec((tm, half), lambda i: (i, gab)), pl.BlockSpec((tm, half), lambda i: (i, gab + 1)),
        pl.BlockSpec(wbg.shape, const), pl.BlockSpec(wba.shape, const), pl.BlockSpec(wmo.shape, const),
        pl.BlockSpec((1, d), const),
        pl.BlockSpec((N_EXPERTS, d), const),
        pl.BlockSpec((N_EXPERTS, 1), const),
        pl.BlockSpec((tm, tm), const),
    ]
    out_specs = [
        pl.BlockSpec((tm, d), row),
        pl.BlockSpec((tm, d), row),
        pl.BlockSpec((TOP_K, tm), lambda i: (0, i)),
        pl.BlockSpec((TOP_K, tm), lambda i: (0, i)),
        pl.BlockSpec((TOP_K, tm), lambda i: (0, i)),
        pl.BlockSpec((1, N_EXPERTS, LANES), lambda i: (i, 0, 0)),
        pl.BlockSpec((N_EXPERTS, LANES), const),
    ]
    out_shape = [
        jax.ShapeDtypeStruct((n, d), F32),
        jax.ShapeDtypeStruct((n, d), BF16),
        jax.ShapeDtypeStruct((TOP_K, n), jnp.int32),
        jax.ShapeDtypeStruct((TOP_K, n), F32),
        jax.ShapeDtypeStruct((TOP_K, n), jnp.int32),
        jax.ShapeDtypeStruct((nt, N_EXPERTS, LANES), jnp.int32),
        jax.ShapeDtypeStruct((N_EXPERTS, LANES), jnp.int32),
    ]
    return pl.pallas_call(
        _merge_kernel,
        grid=(nt,),
        in_specs=in_specs,
        out_specs=out_specs,
        out_shape=out_shape,
        scratch_shapes=[pltpu.VMEM((N_EXPERTS, LANES), F32), pltpu.VMEM((2, hw // LANES, tm, LANES), F32),
                        pltpu.VMEM((2, hw // LANES, tm, LANES), F32)],
        compiler_params=_cparams(("arbitrary",)),
        name="merge_router",
    )(x2, y_gla, o1, o2, o3, l1, l2, l3, proj, proj, proj, proj, wbg, wba, wmo,
      norm_ffn_g.reshape(1, d), router_w.T, router_b.reshape(N_EXPERTS, 1), triu)


def _dispatch_kernel(tlo_ref, thi_ref, hn_ref, dest_ref, out_ref, *, bm, tt):
    b = pl.program_id(0)
    row_id = b * bm + lax.broadcasted_iota(jnp.int32, (bm, tt), 0)
    out_ref[...] = jnp.zeros_like(out_ref)

    def body(t, carry):
        d4 = dest_ref[t]
        hit = row_id == d4[0:1, :]
        for k in range(1, TOP_K):
            hit = jnp.logical_or(hit, row_id == d4[k:k + 1, :])
        onehot = jnp.where(hit, 1.0, 0.0).astype(BF16)
        out_ref[...] += _dot(onehot, hn_ref[t]).astype(BF16)
        return carry

    lax.fori_loop(tlo_ref[b], thi_ref[b], body, 0)


def _dispatch(hn, dest, tlo, thi, nblk, bm, tt):
    n, d = hn.shape
    nt = n // tt
    kern = functools.partial(_dispatch_kernel, bm=bm, tt=tt)
    grid_spec = pltpu.PrefetchScalarGridSpec(
        num_scalar_prefetch=2,
        grid=(nblk,),
        in_specs=[
            pl.BlockSpec(memory_space=pltpu.VMEM),
            pl.BlockSpec(memory_space=pltpu.VMEM),
        ],
        out_specs=pl.BlockSpec((bm, d), lambda b, tlo, thi: (b, 0)),
    )
    return pl.pallas_call(
        kern,
        grid_spec=grid_spec,
        out_shape=jax.ShapeDtypeStruct((nblk * bm, d), BF16),
        compiler_params=_cparams(("arbitrary",)),
        name="moe_dispatch",
    )(tlo, thi, hn.reshape(nt, tt, d), dest.reshape(TOP_K, nt, tt).transpose(1, 0, 2))


def _expert_kernel(bexp_ref, nact_ref, x_ref, wgu_ref, wd_ref, bgu_ref, bd_ref, perm_ref, y_ref, wgu_s, wd_s):
    i = pl.program_id(0)
    e = bexp_ref[i]
    prev = bexp_ref[jnp.maximum(i - 1, 0)]
    active = i < nact_ref[0]
    grp = MXU_DIM
    ff2 = wgu_s.shape[1]
    halfg = grp // 2

    @pl.when(jnp.logical_and(active, jnp.logical_or(i == 0, e != prev)))
    def _():
        perm = perm_ref[...]
        for g in range(ff2 // grp):
            cols = slice(g * grp, (g + 1) * grp)
            wgu_s[:, cols] = _dot(wgu_ref[0, :, cols].astype(BF16), perm).astype(BF16)
        wd_s[...] = wd_ref[0].astype(BF16)

    @pl.when(active)
    def _():
        x = x_ref[...]
        acts = []
        for g in range(ff2 // grp):
            cols = slice(g * grp, (g + 1) * grp)
            hg = _dot(x, wgu_s[:, cols]) + bgu_ref[0, :, cols]
            gate = jnp.minimum(hg[:, :halfg], SWIGLU_LIMIT)
            up = jnp.clip(hg[:, halfg:], -SWIGLU_LIMIT, SWIGLU_LIMIT)
            acts.append(((up + 1.0) * (gate * jax.nn.sigmoid(SWIGLU_ALPHA * gate))).astype(BF16))
        act = jnp.concatenate(acts, axis=1)
        y_ref[...] = (_dot(act, wd_s[...]) + bd_ref[0]).astype(y_ref.dtype)

    @pl.when(jnp.logical_not(active))
    def _():
        y_ref[...] = jnp.zeros_like(y_ref)


def _gate_up_permutation():
    grp = MXU_DIM
    halfg = grp // 2
    out = np.arange(grp)
    src = np.where(out < halfg, 2 * out, 2 * (out - halfg) + 1)
    p = np.zeros((grp, grp), np.float32)
    p[src, out] = 1.0
    return jnp.asarray(p, BF16)


def _experts(x_pad, bexp, nact, w_gate_up, b_gate_up, w_down, b_down, bm):
    p_rows, d = x_pad.shape
    ne, _, ff2 = w_gate_up.shape
    ff = ff2 // 2
    grp = MXU_DIM
    assert ff2 % grp == 0
    nblk = p_rows // bm
    bgu = b_gate_up.reshape(ne, ff2 // grp, grp // 2, 2).transpose(0, 1, 3, 2).reshape(ne, 1, ff2)
    grid_spec = pltpu.PrefetchScalarGridSpec(
        num_scalar_prefetch=2,
        grid=(nblk,),
        in_specs=[
            pl.BlockSpec((bm, d), lambda i, be, na: (i, 0)),
            pl.BlockSpec((1, d, ff2), lambda i, be, na: (be[i], 0, 0)),
            pl.BlockSpec((1, ff, d), lambda i, be, na: (be[i], 0, 0)),
            pl.BlockSpec((1, 1, ff2), lambda i, be, na: (be[i], 0, 0)),
            pl.BlockSpec((1, 1, d), lambda i, be, na: (be[i], 0, 0)),
            pl.BlockSpec((grp, grp), lambda i, be, na: (0, 0)),
        ],
        out_specs=pl.BlockSpec((bm, d), lambda i, be, na: (i, 0)),
        scratch_shapes=[pltpu.VMEM((d, ff2), BF16), pltpu.VMEM((ff, d), BF16)],
    )
    return pl.pallas_call(
        _expert_kernel,
        grid_spec=grid_spec,
        out_shape=jax.ShapeDtypeStruct((p_rows, d), BF16),
        compiler_params=_cparams(("arbitrary",)),
        name="moe_experts",
    )(bexp, nact, x_pad, w_gate_up, w_down, bgu, b_down.reshape(ne, 1, d), _gate_up_permutation())


COMBINE_CHUNK = 32


def _combine_kernel(lo_ref, hi_ref, h_ref, dest_ref, gate_ref, g_ref, y_hbm, out_ref, ybuf, base_ref, sem,
                    *, tt, max_slots):
    i = pl.program_id(0)
    ch = COMBINE_CHUNK
    kg = MXU_DIM
    spg = kg // ch

    def window_copy(chunk, slot):
        return pltpu.make_async_copy(y_hbm.at[pl.ds(chunk * ch, ch), :], ybuf.at[pl.ds(slot * ch, ch), :], sem)

    def per_expert(e, slot):
        lo = lo_ref[i * N_EXPERTS + e]
        hi = hi_ref[i * N_EXPERTS + e]
        c0 = lo // ch
        c1 = jnp.where(hi > lo, (hi + ch - 1) // ch, c0)

        def per_chunk(c, s):
            window_copy(c, s).start()
            base_ref[s] = c * ch
            return s + 1

        return lax.fori_loop(c0, c1, per_chunk, slot)

    nslots = lax.fori_loop(0, N_EXPERTS, per_expert, 0)
    ngroups = (nslots + spg - 1) // spg

    def clear(s, carry):
        ybuf[pl.ds(s * ch, ch), :] = jnp.zeros((ch, ybuf.shape[1]), ybuf.dtype)
        base_ref[s] = -(1 << 20)
        return carry

    lax.fori_loop(nslots, ngroups * spg, clear, 0)

    def wait_one(s, carry):
        window_copy(0, s).wait()
        return carry

    lax.fori_loop(0, nslots, wait_one, 0)

    dest = dest_ref[0]
    gate = gate_ref[0]
    sub = lax.broadcasted_iota(jnp.int32, (ch, tt), 0)

    def per_group(g, acc):
        parts = []
        for s in range(spg):
            rid = base_ref[g * spg + s] + sub
            w = jnp.zeros((ch, tt), F32)
            for k in range(TOP_K):
                w = w + jnp.where(rid == dest[k:k + 1, :], gate[k:k + 1, :], 0.0)
            parts.append(w)
        gt = jnp.concatenate(parts, axis=0).astype(BF16)
        r0 = pl.multiple_of(g * kg, kg)
        return acc + _dot_tn(gt, ybuf[pl.ds(r0, kg), :])

    moe = lax.fori_loop(0, ngroups, per_group, jnp.zeros((tt, out_ref.shape[1]), F32))
    h = h_ref[...] + moe
    out_ref[...] = h * lax.rsqrt(jnp.mean(h * h, axis=-1, keepdims=True) + EPS) * g_ref[...]


def _combine(h, y_pad, dest, gates, lo_tab, hi_tab, norm_g, tt):
    n, d = h.shape
    nt = n // tt
    max_slots = (TOP_K * tt) // COMBINE_CHUNK + 2 * N_EXPERTS
    max_slots = -(-max_slots // (MXU_DIM // COMBINE_CHUNK)) * (MXU_DIM // COMBINE_CHUNK)
    kern = functools.partial(_combine_kernel, tt=tt, max_slots=max_slots)
    grid_spec = pltpu.PrefetchScalarGridSpec(
        num_scalar_prefetch=2,
        grid=(nt,),
        in_specs=[
            pl.BlockSpec((tt, d), lambda i, lo, hi: (i, 0)),
            pl.BlockSpec((1, TOP_K, tt), lambda i, lo, hi: (i, 0, 0)),
            pl.BlockSpec((1, TOP_K, tt), lambda i, lo, hi: (i, 0, 0)),
            pl.BlockSpec((1, d), lambda i, lo, hi: (0, 0)),
            pl.BlockSpec(memory_space=pl.ANY),
        ],
        out_specs=pl.BlockSpec((tt, d), lambda i, lo, hi: (i, 0)),
        scratch_shapes=[
            pltpu.VMEM((max_slots * COMBINE_CHUNK, d), BF16),
            pltpu.SMEM((max_slots,), jnp.int32),
            pltpu.SemaphoreType.DMA(()),
        ],
    )
    return pl.pallas_call(
        kern,
        grid_spec=grid_spec,
        out_shape=jax.ShapeDtypeStruct((n, d), F32),
        compiler_params=_cparams(("arbitrary",)),
        name="moe_combine",
    )(lo_tab, hi_tab, h, dest.reshape(TOP_K, nt, tt).transpose(1, 0, 2),
      gates.reshape(TOP_K, nt, tt).transpose(1, 0, 2), norm_g.reshape(1, d), y_pad)


def _routing_tables(top_e, rank, cum, counts, bm, nblk):
    nt = cum.shape[0]
    padded = (counts + bm - 1) // bm * bm
    pend = jnp.cumsum(padded)
    pstart = pend - padded
    eid = jnp.arange(N_EXPERTS, dtype=jnp.int32)[:, None, None]
    dest = jnp.sum(jnp.where(top_e[None] == eid, pstart[:, None, None], 0), axis=0) + rank
    blk = jnp.arange(nblk, dtype=jnp.int32)
    bexp_raw = jnp.sum((pend // bm)[None, :] <= blk[:, None], axis=1).astype(jnp.int32)
    nact = (pend[-1] // bm).astype(jnp.int32)
    last = jnp.minimum(bexp_raw[jnp.maximum(nact - 1, 0)], N_EXPERTS - 1)
    bexp = jnp.where(blk < nact, jnp.minimum(bexp_raw, N_EXPERTS - 1), last)
    cum_full = jnp.concatenate([cum, counts[None, :]], axis=0)
    j0 = blk * bm - pstart[bexp]
    j1 = jnp.minimum(j0 + bm, counts[bexp])
    cum_e = cum_full[:, bexp]
    tlo = jnp.sum(cum_e[1:] <= j0[None, :], axis=0).astype(jnp.int32)
    thi = jnp.sum(cum_e[:-1] < j1[None, :], axis=0).astype(jnp.int32)
    valid = blk < nact
    tlo = jnp.where(valid, jnp.minimum(tlo, thi), 0)
    thi = jnp.where(valid, thi, 0)
    lo_tab = (pstart[None, :] + cum_full[:-1]).reshape(-1).astype(jnp.int32)
    hi_tab = (pstart[None, :] + cum_full[1:]).reshape(-1).astype(jnp.int32)
    return dest.astype(jnp.int32), bexp, nact.reshape(1), tlo, thi, lo_tab, hi_tab


MOE_BLOCK_ROWS = 256
TOKEN_TILE = 256


def kernel(x, norm_mix_g, w_in, gla_gate_up, gla_gate_bias, gla_norm_g, w_branch_gla, w_branch_att, w_mix_out,
           norm_ffn_g, router_w, router_b, expert_w_gate_up, expert_b_gate_up, expert_w_down, expert_b_down,
           norm_final_g):
    batch, seq, d = x.shape
    assert norm_mix_g.shape[0] == 1, "single layer"
    n = batch * seq
    gla_k = gla_gate_up.shape[2]
    gla_v = w_branch_gla.shape[1]
    att_w = len(ATT_GROUPS) * ATT_HEADS_PER_GROUP * ATT_HEAD_DIM
    hw = ATT_HEADS_PER_GROUP * ATT_HEAD_DIM
    ngrp = len(ATT_GROUPS)
    assert att_w == ngrp * hw and all(s % hw == 0 for s in (gla_k, gla_v, d))
    assert ATT_GROUPS[0][1] == 1 and all(win // dil == ATT_BLOCK for win, dil in ATT_GROUPS)
    src, off = {}, 0
    for name, size in (('gq', gla_k), ('gk', gla_k), ('gv', gla_v), ('gr', gla_v), ('glr', GLA_GATE_RANK),
                       ('aq', att_w), ('ak', att_w), ('av', att_w), ('gate_gla', d), ('gate_att', d)):
        src[name] = (off, size)
        off += size
    assert off == w_in.shape[2]
    w = w_in[0]

    def cols(name, group=None):
        start, size = src[name]
        return w[:, start:start + size] if group is None else w[:, start + group * hw:start + (group + 1) * hw]

    plain = [cols('gq'), cols('gk'), cols('gv'), cols('gr'), cols('gate_gla'), cols('gate_att'),
             cols('aq', 0), cols('ak', 0), cols('av', 0)]
    layout, off = {}, 0
    for name, part in zip(('gq', 'gk', 'gv', 'gr', 'gate_gla', 'gate_att', 'aq', 'ak', 'av'), plain):
        layout[name] = off
        off += part.shape[1]
    n_plain = off // hw
    dilated = [cols(kind, g) for g in range(1, ngrp) for kind in ('aq', 'ak', 'av')]
    w_main = jnp.concatenate(plain + dilated, axis=1).astype(BF16)
    w_glr = cols('glr')
    dilations = [dil for _, dil in ATT_GROUPS[1:]]

    x2 = x.reshape(n, d)
    proj, glr, *qkv_dil = _in_projection(x2, norm_mix_g[0], w_main, w_glr, batch, seq, n_plain, dilations)
    y_gla = _gla(proj, glr, gla_gate_up[0], gla_gate_bias[0], gla_norm_g[0], batch, seq, layout)
    att = [_dilated_attention(proj.reshape(batch, 1, seq, proj.shape[1]), 1, layout['aq'])]
    att += [_dilated_attention(a, dil, 0) for a, dil in zip(qkv_dil, dilations)]
    tt = min(TOKEN_TILE, n)
    h, hn, top_e, gates, rank, cum, cnt = _merge(
        x2, y_gla, att, proj, layout, w_branch_gla[0].astype(BF16), w_branch_att[0].astype(BF16),
        w_mix_out[0].astype(BF16), norm_ffn_g[0], router_w[0], router_b[0], seq, tt)

    bm = MOE_BLOCK_ROWS
    nblk = (n * TOP_K) // bm + N_EXPERTS
    dest, bexp, nact, tlo, thi, lo_tab, hi_tab = _routing_tables(top_e, rank, cum[:, :, 0], cnt[:, 0], bm, nblk)
    x_pad = _dispatch(hn, dest, tlo, thi, nblk, bm, tt)
    y_pad = _experts(x_pad, bexp, nact, expert_w_gate_up[0], expert_b_gate_up[0], expert_w_down[0],
                     expert_b_down[0], bm)
    out = _combine(h, y_pad, dest, gates, lo_tab, hi_tab, norm_final_g, tt)
    return out.reshape(batch, seq, d)
```

```python
import functools
import math

import jax
import jax.numpy as jnp
import numpy as np
from jax import lax
from jax.experimental import pallas as pl
from jax.experimental.pallas import tpu as pltpu

GLA_HEADS = 4
GLA_GATE_RANK = 16
GLA_TAU = 16.0
GLA_CHUNK = 64
ATT_GROUPS = ((128, 1), (512, 4), (2048, 16))
ATT_HEADS_PER_GROUP = 8
ATT_HEAD_DIM = 64
ATT_BLOCK = 128
ROT_DIM = ATT_HEAD_DIM // 4
ROPE_THETA = 500000.0
N_EXPERTS = 32
TOP_K = 4
SWIGLU_LIMIT = 7.0
SWIGLU_ALPHA = 1.702
EPS = 1e-5

LANES = 128
MXU_DIM = 256
VMEM_LIMIT = 56 * 1024 * 1024

F32 = jnp.float32
BF16 = jnp.bfloat16


def _cparams(sem):
    return pltpu.CompilerParams(dimension_semantics=sem, vmem_limit_bytes=VMEM_LIMIT)


def _dot(a, b):
    return jnp.dot(a, b, preferred_element_type=F32)


def _dot_nt(a, b):
    return lax.dot_general(a, b, (((1,), (1,)), ((), ())), preferred_element_type=F32)


def _dot_tn(a, b):
    return lax.dot_general(a, b, (((0,), (0,)), ((), ())), preferred_element_type=F32)


def _split2(a):
    hi = a.astype(BF16)
    lo = (a - hi.astype(F32)).astype(BF16)
    return hi, lo


def _inproj_kernel(x_ref, g_ref, w_ref, wglr_ref, cos_ref, sa_ref, sb_ref, proj_ref, glr_ref, *rest,
                   n_plain, dilations, q_scale):
    dil_refs, (xn_ref, stage_ref) = rest[:len(dilations)], rest[len(dilations):]
    j = pl.program_id(1)

    @pl.when(j == 0)
    def _():
        x = x_ref[...]
        y = x * lax.rsqrt(jnp.mean(x * x, axis=-1, keepdims=True) + EPS) * g_ref[...]
        xn_ref[...] = y.astype(BF16)
        yh, yl = _split2(y)
        wh, wl = _split2(wglr_ref[...])
        glr_ref[...] = _dot(yh, wh) + _dot(yh, wl) + _dot(yl, wh)

    tm, tn = xn_ref.shape[0], w_ref.shape[1]
    half = ROT_DIM // 2

    def matmul():
        return _dot(xn_ref[...], w_ref[...])

    def rotary(scale):
        acc = matmul()
        c, sa, sb = cos_ref[...], sa_ref[...], sb_ref[...]
        for s in range(tn // LANES):
            t = acc[:, s * LANES:(s + 1) * LANES]
            out = t * c + pltpu.roll(t, LANES - half, 1) * sa + pltpu.roll(t, half, 1) * sb
            yield s, out if scale == 1.0 else out * scale

    q0 = n_plain - 3

    @pl.when(jnp.logical_and(j < n_plain, jnp.logical_and(j != q0, j != q0 + 1)))
    def _():
        proj_ref[...] = matmul().astype(BF16)

    for kind, scale in ((0, q_scale), (1, 1.0)):
        @pl.when(j == q0 + kind)
        def _(scale=scale):
            for s, val in rotary(scale):
                proj_ref[:, s * LANES:(s + 1) * LANES] = val.astype(BF16)

    for gi, dil in enumerate(dilations):
        base = n_plain + 3 * gi
        out_ref = dil_refs[gi]

        def scatter(dil=dil, out_ref=out_ref):
            for r in range(dil):
                for s in range(tn // LANES):
                    out_ref[0, r, :, s * LANES:(s + 1) * LANES] = \
                        stage_ref[s, pl.ds(r, tm // dil, stride=dil), :].astype(BF16)

        for kind, scale in ((0, q_scale), (1, 1.0)):
            @pl.when(j == base + kind)
            def _(scale=scale, scatter=scatter):
                for s, val in rotary(scale):
                    stage_ref[s] = val
                scatter()

        @pl.when(j == base + 2)
        def _(scatter=scatter):
            acc = matmul()
            for s in range(tn // LANES):
                stage_ref[s] = acc[:, s * LANES:(s + 1) * LANES]
            scatter()


def _rope_tables(seq):
    half = ROT_DIM // 2
    inv_freq = ROPE_THETA ** (-np.arange(half, dtype=np.float32) * np.float32(2.0 / ROT_DIM))
    pos = jnp.arange(seq, dtype=F32)
    ang = pos[:, None] * jnp.asarray(inv_freq, F32)[None, :]
    cos = jnp.cos(ang)
    sin = jnp.sin(ang)
    lane = np.arange(LANES) % ATT_HEAD_DIM
    idx = np.where(lane < ROT_DIM, lane % half, 0)
    in_rot = jnp.asarray(lane < ROT_DIM)
    first = jnp.asarray(lane < half)
    second = jnp.asarray((lane >= half) & (lane < ROT_DIM))
    cos_t = jnp.where(in_rot[None, :], cos[:, idx], 1.0)
    sin_g = sin[:, idx]
    sa = jnp.where(first[None, :], -sin_g, 0.0)
    sb = jnp.where(second[None, :], sin_g, 0.0)
    return cos_t.astype(F32), sa.astype(F32), sb.astype(F32)


def _in_projection(x2, norm_g, w_main, w_glr, batch, seq, n_plain, dilations):
    n, d = x2.shape
    tn = ATT_HEADS_PER_GROUP * ATT_HEAD_DIM
    tm = min(2048, seq)
    ntile = w_main.shape[1] // tn
    assert n % tm == 0 and seq % tm == 0 and ntile == n_plain + 3 * len(dilations)
    assert all(tm % (dil * 16) == 0 for dil in dilations)
    cos_t, sa, sb = _rope_tables(seq)
    spt = seq // tm
    kern = functools.partial(_inproj_kernel, n_plain=n_plain, dilations=tuple(dilations),
                             q_scale=ATT_HEAD_DIM ** -0.5)
    out_specs = [
        pl.BlockSpec((tm, tn), lambda i, j: (i, jnp.minimum(j, n_plain - 1))),
        pl.BlockSpec((tm, GLA_GATE_RANK), lambda i, j: (i, 0)),
    ]
    out_shape = [
        jax.ShapeDtypeStruct((n, n_plain * tn), BF16),
        jax.ShapeDtypeStruct((n, GLA_GATE_RANK), F32),
    ]
    for gi, dil in enumerate(dilations):
        base = n_plain + 3 * gi
        out_specs.append(pl.BlockSpec(
            (1, dil, tm // dil, tn),
            lambda i, j, base=base: (i // spt, 0, i % spt, jnp.clip(j - base, 0, 2))))
        out_shape.append(jax.ShapeDtypeStruct((batch, dil, seq // dil, 3 * tn), BF16))
    return pl.pallas_call(
        kern,
        grid=(n // tm, ntile),
        in_specs=[
            pl.BlockSpec((tm, d), lambda i, j: (i, 0), pipeline_mode=pl.Buffered(1)),
            pl.BlockSpec((1, d), lambda i, j: (0, 0)),
            pl.BlockSpec((d, tn), lambda i, j: (0, j)),
            pl.BlockSpec((d, GLA_GATE_RANK), lambda i, j: (0, 0)),
            pl.BlockSpec((tm, LANES), lambda i, j: (i % spt, 0), pipeline_mode=pl.Buffered(1)),
            pl.BlockSpec((tm, LANES), lambda i, j: (i % spt, 0), pipeline_mode=pl.Buffered(1)),
            pl.BlockSpec((tm, LANES), lambda i, j: (i % spt, 0), pipeline_mode=pl.Buffered(1)),
        ],
        out_specs=out_specs,
        out_shape=out_shape,
        scratch_shapes=[pltpu.VMEM((tm, d), BF16), pltpu.VMEM((tn // LANES, tm, LANES), F32)],
        compiler_params=_cparams(("arbitrary", "arbitrary")),
        name="in_projection",
    )(x2, norm_g.reshape(1, d), w_main, w_glr, cos_t, sa, sb)


def _gla_kernel(q_ref, k_ref, v_ref, r_ref, glr_ref, up_ref, bias_ref, ng_ref, tri_ref, y_ref, state_ref,
                *, dk, dv, ts):
    t = pl.program_id(2)

    @pl.when(t == 0)
    def _():
        state_ref[...] = jnp.zeros_like(state_ref)

    c = GLA_CHUNK
    hps = state_ref.shape[0]
    gh, gl = _split2(glr_ref[...])
    uh, ul = _split2(up_ref[...])
    z = _dot(gh, uh) + _dot(gh, ul) + _dot(gl, uh) + bias_ref[...]
    logdec = (jnp.minimum(z, 0.0) - jnp.log1p(jnp.exp(-jnp.abs(z)))) * (1.0 / GLA_TAU)
    ldh, ldl = _split2(logdec)
    tri = tri_ref[...]
    row = lax.broadcasted_iota(jnp.int32, (c, c), 0)
    col = lax.broadcasted_iota(jnp.int32, (c, c), 1)
    causal = col <= row
    scale = dk ** -0.5
    ng = ng_ref[...]
    states = [state_ref[hh] for hh in range(hps)]
    for ci in range(ts // c):
        sl = slice(ci * c, (ci + 1) * c)
        b2 = _dot(tri, ldh[sl]) + _dot(tri, ldl[sl])
        for hh in range(hps):
            ks = slice(hh * dk, (hh + 1) * dk)
            vs = slice(hh * dv, (hh + 1) * dv)
            b = b2[:, ks]
            bl = b[c - 1:c, :]
            eb = jnp.exp(b)
            enb = jnp.exp(-b)
            ebl = jnp.exp(bl)
            qf = q_ref[sl, ks].astype(F32)
            kf = k_ref[sl, ks].astype(F32)
            qe = (qf * scale * eb).astype(BF16)
            ke = (kf * enb).astype(BF16)
            kd = (kf * enb * ebl).astype(BF16)
            vb = v_ref[sl, vs]
            a = jnp.where(causal, _dot_nt(qe, ke), 0.0).astype(BF16)
            o = _dot(a, vb) + _dot(qe, states[hh].astype(BF16))
            dec = jnp.transpose(jnp.broadcast_to(ebl, (dk, dk)))
            dec_full = jnp.concatenate([dec] * (dv // dk), axis=1)
            states[hh] = dec_full * states[hh] + _dot_tn(kd, vb)
            o = o * lax.rsqrt(jnp.mean(o * o, axis=-1, keepdims=True) + EPS) * ng
            rf = r_ref[sl, vs].astype(F32)
            o = o * (rf * jax.nn.sigmoid(rf))
            y_ref[sl, vs] = o.astype(BF16)
    for hh in range(hps):
        state_ref[hh] = states[hh]


def _gla(proj, glr, gate_up, gate_bias, norm_g, batch, seq, layout):
    n = proj.shape[0]
    dk = (layout['gk'] - layout['gq']) // GLA_HEADS
    dv = (layout['gr'] - layout['gv']) // GLA_HEADS
    ts = min(512, seq)
    assert dk == LANES and dv % dk == 0 and seq % ts == 0 and ts % GLA_CHUNK == 0
    spt = seq // ts
    hps = GLA_HEADS
    wk, wv = hps * dk, hps * dv
    assert GLA_HEADS % hps == 0 and all(layout[s] % wk == 0 for s in ('gq', 'gk'))
    assert all(layout[s] % wv == 0 for s in ('gv', 'gr'))
    qb, kb = layout['gq'] // wk, layout['gk'] // wk
    vb, rb = layout['gv'] // wv, layout['gr'] // wv
    tri = jnp.asarray(np.tril(np.ones((GLA_CHUNK, GLA_CHUNK), np.float32)), BF16)
    kern = functools.partial(_gla_kernel, dk=dk, dv=dv, ts=ts)
    row = lambda b, h, t: b * spt + t
    return pl.pallas_call(
        kern,
        grid=(batch, GLA_HEADS // hps, spt),
        in_specs=[
            pl.BlockSpec((ts, wk), lambda b, h, t: (row(b, h, t), qb + h)),
            pl.BlockSpec((ts, wk), lambda b, h, t: (row(b, h, t), kb + h)),
            pl.BlockSpec((ts, wv), lambda b, h, t: (row(b, h, t), vb + h)),
            pl.BlockSpec((ts, wv), lambda b, h, t: (row(b, h, t), rb + h)),
            pl.BlockSpec((ts, GLA_GATE_RANK), lambda b, h, t: (row(b, h, t), 0)),
            pl.BlockSpec((GLA_GATE_RANK, wk), lambda b, h, t: (0, h)),
            pl.BlockSpec((1, wk), lambda b, h, t: (0, h)),
            pl.BlockSpec((1, dv), lambda b, h, t: (0, 0)),
            pl.BlockSpec((GLA_CHUNK, GLA_CHUNK), lambda b, h, t: (0, 0)),
        ],
        out_specs=pl.BlockSpec((ts, wv), lambda b, h, t: (row(b, h, t), h)),
        out_shape=jax.ShapeDtypeStruct((n, GLA_HEADS * dv), BF16),
        scratch_shapes=[pltpu.VMEM((hps, dk, dv), F32)],
        compiler_params=_cparams(("arbitrary", "arbitrary", "arbitrary")),
        name="gla",
    )(proj, proj, proj, proj, glr, gate_up, gate_bias.reshape(1, -1), norm_g.reshape(1, -1), tri)


def _att_kernel(q_ref, k_ref, v_ref, o_ref, lse_ref, *, nb, pairs, unroll):
    blk = ATT_BLOCK
    hd = ATT_HEAD_DIM
    row = lax.broadcasted_iota(jnp.int32, (2 * blk, 2 * blk), 0) % blk
    col = lax.broadcasted_iota(jnp.int32, (2 * blk, 2 * blk), 1)
    dist = row + blk - col
    band = jnp.logical_and(dist >= 0, dist <= blk)
    causal = band[:, blk:]
    head0 = lax.broadcasted_iota(jnp.int32, (blk, LANES), 1) < hd
    neg = -jnp.inf

    def block(j, first):
        r0 = pl.multiple_of(j * blk, blk)
        for p in range(pairs):
            cs = slice(p * LANES, (p + 1) * LANES)
            q2 = q_ref[0, 0, pl.ds(r0, blk), cs]
            zero = jnp.zeros_like(q2)
            qs = jnp.concatenate([jnp.where(head0, q2, zero), jnp.where(head0, zero, q2)], axis=0)
            if first:
                kk = k_ref[0, 0, pl.ds(r0, blk), cs]
                vv = v_ref[0, 0, pl.ds(r0, blk), cs]
                s = jnp.where(causal, _dot_nt(qs, kk), neg)
            else:
                k0 = pl.multiple_of(r0 - blk, blk)
                kk = k_ref[0, 0, pl.ds(k0, 2 * blk), cs]
                vv = v_ref[0, 0, pl.ds(k0, 2 * blk), cs]
                s = jnp.where(band, _dot_nt(qs, kk), neg)
            mx = jnp.max(s, axis=-1, keepdims=True)
            e = jnp.exp(s - mx).astype(BF16)
            one = jnp.ones_like(vv)
            hv = lax.broadcasted_iota(jnp.int32, vv.shape, 1) < hd
            out0 = _dot(e[:blk], jnp.where(hv, vv, one))
            out1 = _dot(e[blk:], jnp.where(hv, one, vv))
            num = jnp.where(head0, out0, out1)
            den = pltpu.roll(jnp.where(head0, out1, out0), hd, 1)
            o_ref[0, 0, pl.ds(r0, blk), cs] = (num / den).astype(o_ref.dtype)
            mxb = jnp.where(head0, mx[:blk], mx[blk:])
            lse_ref[0, 0, pl.ds(r0, blk), cs] = mxb + jnp.log(den)

    block(0, True)

    def body(j, carry):
        block(j, False)
        return carry

    lax.fori_loop(1, nb, body, 0, unroll=unroll)


def _dilated_attention(qkv, dilation, col0):
    batch, dil, sub, width = qkv.shape
    hw = ATT_HEADS_PER_GROUP * ATT_HEAD_DIM
    assert dil == dilation and sub % ATT_BLOCK == 0 and col0 % hw == 0
    nb = sub // ATT_BLOCK
    pairs = 1 if sub >= 4096 else hw // LANES
    unroll = max(1, min(4 // pairs, nb - 1))
    wpb = pairs * LANES
    steps = hw // wpb
    kern = functools.partial(_att_kernel, nb=nb, pairs=pairs, unroll=unroll)

    def in_spec(kind):
        base = (col0 + kind * hw) // wpb
        return pl.BlockSpec((1, 1, sub, wpb), lambda b, r, p: (b, r, 0, base + p))

    out_spec = pl.BlockSpec((1, 1, sub, wpb), lambda b, r, p: (b, r, 0, p))
    return pl.pallas_call(
        kern,
        grid=(batch, dil, steps),
        in_specs=[in_spec(0), in_spec(1), in_spec(2)],
        out_specs=[out_spec, out_spec],
        out_shape=[
            jax.ShapeDtypeStruct((batch, dil, sub, hw), BF16),
            jax.ShapeDtypeStruct((batch, dil, sub, hw), F32),
        ],
        compiler_params=_cparams(("arbitrary", "arbitrary", "arbitrary")),
        name=f"dilated_attention_d{dilation}",
    )(qkv, qkv, qkv)


def _merge_kernel(x_ref, yg_ref, o1_ref, o2_ref, o3_ref, l1_ref, l2_ref, l3_ref,
                  gg0_ref, gg1_ref, ga0_ref, ga1_ref, wbg_ref, wba_ref, wmo_ref, nf_ref, rw_ref, rb_ref, triu_ref,
                  ltri_ref, h_ref, hn_ref, tg_ref, lc_ref, cum_ref, cnt_ref, carry_ref, os_ref, ls_ref):
    i = pl.program_id(0)
    ne = N_EXPERTS

    @pl.when(i == 0)
    def _():
        carry_ref[...] = jnp.zeros_like(carry_ref)

    def token_order(src_ref, stage_ref):
        dil, sub, w = src_ref.shape[1:]
        if dil == 1:
            return src_ref[0, 0].astype(F32)
        for r in range(dil):
            val = src_ref[0, r].astype(F32)
            for s in range(w // LANES):
                stage_ref[s, pl.ds(r, sub, stride=dil), :] = val[:, s * LANES:(s + 1) * LANES]
        return jnp.concatenate([stage_ref[s] for s in range(w // LANES)], axis=1)

    l1, o1 = token_order(l1_ref, None), token_order(o1_ref, None)
    l2, o2 = token_order(l2_ref, ls_ref.at[0]), token_order(o2_ref, os_ref.at[0])
    l3, o3 = token_order(l3_ref, ls_ref.at[1]), token_order(o3_ref, os_ref.at[1])
    m = jnp.maximum(jnp.maximum(l1, l2), l3)
    w1, w2, w3 = jnp.exp(l1 - m), jnp.exp(l2 - m), jnp.exp(l3 - m)
    y_att = (w1 * o1 + w2 * o2 + w3 * o3) / (w1 + w2 + w3)

    m_gla = _dot(yg_ref[...], wbg_ref[...])
    m_att = _dot(y_att.astype(BF16), wba_ref[...])
    gg = jnp.concatenate([gg0_ref[...], gg1_ref[...]], axis=1).astype(F32)
    ga = jnp.concatenate([ga0_ref[...], ga1_ref[...]], axis=1).astype(F32)
    merged = jax.nn.sigmoid(gg) * m_gla + jax.nn.sigmoid(ga) * m_att
    h = x_ref[...] + _dot(merged.astype(BF16), wmo_ref[...])
    h_ref[...] = h
    hn = h * lax.rsqrt(jnp.mean(h * h, axis=-1, keepdims=True) + EPS) * nf_ref[...]
    hn_ref[...] = hn.astype(BF16)

    hh, hl = _split2(hn)
    rh, rl = _split2(rw_ref[...])
    logit = _dot_nt(rh, hh) + _dot_nt(rh, hl) + _dot_nt(rl, hh) + rb_ref[...]
    tm = logit.shape[1]
    eid = lax.broadcasted_iota(jnp.int32, (ne, tm), 0)
    member = jnp.zeros((ne, tm), jnp.bool_)
    vals, idxs = [], []
    for _ in range(TOP_K):
        mx = jnp.max(logit, axis=0, keepdims=True)
        idx = jnp.min(jnp.where(logit == mx, eid, ne), axis=0, keepdims=True)
        sel = eid == idx
        member = jnp.logical_or(member, sel)
        logit = jnp.where(sel, -jnp.inf, logit)
        vals.append(mx)
        idxs.append(idx)
    ex = [jnp.exp(v - vals[0]) for v in vals]
    tot = ex[0] + ex[1] + ex[2] + ex[3]
    memf = jnp.where(member, 1.0, 0.0)
    lens = jnp.sum(memf, axis=1, keepdims=True)
    lens_b = jnp.broadcast_to(lens, (ne, LANES))
    off = _dot(ltri_ref[...], lens_b.astype(BF16))
    pos = _dot(memf.astype(BF16), triu_ref[...]) + off[:, 0:1]
    for k in range(TOP_K):
        tg_ref[0, k:k + 1, :] = ex[k] / tot
        lc_ref[0, k:k + 1, :] = jnp.sum(jnp.where(eid == idxs[k], pos, 0.0), axis=0, keepdims=True).astype(jnp.int32)
    carry = carry_ref[...]
    cum_ref[0] = carry.astype(jnp.int32)
    carry = carry + lens_b
    carry_ref[...] = carry
    cnt_ref[...] = carry.astype(jnp.int32)


def _merge(x2, y_gla, att, proj, layout, wbg, wba, wmo, norm_ffn_g, router_w, router_b, seq, tm):
    n, d = x2.shape
    hw = ATT_HEADS_PER_GROUP * ATT_HEAD_DIM
    half = d // 2
    assert n % tm == 0 and seq % tm == 0
    assert tm <= 256, "per-tile expert counts go through a bf16 matmul operand: exact up to 256"
    assert layout['gate_gla'] % half == 0 and layout['gate_att'] % half == 0
    assert all(tm % (o.shape[1] * 16) == 0 for o, _ in att) and att[0][0].shape[1] == 1
    nt = n // tm
    spt = seq // tm
    ggb = layout['gate_gla'] // half
    gab = layout['gate_att'] // half
    triu = jnp.asarray(np.triu(np.ones((tm, tm), np.float32), 1), BF16)
    row = lambda i: (i, 0)
    const = lambda i: (0, 0)

    def att_spec(a):
        dil = a.shape[1]
        return pl.BlockSpec((1, dil, tm // dil, hw), lambda i: (i // spt, 0, i % spt, 0))

    (o1, l1), (o2, l2), (o3, l3) = att
    in_specs = [
        pl.BlockSpec((tm, d), row),
        pl.BlockSpec((tm, y_gla.shape[1]), row),
        att_spec(o1), att_spec(o2), att_spec(o3), att_spec(l1), att_spec(l2), att_spec(l3),
        pl.BlockSpec((tm, half), lambda i: (i, ggb)), pl.BlockSpec((tm, half), lambda i: (i, ggb + 1)),
        pl.BlockSpec((tm, half), lambda i: (i, gab)), pl.BlockSpec((tm, half), lambda i: (i, gab + 1)),
        pl.BlockSpec(wbg.shape, const), pl.BlockSpec(wba.shape, const), pl.BlockSpec(wmo.shape, const),
        pl.BlockSpec((1, d), const),
        pl.BlockSpec((N_EXPERTS, d), const),
        pl.BlockSpec((N_EXPERTS, 1), const),
        pl.BlockSpec((tm, tm), const),
        pl.BlockSpec((N_EXPERTS, N_EXPERTS), const),
    ]
    out_specs = [
        pl.BlockSpec((tm, d), row),
        pl.BlockSpec((tm, d), row),
        pl.BlockSpec((1, TOP_K, tm), lambda i: (i, 0, 0)),
        pl.BlockSpec((1, TOP_K, tm), lambda i: (i, 0, 0)),
        pl.BlockSpec((1, N_EXPERTS, LANES), lambda i: (i, 0, 0)),
        pl.BlockSpec((N_EXPERTS, LANES), const),
    ]
    out_shape = [
        jax.ShapeDtypeStruct((n, d), F32),
        jax.ShapeDtypeStruct((n, d), BF16),
        jax.ShapeDtypeStruct((nt, TOP_K, tm), F32),
        jax.ShapeDtypeStruct((nt, TOP_K, tm), jnp.int32),
        jax.ShapeDtypeStruct((nt, N_EXPERTS, LANES), jnp.int32),
        jax.ShapeDtypeStruct((N_EXPERTS, LANES), jnp.int32),
    ]
    ltri = jnp.asarray(np.tril(np.ones((N_EXPERTS, N_EXPERTS), np.float32), -1), BF16)
    return pl.pallas_call(
        _merge_kernel,
        grid=(nt,),
        in_specs=in_specs,
        out_specs=out_specs,
        out_shape=out_shape,
        scratch_shapes=[pltpu.VMEM((N_EXPERTS, LANES), F32), pltpu.VMEM((2, hw // LANES, tm, LANES), F32),
                        pltpu.VMEM((2, hw // LANES, tm, LANES), F32)],
        compiler_params=_cparams(("arbitrary",)),
        name="merge_router",
    )(x2, y_gla, o1, o2, o3, l1, l2, l3, proj, proj, proj, proj, wbg, wba, wmo,
      norm_ffn_g.reshape(1, d), router_w.T, router_b.reshape(N_EXPERTS, 1), triu, ltri)


LIN_SUB = 8


def _lin_pack(val):
    return [val[:, c * LANES:(c + 1) * LANES] for c in range(LIN_SUB)]


def _lin_unpack(slabs):
    return jnp.concatenate(slabs, axis=1)


def _dispatch_kernel(off_ref, len_ref, dst_ref, zdst_ref, zlen_ref, nact_ref, hn_ref, loc_ref, x_hbm,
                     xs_ref, zero_ref, sem, zsem, *, rloc):
    i = pl.program_id(0)
    nt = pl.num_programs(0)
    slot = i % 2
    ne = N_EXPERTS

    def drain(s):
        rows = rloc * LIN_SUB
        pltpu.make_async_copy(xs_ref.at[s], x_hbm.at[pl.ds(0, rows)], sem.at[s]).wait()

    @pl.when(i == 0)
    def _():
        zero_ref[...] = jnp.zeros_like(zero_ref)

        def fill_copy(e):
            rows = zlen_ref[e] * LIN_SUB
            dst = pl.multiple_of(zdst_ref[e] * LIN_SUB, 8)
            return rows, pltpu.make_async_copy(zero_ref.at[pl.ds(0, rows)], x_hbm.at[pl.ds(dst, rows)], zsem)

        def fill(e, carry):
            rows, copy = fill_copy(e)
            pl.when(rows > 0)(copy.start)
            return carry

        def fill_done(e, carry):
            rows, copy = fill_copy(e)
            pl.when(rows > 0)(copy.wait)
            return carry

        def block_copy(b):
            rows = zero_ref.shape[0]
            return pltpu.make_async_copy(zero_ref, x_hbm.at[pl.ds(pl.multiple_of(b * rows, rows), rows)], zsem)

        def fill_block(b, carry):
            block_copy(b).start()
            return carry

        def fill_block_done(b, carry):
            block_copy(b).wait()
            return carry

        nblk = x_hbm.shape[0] // zero_ref.shape[0]
        lax.fori_loop(0, ne, fill, 0)
        lax.fori_loop(nact_ref[0], nblk, fill_block, 0)
        lax.fori_loop(0, ne, fill_done, 0)
        lax.fori_loop(nact_ref[0], nblk, fill_block_done, 0)

    @pl.when(i >= 2)
    def _():
        drain(slot)

    loc = loc_ref[0]
    tt = loc.shape[1]
    row_id = lax.broadcasted_iota(jnp.int32, (rloc, tt), 0)
    hit = row_id == loc[0:1, :]
    for k in range(1, TOP_K):
        hit = jnp.logical_or(hit, row_id == loc[k:k + 1, :])
    onehot = jnp.where(hit, 1.0, 0.0).astype(BF16)
    xs = _dot(onehot, hn_ref[...])
    for c, slab in enumerate(_lin_pack(xs)):
        xs_ref[slot, pl.ds(c, rloc, stride=LIN_SUB), :] = slab

    def send(e, carry):
        rows = len_ref[i * ne + e] * LIN_SUB

        @pl.when(rows > 0)
        def _():
            src = pl.multiple_of(off_ref[i * ne + e] * LIN_SUB, 8)
            dst = pl.multiple_of(dst_ref[i * ne + e] * LIN_SUB, 8)
            pltpu.make_async_copy(xs_ref.at[slot, pl.ds(src, rows)], x_hbm.at[pl.ds(dst, rows)], sem.at[slot]).start()
        return carry

    lax.fori_loop(0, ne, send, 0)

    @pl.when(i == nt - 1)
    def _():
        @pl.when(i >= 1)
        def _():
            drain(1 - slot)
        drain(slot)


def _dispatch(hn, loc, tabs, nact, p_rows, rloc):
    n, d = hn.shape
    nt, _, tt = loc.shape
    assert d == LIN_SUB * LANES and p_rows % MOE_BLOCK_ROWS == 0
    kern = functools.partial(_dispatch_kernel, rloc=rloc)
    grid_spec = pltpu.PrefetchScalarGridSpec(
        num_scalar_prefetch=6,
        grid=(nt,),
        in_specs=[
            pl.BlockSpec((tt, d), lambda i, *_: (i, 0)),
            pl.BlockSpec((1, TOP_K, tt), lambda i, *_: (i, 0, 0)),
        ],
        out_specs=pl.BlockSpec(memory_space=pl.ANY),
        scratch_shapes=[
            pltpu.VMEM((2, rloc * LIN_SUB, LANES), F32),
            pltpu.VMEM((MOE_BLOCK_ROWS * LIN_SUB, LANES), F32),
            pltpu.SemaphoreType.DMA((2,)),
            pltpu.SemaphoreType.DMA(()),
        ],
    )
    return pl.pallas_call(
        kern,
        grid_spec=grid_spec,
        out_shape=jax.ShapeDtypeStruct((p_rows * LIN_SUB, LANES), F32),
        compiler_params=_cparams(("arbitrary",)),
        name="moe_dispatch",
    )(tabs['off'], tabs['len'], tabs['dst'], tabs['zdst'], tabs['zlen'], nact, hn, loc)


def _expert_kernel(bexp_ref, nact_ref, x_ref, wgu_ref, wd_ref, bgu_ref, bd_ref, perm_ref, y_ref, wgu_s, wd_s):
    i = pl.program_id(0)
    e = bexp_ref[i]
    prev = bexp_ref[jnp.maximum(i - 1, 0)]
    active = i < nact_ref[0]
    grp = MXU_DIM
    ff2 = wgu_s.shape[1]
    halfg = grp // 2

    @pl.when(jnp.logical_and(active, jnp.logical_or(i == 0, e != prev)))
    def _():
        perm = perm_ref[...]
        for g in range(ff2 // grp):
            cols = slice(g * grp, (g + 1) * grp)
            wgu_s[:, cols] = _dot(wgu_ref[0, :, cols].astype(BF16), perm).astype(BF16)
        wd_s[...] = wd_ref[0].astype(BF16)

    @pl.when(active)
    def _():
        bm = x_ref.shape[0] // LIN_SUB
        x = _lin_unpack([x_ref[pl.ds(c, bm, stride=LIN_SUB), :] for c in range(LIN_SUB)]).astype(BF16)
        acts = []
        for g in range(ff2 // grp):
            cols = slice(g * grp, (g + 1) * grp)
            hg = _dot(x, wgu_s[:, cols]) + bgu_ref[0, :, cols]
            gate = jnp.minimum(hg[:, :halfg], SWIGLU_LIMIT)
            up = jnp.clip(hg[:, halfg:], -SWIGLU_LIMIT, SWIGLU_LIMIT)
            acts.append(((up + 1.0) * (gate * jax.nn.sigmoid(SWIGLU_ALPHA * gate))).astype(BF16))
        act = jnp.concatenate(acts, axis=1)
        y = _dot(act, wd_s[...]) + bd_ref[0]
        for c, slab in enumerate(_lin_pack(y)):
            y_ref[pl.ds(c, bm, stride=LIN_SUB), :] = slab

    @pl.when(jnp.logical_not(active))
    def _():
        y_ref[...] = jnp.zeros_like(y_ref)


def _gate_up_permutation():
    grp = MXU_DIM
    halfg = grp // 2
    out = np.arange(grp)
    src = np.where(out < halfg, 2 * out, 2 * (out - halfg) + 1)
    p = np.zeros((grp, grp), np.float32)
    p[src, out] = 1.0
    return jnp.asarray(p, BF16)


def _experts(x_lin, bexp, nact, w_gate_up, b_gate_up, w_down, b_down, bm):
    ne, d, ff2 = w_gate_up.shape
    ff = ff2 // 2
    grp = MXU_DIM
    assert ff2 % grp == 0 and d == LIN_SUB * LANES
    nblk = x_lin.shape[0] // (bm * LIN_SUB)
    bgu = b_gate_up.reshape(ne, ff2 // grp, grp // 2, 2).transpose(0, 1, 3, 2).reshape(ne, 1, ff2)
    grid_spec = pltpu.PrefetchScalarGridSpec(
        num_scalar_prefetch=2,
        grid=(nblk,),
        in_specs=[
            pl.BlockSpec((bm * LIN_SUB, LANES), lambda i, be, na: (i, 0)),
            pl.BlockSpec((1, d, ff2), lambda i, be, na: (be[i], 0, 0)),
            pl.BlockSpec((1, ff, d), lambda i, be, na: (be[i], 0, 0)),
            pl.BlockSpec((1, 1, ff2), lambda i, be, na: (be[i], 0, 0)),
            pl.BlockSpec((1, 1, d), lambda i, be, na: (be[i], 0, 0)),
            pl.BlockSpec((grp, grp), lambda i, be, na: (0, 0)),
        ],
        out_specs=pl.BlockSpec((bm * LIN_SUB, LANES), lambda i, be, na: (i, 0)),
        scratch_shapes=[pltpu.VMEM((d, ff2), BF16), pltpu.VMEM((ff, d), BF16)],
    )
    return pl.pallas_call(
        _expert_kernel,
        grid_spec=grid_spec,
        out_shape=jax.ShapeDtypeStruct(x_lin.shape, F32),
        compiler_params=_cparams(("arbitrary",)),
        name="moe_experts",
    )(bexp, nact, x_lin, w_gate_up, w_down, bgu, b_down.reshape(ne, 1, d), _gate_up_permutation())


def _combine_kernel(off_ref, len_ref, dst_ref, h_ref, loc_ref, gate_ref, g_ref, y_hbm, out_ref,
                    ybuf, sem, *, rloc):
    i = pl.program_id(0)
    nt = pl.num_programs(0)
    slot = i % 2
    ne = N_EXPERTS

    def fetch(step, s):
        def run(e, carry):
            rows = len_ref[step * ne + e] * LIN_SUB

            @pl.when(rows > 0)
            def _():
                src = pl.multiple_of(dst_ref[step * ne + e] * LIN_SUB, 8)
                dst = pl.multiple_of(off_ref[step * ne + e] * LIN_SUB, 8)
                pltpu.make_async_copy(y_hbm.at[pl.ds(src, rows)], ybuf.at[s, pl.ds(dst, rows)], sem.at[s]).start()
            return carry

        lax.fori_loop(0, ne, run, 0)

    @pl.when(i == 0)
    def _():
        fetch(0, 0)

    @pl.when(i + 1 < nt)
    def _():
        fetch(i + 1, 1 - slot)

    pltpu.make_async_copy(y_hbm.at[pl.ds(0, rloc * LIN_SUB)], ybuf.at[slot], sem.at[slot]).wait()

    y = _lin_unpack([ybuf[slot, pl.ds(c, rloc, stride=LIN_SUB), :] for c in range(LIN_SUB)]).astype(BF16)
    loc = loc_ref[0]
    gate = gate_ref[0]
    tt = loc.shape[1]
    row_id = lax.broadcasted_iota(jnp.int32, (rloc, tt), 0)
    w = jnp.zeros((rloc, tt), F32)
    for k in range(TOP_K):
        w = w + jnp.where(row_id == loc[k:k + 1, :], gate[k:k + 1, :], 0.0)
    h = h_ref[...] + _dot_tn(w.astype(BF16), y)
    out_ref[...] = h * lax.rsqrt(jnp.mean(h * h, axis=-1, keepdims=True) + EPS) * g_ref[...]


def _combine(h, y_lin, loc, gates, tabs, norm_g, rloc):
    n, d = h.shape
    nt, _, tt = loc.shape
    kern = functools.partial(_combine_kernel, rloc=rloc)
    grid_spec = pltpu.PrefetchScalarGridSpec(
        num_scalar_prefetch=3,
        grid=(nt,),
        in_specs=[
            pl.BlockSpec((tt, d), lambda i, *_: (i, 0)),
            pl.BlockSpec((1, TOP_K, tt), lambda i, *_: (i, 0, 0)),
            pl.BlockSpec((1, TOP_K, tt), lambda i, *_: (i, 0, 0)),
            pl.BlockSpec((1, d), lambda i, *_: (0, 0)),
            pl.BlockSpec(memory_space=pl.ANY),
        ],
        out_specs=pl.BlockSpec((tt, d), lambda i, *_: (i, 0)),
        scratch_shapes=[
            pltpu.VMEM((2, rloc * LIN_SUB, LANES), F32),
            pltpu.SemaphoreType.DMA((2,)),
        ],
    )
    return pl.pallas_call(
        kern,
        grid_spec=grid_spec,
        out_shape=jax.ShapeDtypeStruct((n, d), F32),
        compiler_params=_cparams(("arbitrary",)),
        name="moe_combine",
    )(tabs['off'], tabs['len'], tabs['dst'], h, loc, gates, norm_g.reshape(1, d), y_lin)


def _routing_tables(cum, counts, bm, nblk):
    padded = (counts + bm - 1) // bm * bm
    pend = jnp.cumsum(padded)
    pstart = pend - padded
    blk = jnp.arange(nblk, dtype=jnp.int32)
    bexp_raw = jnp.sum((pend // bm)[None, :] <= blk[:, None], axis=1).astype(jnp.int32)
    nact = (pend[-1] // bm).astype(jnp.int32)
    last = jnp.minimum(bexp_raw[jnp.maximum(nact - 1, 0)], N_EXPERTS - 1)
    bexp = jnp.where(blk < nact, jnp.minimum(bexp_raw, N_EXPERTS - 1), last)
    cum_full = jnp.concatenate([cum, counts[None, :]], axis=0)
    run_len = cum_full[1:] - cum_full[:-1]
    run_off = jnp.cumsum(run_len, axis=1) - run_len
    tabs = {
        'off': run_off.reshape(-1), 'len': run_len.reshape(-1),
        'dst': (pstart[None, :] + cum_full[:-1]).reshape(-1),
        'zdst': pstart + counts, 'zlen': padded - counts,
    }
    return {k: v.astype(jnp.int32) for k, v in tabs.items()}, bexp, nact.reshape(1)


MOE_BLOCK_ROWS = 256
TOKEN_TILE = 256


def kernel(x, norm_mix_g, w_in, gla_gate_up, gla_gate_bias, gla_norm_g, w_branch_gla, w_branch_att, w_mix_out,
           norm_ffn_g, router_w, router_b, expert_w_gate_up, expert_b_gate_up, expert_w_down, expert_b_down,
           norm_final_g):
    batch, seq, d = x.shape
    assert norm_mix_g.shape[0] == 1, "single layer"
    n = batch * seq
    gla_k = gla_gate_up.shape[2]
    gla_v = w_branch_gla.shape[1]
    att_w = len(ATT_GROUPS) * ATT_HEADS_PER_GROUP * ATT_HEAD_DIM
    hw = ATT_HEADS_PER_GROUP * ATT_HEAD_DIM
    ngrp = len(ATT_GROUPS)
    assert att_w == ngrp * hw and all(s % hw == 0 for s in (gla_k, gla_v, d))
    assert ATT_GROUPS[0][1] == 1 and all(win // dil == ATT_BLOCK for win, dil in ATT_GROUPS)
    src, off = {}, 0
    for name, size in (('gq', gla_k), ('gk', gla_k), ('gv', gla_v), ('gr', gla_v), ('glr', GLA_GATE_RANK),
                       ('aq', att_w), ('ak', att_w), ('av', att_w), ('gate_gla', d), ('gate_att', d)):
        src[name] = (off, size)
        off += size
    assert off == w_in.shape[2]
    w = w_in[0]

    def cols(name, group=None):
        start, size = src[name]
        return w[:, start:start + size] if group is None else w[:, start + group * hw:start + (group + 1) * hw]

    plain = [cols('gq'), cols('gk'), cols('gv'), cols('gr'), cols('gate_gla'), cols('gate_att'),
             cols('aq', 0), cols('ak', 0), cols('av', 0)]
    layout, off = {}, 0
    for name, part in zip(('gq', 'gk', 'gv', 'gr', 'gate_gla', 'gate_att', 'aq', 'ak', 'av'), plain):
        layout[name] = off
        off += part.shape[1]
    n_plain = off // hw
    dilated = [cols(kind, g) for g in range(1, ngrp) for kind in ('aq', 'ak', 'av')]
    w_main = jnp.concatenate(plain + dilated, axis=1).astype(BF16)
    w_glr = cols('glr')
    dilations = [dil for _, dil in ATT_GROUPS[1:]]

    x2 = x.reshape(n, d)
    proj, glr, *qkv_dil = _in_projection(x2, norm_mix_g[0], w_main, w_glr, batch, seq, n_plain, dilations)
    y_gla = _gla(proj, glr, gla_gate_up[0], gla_gate_bias[0], gla_norm_g[0], batch, seq, layout)
    att = [_dilated_attention(proj.reshape(batch, 1, seq, proj.shape[1]), 1, layout['aq'])]
    att += [_dilated_attention(a, dil, 0) for a, dil in zip(qkv_dil, dilations)]
    tt = min(TOKEN_TILE, n)
    h, hn, gates, loc, cum, cnt = _merge(
        x2, y_gla, att, proj, layout, w_branch_gla[0].astype(BF16), w_branch_att[0].astype(BF16),
        w_mix_out[0].astype(BF16), norm_ffn_g[0], router_w[0], router_b[0], seq, tt)

    bm = MOE_BLOCK_ROWS
    nt = n // tt
    rloc = TOP_K * tt
    nblk = -(-(TOP_K * n) // bm) + N_EXPERTS
    tabs, bexp, nact = _routing_tables(cum[:, :, 0], cnt[:, 0], bm, nblk)
    x_lin = _dispatch(hn, loc, tabs, nact, nblk * bm, rloc)
    y_lin = _experts(x_lin, bexp, nact, expert_w_gate_up[0], expert_b_gate_up[0], expert_w_down[0],
                     expert_b_down[0], bm)
    out = _combine(h, y_lin, loc, gates, tabs, norm_final_g, rloc)
    return out.reshape(batch, seq, d)
```

```python
import functools
import math

import jax
import jax.numpy as jnp
import numpy as np
from jax import lax
from jax.experimental import pallas as pl
from jax.experimental.pallas import tpu as pltpu

GLA_HEADS = 4
GLA_GATE_RANK = 16
GLA_TAU = 16.0
GLA_CHUNK = 64
ATT_GROUPS = ((128, 1), (512, 4), (2048, 16))
ATT_HEADS_PER_GROUP = 8
ATT_HEAD_DIM = 64
ATT_BLOCK = 128
ROT_DIM = ATT_HEAD_DIM // 4
ROPE_THETA = 500000.0
N_EXPERTS = 32
TOP_K = 4
SWIGLU_LIMIT = 7.0
SWIGLU_ALPHA = 1.702
EPS = 1e-5

LANES = 128
MXU_DIM = 256
VMEM_LIMIT = 56 * 1024 * 1024

F32 = jnp.float32
BF16 = jnp.bfloat16


def _cparams(sem):
    return pltpu.CompilerParams(dimension_semantics=sem, vmem_limit_bytes=VMEM_LIMIT)


def _dot(a, b):
    return jnp.dot(a, b, preferred_element_type=F32)


def _dot_nt(a, b):
    return lax.dot_general(a, b, (((1,), (1,)), ((), ())), preferred_element_type=F32)


def _dot_tn(a, b):
    return lax.dot_general(a, b, (((0,), (0,)), ((), ())), preferred_element_type=F32)


def _split2(a):
    hi = a.astype(BF16)
    lo = (a - hi.astype(F32)).astype(BF16)
    return hi, lo


def _inproj_kernel(x_ref, g_ref, w_ref, wglr_ref, cos_ref, sa_ref, sb_ref, proj_ref, glr_ref, *rest,
                   n_plain, dilations, q_scale, tn):
    dil_refs, (xn_ref, stage_ref) = rest[:len(dilations)], rest[len(dilations):]
    x = x_ref[...]
    y = x * lax.rsqrt(jnp.mean(x * x, axis=-1, keepdims=True) + EPS) * g_ref[...]
    xn_ref[...] = y.astype(BF16)
    yh, yl = _split2(y)
    wh, wl = _split2(wglr_ref[...])
    glr_ref[...] = _dot(yh, wh) + _dot(yh, wl) + _dot(yl, wh)

    tm = xn_ref.shape[0]
    half = ROT_DIM // 2
    nslab = tn // LANES

    def slabs(j, kind):
        acc = _dot(xn_ref[...], w_ref[:, j * tn:(j + 1) * tn])
        for s in range(nslab):
            t = acc[:, s * LANES:(s + 1) * LANES]
            if kind < 2:
                t = t * cos_ref[...] + pltpu.roll(t, LANES - half, 1) * sa_ref[...] \
                    + pltpu.roll(t, half, 1) * sb_ref[...]
                if kind == 0:
                    t = t * q_scale
            yield s, t

    q0 = n_plain - 3
    for j in range(n_plain):
        for s, t in slabs(j, j - q0 if j >= q0 else 2):
            proj_ref[:, j * tn + s * LANES:j * tn + (s + 1) * LANES] = t.astype(BF16)

    for gi, dil in enumerate(dilations):
        for kind in range(3):
            j = n_plain + 3 * gi + kind
            buf = (3 * gi + kind) % stage_ref.shape[0]
            for s, t in slabs(j, kind):
                stage_ref[buf, s] = t
            for r in range(dil):
                for s in range(nslab):
                    dil_refs[gi][0, r, :, kind * tn + s * LANES:kind * tn + (s + 1) * LANES] = \
                        stage_ref[buf, s, pl.ds(r, tm // dil, stride=dil), :].astype(BF16)


def _rope_tables(seq):
    half = ROT_DIM // 2
    inv_freq = ROPE_THETA ** (-np.arange(half, dtype=np.float32) * np.float32(2.0 / ROT_DIM))
    pos = jnp.arange(seq, dtype=F32)
    ang = pos[:, None] * jnp.asarray(inv_freq, F32)[None, :]
    cos = jnp.cos(ang)
    sin = jnp.sin(ang)
    lane = np.arange(LANES) % ATT_HEAD_DIM
    idx = np.where(lane < ROT_DIM, lane % half, 0)
    in_rot = jnp.asarray(lane < ROT_DIM)
    first = jnp.asarray(lane < half)
    second = jnp.asarray((lane >= half) & (lane < ROT_DIM))
    cos_t = jnp.where(in_rot[None, :], cos[:, idx], 1.0)
    sin_g = sin[:, idx]
    sa = jnp.where(first[None, :], -sin_g, 0.0)
    sb = jnp.where(second[None, :], sin_g, 0.0)
    return cos_t.astype(F32), sa.astype(F32), sb.astype(F32)


def _in_projection(x2, norm_g, w_main, w_glr, batch, seq, n_plain, dilations):
    n, d = x2.shape
    tn = ATT_HEADS_PER_GROUP * ATT_HEAD_DIM
    tm = min(512, seq)
    width = w_main.shape[1]
    assert n % tm == 0 and seq % tm == 0 and width == (n_plain + 3 * len(dilations)) * tn
    assert all(tm % (dil * 16) == 0 for dil in dilations)
    cos_t, sa, sb = _rope_tables(seq)
    spt = seq // tm
    kern = functools.partial(_inproj_kernel, n_plain=n_plain, dilations=tuple(dilations),
                             q_scale=ATT_HEAD_DIM ** -0.5, tn=tn)
    out_specs = [
        pl.BlockSpec((tm, n_plain * tn), lambda i: (i, 0)),
        pl.BlockSpec((tm, GLA_GATE_RANK), lambda i: (i, 0)),
    ]
    out_shape = [
        jax.ShapeDtypeStruct((n, n_plain * tn), BF16),
        jax.ShapeDtypeStruct((n, GLA_GATE_RANK), F32),
    ]
    for dil in dilations:
        out_specs.append(pl.BlockSpec((1, dil, tm // dil, 3 * tn), lambda i: (i // spt, 0, i % spt, 0)))
        out_shape.append(jax.ShapeDtypeStruct((batch, dil, seq // dil, 3 * tn), BF16))
    const = lambda i: (0, 0)
    return pl.pallas_call(
        kern,
        grid=(n // tm,),
        in_specs=[
            pl.BlockSpec((tm, d), lambda i: (i, 0)),
            pl.BlockSpec((1, d), const),
            pl.BlockSpec((d, width), const, pipeline_mode=pl.Buffered(1)),
            pl.BlockSpec((d, GLA_GATE_RANK), const),
            pl.BlockSpec((tm, LANES), lambda i: (i % spt, 0)),
            pl.BlockSpec((tm, LANES), lambda i: (i % spt, 0)),
            pl.BlockSpec((tm, LANES), lambda i: (i % spt, 0)),
        ],
        out_specs=out_specs,
        out_shape=out_shape,
        scratch_shapes=[pltpu.VMEM((tm, d), BF16), pltpu.VMEM((2, tn // LANES, tm, LANES), F32)],
        compiler_params=_cparams(("arbitrary",)),
        name="in_projection",
    )(x2, norm_g.reshape(1, d), w_main, w_glr, cos_t, sa, sb)


def _gla_kernel(q_ref, k_ref, v_ref, r_ref, glr_ref, up_ref, bias_ref, ng_ref, tri_ref, y_ref, state_ref,
                *, dk, dv, ts):
    t = pl.program_id(2)

    @pl.when(t == 0)
    def _():
        state_ref[...] = jnp.zeros_like(state_ref)

    c = GLA_CHUNK
    hps = state_ref.shape[0]
    gh, gl = _split2(glr_ref[...])
    uh, ul = _split2(up_ref[...])
    z = _dot(gh, uh) + _dot(gh, ul) + _dot(gl, uh) + bias_ref[...]
    logdec = (jnp.minimum(z, 0.0) - jnp.log1p(jnp.exp(-jnp.abs(z)))) * (1.0 / GLA_TAU)
    ldh, ldl = _split2(logdec)
    tri = tri_ref[...]
    row = lax.broadcasted_iota(jnp.int32, (c, c), 0)
    col = lax.broadcasted_iota(jnp.int32, (c, c), 1)
    causal = col <= row
    scale = dk ** -0.5
    ng = ng_ref[...]
    states = [state_ref[hh] for hh in range(hps)]
    for ci in range(ts // c):
        sl = slice(ci * c, (ci + 1) * c)
        b2 = _dot(tri, ldh[sl]) + _dot(tri, ldl[sl])
        for hh in range(hps):
            ks = slice(hh * dk, (hh + 1) * dk)
            vs = slice(hh * dv, (hh + 1) * dv)
            b = b2[:, ks]
            bl = b[c - 1:c, :]
            eb = jnp.exp(b)
            enb = jnp.exp(-b)
            ebl = jnp.exp(bl)
            qf = q_ref[sl, ks].astype(F32)
            kf = k_ref[sl, ks].astype(F32)
            qe = (qf * scale * eb).astype(BF16)
            ke = (kf * enb).astype(BF16)
            kd = (kf * enb * ebl).astype(BF16)
            vb = v_ref[sl, vs]
            a = jnp.where(causal, _dot_nt(qe, ke), 0.0).astype(BF16)
            o = _dot(a, vb) + _dot(qe, states[hh].astype(BF16))
            dec = jnp.transpose(jnp.broadcast_to(ebl, (dk, dk)))
            dec_full = jnp.concatenate([dec] * (dv // dk), axis=1)
            states[hh] = dec_full * states[hh] + _dot_tn(kd, vb)
            o = o * lax.rsqrt(jnp.mean(o * o, axis=-1, keepdims=True) + EPS) * ng
            rf = r_ref[sl, vs].astype(F32)
            o = o * (rf * jax.nn.sigmoid(rf))
            y_ref[sl, vs] = o.astype(BF16)
    for hh in range(hps):
        state_ref[hh] = states[hh]


def _gla(proj, glr, gate_up, gate_bias, norm_g, batch, seq, layout):
    n = proj.shape[0]
    dk = (layout['gk'] - layout['gq']) // GLA_HEADS
    dv = (layout['gr'] - layout['gv']) // GLA_HEADS
    ts = min(512, seq)
    assert dk == LANES and dv % dk == 0 and seq % ts == 0 and ts % GLA_CHUNK == 0
    spt = seq // ts
    hps = GLA_HEADS
    wk, wv = hps * dk, hps * dv
    assert GLA_HEADS % hps == 0 and all(layout[s] % wk == 0 for s in ('gq', 'gk'))
    assert all(layout[s] % wv == 0 for s in ('gv', 'gr'))
    qb, kb = layout['gq'] // wk, layout['gk'] // wk
    vb, rb = layout['gv'] // wv, layout['gr'] // wv
    tri = jnp.asarray(np.tril(np.ones((GLA_CHUNK, GLA_CHUNK), np.float32)), BF16)
    kern = functools.partial(_gla_kernel, dk=dk, dv=dv, ts=ts)
    row = lambda b, h, t: b * spt + t
    return pl.pallas_call(
        kern,
        grid=(batch, GLA_HEADS // hps, spt),
        in_specs=[
            pl.BlockSpec((ts, wk), lambda b, h, t: (row(b, h, t), qb + h)),
            pl.BlockSpec((ts, wk), lambda b, h, t: (row(b, h, t), kb + h)),
            pl.BlockSpec((ts, wv), lambda b, h, t: (row(b, h, t), vb + h)),
            pl.BlockSpec((ts, wv), lambda b, h, t: (row(b, h, t), rb + h)),
            pl.BlockSpec((ts, GLA_GATE_RANK), lambda b, h, t: (row(b, h, t), 0)),
            pl.BlockSpec((GLA_GATE_RANK, wk), lambda b, h, t: (0, h)),
            pl.BlockSpec((1, wk), lambda b, h, t: (0, h)),
            pl.BlockSpec((1, dv), lambda b, h, t: (0, 0)),
            pl.BlockSpec((GLA_CHUNK, GLA_CHUNK), lambda b, h, t: (0, 0)),
        ],
        out_specs=pl.BlockSpec((ts, wv), lambda b, h, t: (row(b, h, t), h)),
        out_shape=jax.ShapeDtypeStruct((n, GLA_HEADS * dv), BF16),
        scratch_shapes=[pltpu.VMEM((hps, dk, dv), F32)],
        compiler_params=_cparams(("arbitrary", "arbitrary", "arbitrary")),
        name="gla",
    )(proj, proj, proj, proj, glr, gate_up, gate_bias.reshape(1, -1), norm_g.reshape(1, -1), tri)


def _att_kernel(q_ref, k_ref, v_ref, o_ref, lse_ref, *, nb, pairs, unroll):
    blk = ATT_BLOCK
    hd = ATT_HEAD_DIM
    row = lax.broadcasted_iota(jnp.int32, (2 * blk, 2 * blk), 0) % blk
    col = lax.broadcasted_iota(jnp.int32, (2 * blk, 2 * blk), 1)
    dist = row + blk - col
    band = jnp.logical_and(dist >= 0, dist <= blk)
    causal = band[:, blk:]
    head0 = lax.broadcasted_iota(jnp.int32, (blk, LANES), 1) < hd
    neg = -jnp.inf

    def block(j, first):
        r0 = pl.multiple_of(j * blk, blk)
        for p in range(pairs):
            cs = slice(p * LANES, (p + 1) * LANES)
            q2 = q_ref[0, 0, pl.ds(r0, blk), cs]
            zero = jnp.zeros_like(q2)
            qs = jnp.concatenate([jnp.where(head0, q2, zero), jnp.where(head0, zero, q2)], axis=0)
            if first:
                kk = k_ref[0, 0, pl.ds(r0, blk), cs]
                vv = v_ref[0, 0, pl.ds(r0, blk), cs]
                s = jnp.where(causal, _dot_nt(qs, kk), neg)
            else:
                k0 = pl.multiple_of(r0 - blk, blk)
                kk = k_ref[0, 0, pl.ds(k0, 2 * blk), cs]
                vv = v_ref[0, 0, pl.ds(k0, 2 * blk), cs]
                s = jnp.where(band, _dot_nt(qs, kk), neg)
            mx = jnp.max(s, axis=-1, keepdims=True)
            e = jnp.exp(s - mx).astype(BF16)
            one = jnp.ones_like(vv)
            hv = lax.broadcasted_iota(jnp.int32, vv.shape, 1) < hd
            out0 = _dot(e[:blk], jnp.where(hv, vv, one))
            out1 = _dot(e[blk:], jnp.where(hv, one, vv))
            num = jnp.where(head0, out0, out1)
            den = pltpu.roll(jnp.where(head0, out1, out0), hd, 1)
            o_ref[0, 0, pl.ds(r0, blk), cs] = (num / den).astype(o_ref.dtype)
            mxb = jnp.where(head0, mx[:blk], mx[blk:])
            lse_ref[0, 0, pl.ds(r0, blk), cs] = mxb + jnp.log(den)

    block(0, True)

    def body(j, carry):
        block(j, False)
        return carry

    lax.fori_loop(1, nb, body, 0, unroll=unroll)


def _dilated_attention(qkv, dilation, col0):
    batch, dil, sub, width = qkv.shape
    hw = ATT_HEADS_PER_GROUP * ATT_HEAD_DIM
    assert dil == dilation and sub % ATT_BLOCK == 0 and col0 % hw == 0
    nb = sub // ATT_BLOCK
    pairs = 1 if sub >= 4096 else hw // LANES
    unroll = max(1, min(4 // pairs, nb - 1))
    wpb = pairs * LANES
    steps = hw // wpb
    kern = functools.partial(_att_kernel, nb=nb, pairs=pairs, unroll=unroll)

    def in_spec(kind):
        base = (col0 + kind * hw) // wpb
        return pl.BlockSpec((1, 1, sub, wpb), lambda b, r, p: (b, r, 0, base + p))

    out_spec = pl.BlockSpec((1, 1, sub, wpb), lambda b, r, p: (b, r, 0, p))
    return pl.pallas_call(
        kern,
        grid=(batch, dil, steps),
        in_specs=[in_spec(0), in_spec(1), in_spec(2)],
        out_specs=[out_spec, out_spec],
        out_shape=[
            jax.ShapeDtypeStruct((batch, dil, sub, hw), BF16),
            jax.ShapeDtypeStruct((batch, dil, sub, hw), F32),
        ],
        compiler_params=_cparams(("arbitrary", "arbitrary", "arbitrary")),
        name=f"dilated_attention_d{dilation}",
    )(qkv, qkv, qkv)


def _merge_kernel(x_ref, yg_ref, o1_ref, o2_ref, o3_ref, l1_ref, l2_ref, l3_ref,
                  gg0_ref, gg1_ref, ga0_ref, ga1_ref, wbg_ref, wba_ref, wmo_ref, nf_ref, rw_ref, rb_ref, triu_ref,
                  ltri_ref, h_ref, hn_ref, tg_ref, lc_ref, cum_ref, cnt_ref, carry_ref, os_ref, ls_ref):
    i = pl.program_id(0)
    ne = N_EXPERTS

    @pl.when(i == 0)
    def _():
        carry_ref[...] = jnp.zeros_like(carry_ref)

    def token_order(src_ref, stage_ref):
        dil, sub, w = src_ref.shape[1:]
        if dil == 1:
            return src_ref[0, 0].astype(F32)
        for r in range(dil):
            val = src_ref[0, r].astype(F32)
            for s in range(w // LANES):
                stage_ref[s, pl.ds(r, sub, stride=dil), :] = val[:, s * LANES:(s + 1) * LANES]
        return jnp.concatenate([stage_ref[s] for s in range(w // LANES)], axis=1)

    l1, o1 = token_order(l1_ref, None), token_order(o1_ref, None)
    l2, o2 = token_order(l2_ref, ls_ref.at[0]), token_order(o2_ref, os_ref.at[0])
    l3, o3 = token_order(l3_ref, ls_ref.at[1]), token_order(o3_ref, os_ref.at[1])
    m = jnp.maximum(jnp.maximum(l1, l2), l3)
    w1, w2, w3 = jnp.exp(l1 - m), jnp.exp(l2 - m), jnp.exp(l3 - m)
    y_att = (w1 * o1 + w2 * o2 + w3 * o3) / (w1 + w2 + w3)

    m_gla = _dot(yg_ref[...], wbg_ref[...])
    m_att = _dot(y_att.astype(BF16), wba_ref[...])
    gg = jnp.concatenate([gg0_ref[...], gg1_ref[...]], axis=1).astype(F32)
    ga = jnp.concatenate([ga0_ref[...], ga1_ref[...]], axis=1).astype(F32)
    merged = jax.nn.sigmoid(gg) * m_gla + jax.nn.sigmoid(ga) * m_att
    h = x_ref[...] + _dot(merged.astype(BF16), wmo_ref[...])
    h_ref[...] = h
    hn = h * lax.rsqrt(jnp.mean(h * h, axis=-1, keepdims=True) + EPS) * nf_ref[...]
    hn_ref[...] = hn.astype(BF16)

    hh, hl = _split2(hn)
    rh, rl = _split2(rw_ref[...])
    logit = _dot_nt(rh, hh) + _dot_nt(rh, hl) + _dot_nt(rl, hh) + rb_ref[...]
    tm = logit.shape[1]
    eid = lax.broadcasted_iota(jnp.int32, (ne, tm), 0)
    member = jnp.zeros((ne, tm), jnp.bool_)
    vals, idxs = [], []
    for _ in range(TOP_K):
        mx = jnp.max(logit, axis=0, keepdims=True)
        idx = jnp.min(jnp.where(logit == mx, eid, ne), axis=0, keepdims=True)
        sel = eid == idx
        member = jnp.logical_or(member, sel)
        logit = jnp.where(sel, -jnp.inf, logit)
        vals.append(mx)
        idxs.append(idx)
    ex = [jnp.exp(v - vals[0]) for v in vals]
    tot = ex[0] + ex[1] + ex[2] + ex[3]
    memf = jnp.where(member, 1.0, 0.0)
    lens = jnp.sum(memf, axis=1, keepdims=True)
    lens_b = jnp.broadcast_to(lens, (ne, LANES))
    off = _dot(ltri_ref[...], lens_b.astype(BF16))
    pos = _dot(memf.astype(BF16), triu_ref[...]) + off[:, 0:1]
    for k in range(TOP_K):
        tg_ref[0, k:k + 1, :] = ex[k] / tot
        lc_ref[0, k:k + 1, :] = jnp.sum(jnp.where(eid == idxs[k], pos, 0.0), axis=0, keepdims=True).astype(jnp.int32)
    carry = carry_ref[...]
    cum_ref[0] = carry.astype(jnp.int32)
    carry = carry + lens_b
    carry_ref[...] = carry
    cnt_ref[...] = carry.astype(jnp.int32)


def _merge(x2, y_gla, att, proj, layout, wbg, wba, wmo, norm_ffn_g, router_w, router_b, seq, tm):
    n, d = x2.shape
    hw = ATT_HEADS_PER_GROUP * ATT_HEAD_DIM
    half = d // 2
    assert n % tm == 0 and seq % tm == 0
    assert tm <= 256, "per-tile expert counts go through a bf16 matmul operand: exact up to 256"
    assert layout['gate_gla'] % half == 0 and layout['gate_att'] % half == 0
    assert all(tm % (o.shape[1] * 16) == 0 for o, _ in att) and att[0][0].shape[1] == 1
    nt = n // tm
    spt = seq // tm
    ggb = layout['gate_gla'] // half
    gab = layout['gate_att'] // half
    triu = jnp.asarray(np.triu(np.ones((tm, tm), np.float32), 1), BF16)
    row = lambda i: (i, 0)
    const = lambda i: (0, 0)

    def att_spec(a):
        dil = a.shape[1]
        return pl.BlockSpec((1, dil, tm // dil, hw), lambda i: (i // spt, 0, i % spt, 0))

    (o1, l1), (o2, l2), (o3, l3) = att
    in_specs = [
        pl.BlockSpec((tm, d), row),
        pl.BlockSpec((tm, y_gla.shape[1]), row),
        att_spec(o1), att_spec(o2), att_spec(o3), att_spec(l1), att_spec(l2), att_spec(l3),
        pl.BlockSpec((tm, half), lambda i: (i, ggb)), pl.BlockSpec((tm, half), lambda i: (i, ggb + 1)),
        pl.BlockSpec((tm, half), lambda i: (i, gab)), pl.BlockSpec((tm, half), lambda i: (i, gab + 1)),
        pl.BlockSpec(wbg.shape, const), pl.BlockSpec(wba.shape, const), pl.BlockSpec(wmo.shape, const),
        pl.BlockSpec((1, d), const),
        pl.BlockSpec((N_EXPERTS, d), const),
        pl.BlockSpec((N_EXPERTS, 1), const),
        pl.BlockSpec((tm, tm), const),
        pl.BlockSpec((N_EXPERTS, N_EXPERTS), const),
    ]
    out_specs = [
        pl.BlockSpec((tm, d), row),
        pl.BlockSpec((tm, d), row),
        pl.BlockSpec((1, TOP_K, tm), lambda i: (i, 0, 0)),
        pl.BlockSpec((1, TOP_K, tm), lambda i: (i, 0, 0)),
        pl.BlockSpec((1, N_EXPERTS, LANES), lambda i: (i, 0, 0)),
        pl.BlockSpec((N_EXPERTS, LANES), const),
    ]
    out_shape = [
        jax.ShapeDtypeStruct((n, d), F32),
        jax.ShapeDtypeStruct((n, d), BF16),
        jax.ShapeDtypeStruct((nt, TOP_K, tm), F32),
        jax.ShapeDtypeStruct((nt, TOP_K, tm), jnp.int32),
        jax.ShapeDtypeStruct((nt, N_EXPERTS, LANES), jnp.int32),
        jax.ShapeDtypeStruct((N_EXPERTS, LANES), jnp.int32),
    ]
    ltri = jnp.asarray(np.tril(np.ones((N_EXPERTS, N_EXPERTS), np.float32), -1), BF16)
    return pl.pallas_call(
        _merge_kernel,
        grid=(nt,),
        in_specs=in_specs,
        out_specs=out_specs,
        out_shape=out_shape,
        scratch_shapes=[pltpu.VMEM((N_EXPERTS, LANES), F32), pltpu.VMEM((2, hw // LANES, tm, LANES), F32),
                        pltpu.VMEM((2, hw // LANES, tm, LANES), F32)],
        compiler_params=_cparams(("arbitrary",)),
        name="merge_router",
    )(x2, y_gla, o1, o2, o3, l1, l2, l3, proj, proj, proj, proj, wbg, wba, wmo,
      norm_ffn_g.reshape(1, d), router_w.T, router_b.reshape(N_EXPERTS, 1), triu, ltri)


LIN_SUB = 8


def _lin_pack(val):
    return [val[:, c * LANES:(c + 1) * LANES] for c in range(LIN_SUB)]


def _lin_unpack(slabs):
    return jnp.concatenate(slabs, axis=1)


def _dispatch_kernel(off_ref, len_ref, dst_ref, zdst_ref, zlen_ref, nact_ref, hn_ref, loc_ref, x_hbm,
                     xs_ref, zero_ref, sem, zsem, *, rloc):
    i = pl.program_id(0)
    nt = pl.num_programs(0)
    slot = i % 2
    ne = N_EXPERTS

    def drain(s):
        rows = rloc * LIN_SUB
        pltpu.make_async_copy(xs_ref.at[s], x_hbm.at[pl.ds(0, rows)], sem.at[s]).wait()

    @pl.when(i == 0)
    def _():
        zero_ref[...] = jnp.zeros_like(zero_ref)

        def fill_copy(e):
            rows = zlen_ref[e] * LIN_SUB
            dst = pl.multiple_of(zdst_ref[e] * LIN_SUB, 8)
            return rows, pltpu.make_async_copy(zero_ref.at[pl.ds(0, rows)], x_hbm.at[pl.ds(dst, rows)], zsem)

        def fill(e, carry):
            rows, copy = fill_copy(e)
            pl.when(rows > 0)(copy.start)
            return carry

        def fill_done(e, carry):
            rows, copy = fill_copy(e)
            pl.when(rows > 0)(copy.wait)
            return carry

        def block_copy(b):
            rows = zero_ref.shape[0]
            return pltpu.make_async_copy(zero_ref, x_hbm.at[pl.ds(pl.multiple_of(b * rows, rows), rows)], zsem)

        def fill_block(b, carry):
            block_copy(b).start()
            return carry

        def fill_block_done(b, carry):
            block_copy(b).wait()
            return carry

        nblk = x_hbm.shape[0] // zero_ref.shape[0]
        lax.fori_loop(0, ne, fill, 0)
        lax.fori_loop(nact_ref[0], nblk, fill_block, 0)
        lax.fori_loop(0, ne, fill_done, 0)
        lax.fori_loop(nact_ref[0], nblk, fill_block_done, 0)

    @pl.when(i >= 2)
    def _():
        drain(slot)

    loc = loc_ref[0]
    tt = loc.shape[1]
    row_id = lax.broadcasted_iota(jnp.int32, (rloc, tt), 0)
    hit = row_id == loc[0:1, :]
    for k in range(1, TOP_K):
        hit = jnp.logical_or(hit, row_id == loc[k:k + 1, :])
    onehot = jnp.where(hit, 1.0, 0.0).astype(BF16)
    xs = _dot(onehot, hn_ref[...])
    for c, slab in enumerate(_lin_pack(xs)):
        xs_ref[slot, pl.ds(c, rloc, stride=LIN_SUB), :] = slab

    def send(e, carry):
        rows = len_ref[i * ne + e] * LIN_SUB

        @pl.when(rows > 0)
        def _():
            src = pl.multiple_of(off_ref[i * ne + e] * LIN_SUB, 8)
            dst = pl.multiple_of(dst_ref[i * ne + e] * LIN_SUB, 8)
            pltpu.make_async_copy(xs_ref.at[slot, pl.ds(src, rows)], x_hbm.at[pl.ds(dst, rows)], sem.at[slot]).start()
        return carry

    lax.fori_loop(0, ne, send, 0)

    @pl.when(i == nt - 1)
    def _():
        @pl.when(i >= 1)
        def _():
            drain(1 - slot)
        drain(slot)


def _dispatch(hn, loc, tabs, nact, p_rows, rloc):
    n, d = hn.shape
    nt, _, tt = loc.shape
    assert d == LIN_SUB * LANES and p_rows % MOE_BLOCK_ROWS == 0
    kern = functools.partial(_dispatch_kernel, rloc=rloc)
    grid_spec = pltpu.PrefetchScalarGridSpec(
        num_scalar_prefetch=6,
        grid=(nt,),
        in_specs=[
            pl.BlockSpec((tt, d), lambda i, *_: (i, 0)),
            pl.BlockSpec((1, TOP_K, tt), lambda i, *_: (i, 0, 0)),
        ],
        out_specs=pl.BlockSpec(memory_space=pl.ANY),
        scratch_shapes=[
            pltpu.VMEM((2, rloc * LIN_SUB, LANES), F32),
            pltpu.VMEM((MOE_BLOCK_ROWS * LIN_SUB, LANES), F32),
            pltpu.SemaphoreType.DMA((2,)),
            pltpu.SemaphoreType.DMA(()),
        ],
    )
    return pl.pallas_call(
        kern,
        grid_spec=grid_spec,
        out_shape=jax.ShapeDtypeStruct((p_rows * LIN_SUB, LANES), F32),
        compiler_params=_cparams(("arbitrary",)),
        name="moe_dispatch",
    )(tabs['off'], tabs['len'], tabs['dst'], tabs['zdst'], tabs['zlen'], nact, hn, loc)


def _expert_kernel(bexp_ref, nact_ref, first_ref, next_ref, x_ref, wgu_hbm, wd_hbm, bgu_ref, bd_ref, perm_ref,
                   y_ref, wgu_f, wd_f, wgu_s, wd_s, sem):
    i = pl.program_id(0)
    e = bexp_ref[i]
    active = i < nact_ref[0]
    grp = MXU_DIM
    ff2 = wgu_s.shape[1]
    halfg = grp // 2

    def fetch(ex):
        return (pltpu.make_async_copy(wgu_hbm.at[ex], wgu_f, sem.at[0]),
                pltpu.make_async_copy(wd_hbm.at[ex], wd_f, sem.at[1]))

    @pl.when(i == 0)
    def _():
        for copy in fetch(e):
            copy.start()

    @pl.when(jnp.logical_and(active, first_ref[i] == 1))
    def _():
        for copy in fetch(e):
            copy.wait()
        perm = perm_ref[...]
        for g in range(ff2 // grp):
            cols = slice(g * grp, (g + 1) * grp)
            wgu_s[:, cols] = _dot(wgu_f[:, cols].astype(BF16), perm).astype(BF16)
        wd_s[...] = wd_f[...].astype(BF16)
        nxt = next_ref[i]

        @pl.when(nxt >= 0)
        def _():
            for copy in fetch(nxt):
                copy.start()

    @pl.when(active)
    def _():
        bm = x_ref.shape[0] // LIN_SUB
        x = _lin_unpack([x_ref[pl.ds(c, bm, stride=LIN_SUB), :] for c in range(LIN_SUB)]).astype(BF16)
        acts = []
        for g in range(ff2 // grp):
            cols = slice(g * grp, (g + 1) * grp)
            hg = _dot(x, wgu_s[:, cols]) + bgu_ref[0, :, cols]
            gate = jnp.minimum(hg[:, :halfg], SWIGLU_LIMIT)
            up = jnp.clip(hg[:, halfg:], -SWIGLU_LIMIT, SWIGLU_LIMIT)
            acts.append(((up + 1.0) * (gate * jax.nn.sigmoid(SWIGLU_ALPHA * gate))).astype(BF16))
        act = jnp.concatenate(acts, axis=1)
        y = _dot(act, wd_s[...]) + bd_ref[0]
        for c, slab in enumerate(_lin_pack(y)):
            y_ref[pl.ds(c, bm, stride=LIN_SUB), :] = slab

    @pl.when(jnp.logical_not(active))
    def _():
        y_ref[...] = jnp.zeros_like(y_ref)


def _gate_up_permutation():
    grp = MXU_DIM
    halfg = grp // 2
    out = np.arange(grp)
    src = np.where(out < halfg, 2 * out, 2 * (out - halfg) + 1)
    p = np.zeros((grp, grp), np.float32)
    p[src, out] = 1.0
    return jnp.asarray(p, BF16)


def _experts(x_lin, bexp, nact, first, nxt, w_gate_up, b_gate_up, w_down, b_down, bm):
    ne, d, ff2 = w_gate_up.shape
    ff = ff2 // 2
    grp = MXU_DIM
    assert ff2 % grp == 0 and d == LIN_SUB * LANES
    nblk = x_lin.shape[0] // (bm * LIN_SUB)
    bgu = b_gate_up.reshape(ne, ff2 // grp, grp // 2, 2).transpose(0, 1, 3, 2).reshape(ne, 1, ff2)
    grid_spec = pltpu.PrefetchScalarGridSpec(
        num_scalar_prefetch=4,
        grid=(nblk,),
        in_specs=[
            pl.BlockSpec((bm * LIN_SUB, LANES), lambda i, be, *_: (i, 0)),
            pl.BlockSpec(memory_space=pl.ANY),
            pl.BlockSpec(memory_space=pl.ANY),
            pl.BlockSpec((1, 1, ff2), lambda i, be, *_: (be[i], 0, 0)),
            pl.BlockSpec((1, 1, d), lambda i, be, *_: (be[i], 0, 0)),
            pl.BlockSpec((grp, grp), lambda i, be, *_: (0, 0)),
        ],
        out_specs=pl.BlockSpec((bm * LIN_SUB, LANES), lambda i, be, *_: (i, 0)),
        scratch_shapes=[pltpu.VMEM((d, ff2), F32), pltpu.VMEM((ff, d), F32),
                        pltpu.VMEM((d, ff2), BF16), pltpu.VMEM((ff, d), BF16),
                        pltpu.SemaphoreType.DMA((2,))],
    )
    return pl.pallas_call(
        _expert_kernel,
        grid_spec=grid_spec,
        out_shape=jax.ShapeDtypeStruct(x_lin.shape, F32),
        compiler_params=_cparams(("arbitrary",)),
        name="moe_experts",
    )(bexp, nact, first, nxt, x_lin, w_gate_up, w_down, bgu, b_down.reshape(ne, 1, d), _gate_up_permutation())


def _combine_kernel(off_ref, len_ref, dst_ref, h_ref, loc_ref, gate_ref, g_ref, y_hbm, out_ref,
                    ybuf, sem, *, rloc):
    i = pl.program_id(0)
    nt = pl.num_programs(0)
    slot = i % 2
    ne = N_EXPERTS

    def fetch(step, s):
        def run(e, carry):
            rows = len_ref[step * ne + e] * LIN_SUB

            @pl.when(rows > 0)
            def _():
                src = pl.multiple_of(dst_ref[step * ne + e] * LIN_SUB, 8)
                dst = pl.multiple_of(off_ref[step * ne + e] * LIN_SUB, 8)
                pltpu.make_async_copy(y_hbm.at[pl.ds(src, rows)], ybuf.at[s, pl.ds(dst, rows)], sem.at[s]).start()
            return carry

        lax.fori_loop(0, ne, run, 0)

    @pl.when(i == 0)
    def _():
        fetch(0, 0)

    @pl.when(i + 1 < nt)
    def _():
        fetch(i + 1, 1 - slot)

    pltpu.make_async_copy(y_hbm.at[pl.ds(0, rloc * LIN_SUB)], ybuf.at[slot], sem.at[slot]).wait()

    y = _lin_unpack([ybuf[slot, pl.ds(c, rloc, stride=LIN_SUB), :] for c in range(LIN_SUB)]).astype(BF16)
    loc = loc_ref[0]
    gate = gate_ref[0]
    tt = loc.shape[1]
    row_id = lax.broadcasted_iota(jnp.int32, (rloc, tt), 0)
    w = jnp.zeros((rloc, tt), F32)
    for k in range(TOP_K):
        w = w + jnp.where(row_id == loc[k:k + 1, :], gate[k:k + 1, :], 0.0)
    h = h_ref[...] + _dot_tn(w.astype(BF16), y)
    out_ref[...] = h * lax.rsqrt(jnp.mean(h * h, axis=-1, keepdims=True) + EPS) * g_ref[...]


def _combine(h, y_lin, loc, gates, tabs, norm_g, rloc):
    n, d = h.shape
    nt, _, tt = loc.shape
    kern = functools.partial(_combine_kernel, rloc=rloc)
    grid_spec = pltpu.PrefetchScalarGridSpec(
        num_scalar_prefetch=3,
        grid=(nt,),
        in_specs=[
            pl.BlockSpec((tt, d), lambda i, *_: (i, 0)),
            pl.BlockSpec((1, TOP_K, tt), lambda i, *_: (i, 0, 0)),
            pl.BlockSpec((1, TOP_K, tt), lambda i, *_: (i, 0, 0)),
            pl.BlockSpec((1, d), lambda i, *_: (0, 0)),
            pl.BlockSpec(memory_space=pl.ANY),
        ],
        out_specs=pl.BlockSpec((tt, d), lambda i, *_: (i, 0)),
        scratch_shapes=[
            pltpu.VMEM((2, rloc * LIN_SUB, LANES), F32),
            pltpu.SemaphoreType.DMA((2,)),
        ],
    )
    return pl.pallas_call(
        kern,
        grid_spec=grid_spec,
        out_shape=jax.ShapeDtypeStruct((n, d), F32),
        compiler_params=_cparams(("arbitrary",)),
        name="moe_combine",
    )(tabs['off'], tabs['len'], tabs['dst'], h, loc, gates, norm_g.reshape(1, d), y_lin)


def _routing_tables(cum, counts, bm, nblk):
    padded = (counts + bm - 1) // bm * bm
    pend = jnp.cumsum(padded)
    pstart = pend - padded
    blk = jnp.arange(nblk, dtype=jnp.int32)
    bexp_raw = jnp.sum((pend // bm)[None, :] <= blk[:, None], axis=1).astype(jnp.int32)
    nact = (pend[-1] // bm).astype(jnp.int32)
    last = jnp.minimum(bexp_raw[jnp.maximum(nact - 1, 0)], N_EXPERTS - 1)
    bexp = jnp.where(blk < nact, jnp.minimum(bexp_raw, N_EXPERTS - 1), last)
    prev = jnp.concatenate([jnp.full((1,), -1, jnp.int32), bexp[:-1]])
    first = jnp.logical_and(bexp != prev, blk < nact)
    later = jnp.logical_and(first[None, :], blk[None, :] > blk[:, None])
    nxt_blk = jnp.min(jnp.where(later, blk[None, :], nblk), axis=1)
    nxt = jnp.where(nxt_blk < nblk, bexp[jnp.minimum(nxt_blk, nblk - 1)], -1)
    cum_full = jnp.concatenate([cum, counts[None, :]], axis=0)
    run_len = cum_full[1:] - cum_full[:-1]
    run_off = jnp.cumsum(run_len, axis=1) - run_len
    tabs = {
        'off': run_off.reshape(-1), 'len': run_len.reshape(-1),
        'dst': (pstart[None, :] + cum_full[:-1]).reshape(-1),
        'zdst': pstart + counts, 'zlen': padded - counts,
    }
    blocks = (bexp, nact.reshape(1), first.astype(jnp.int32), nxt.astype(jnp.int32))
    return {k: v.astype(jnp.int32) for k, v in tabs.items()}, blocks


MOE_BLOCK_ROWS = 256
TOKEN_TILE = 256


def kernel(x, norm_mix_g, w_in, gla_gate_up, gla_gate_bias, gla_norm_g, w_branch_gla, w_branch_att, w_mix_out,
           norm_ffn_g, router_w, router_b, expert_w_gate_up, expert_b_gate_up, expert_w_down, expert_b_down,
           norm_final_g):
    batch, seq, d = x.shape
    assert norm_mix_g.shape[0] == 1, "single layer"
    n = batch * seq
    gla_k = gla_gate_up.shape[2]
    gla_v = w_branch_gla.shape[1]
    att_w = len(ATT_GROUPS) * ATT_HEADS_PER_GROUP * ATT_HEAD_DIM
    hw = ATT_HEADS_PER_GROUP * ATT_HEAD_DIM
    ngrp = len(ATT_GROUPS)
    assert att_w == ngrp * hw and all(s % hw == 0 for s in (gla_k, gla_v, d))
    assert ATT_GROUPS[0][1] == 1 and all(win // dil == ATT_BLOCK for win, dil in ATT_GROUPS)
    src, off = {}, 0
    for name, size in (('gq', gla_k), ('gk', gla_k), ('gv', gla_v), ('gr', gla_v), ('glr', GLA_GATE_RANK),
                       ('aq', att_w), ('ak', att_w), ('av', att_w), ('gate_gla', d), ('gate_att', d)):
        src[name] = (off, size)
        off += size
    assert off == w_in.shape[2]
    w = w_in[0]

    def cols(name, group=None):
        start, size = src[name]
        return w[:, start:start + size] if group is None else w[:, start + group * hw:start + (group + 1) * hw]

    plain = [cols('gq'), cols('gk'), cols('gv'), cols('gr'), cols('gate_gla'), cols('gate_att'),
             cols('aq', 0), cols('ak', 0), cols('av', 0)]
    layout, off = {}, 0
    for name, part in zip(('gq', 'gk', 'gv', 'gr', 'gate_gla', 'gate_att', 'aq', 'ak', 'av'), plain):
        layout[name] = off
        off += part.shape[1]
    n_plain = off // hw
    dilated = [cols(kind, g) for g in range(1, ngrp) for kind in ('aq', 'ak', 'av')]
    w_main = jnp.concatenate(plain + dilated, axis=1).astype(BF16)
    w_glr = cols('glr')
    dilations = [dil for _, dil in ATT_GROUPS[1:]]

    x2 = x.reshape(n, d)
    proj, glr, *qkv_dil = _in_projection(x2, norm_mix_g[0], w_main, w_glr, batch, seq, n_plain, dilations)
    y_gla = _gla(proj, glr, gla_gate_up[0], gla_gate_bias[0], gla_norm_g[0], batch, seq, layout)
    att = [_dilated_attention(proj.reshape(batch, 1, seq, proj.shape[1]), 1, layout['aq'])]
    att += [_dilated_attention(a, dil, 0) for a, dil in zip(qkv_dil, dilations)]
    tt = min(TOKEN_TILE, n)
    h, hn, gates, loc, cum, cnt = _merge(
        x2, y_gla, att, proj, layout, w_branch_gla[0].astype(BF16), w_branch_att[0].astype(BF16),
        w_mix_out[0].astype(BF16), norm_ffn_g[0], router_w[0], router_b[0], seq, tt)

    bm = MOE_BLOCK_ROWS
    rloc = TOP_K * tt
    nblk = -(-(TOP_K * n) // bm) + N_EXPERTS
    tabs, blocks = _routing_tables(cum[:, :, 0], cnt[:, 0], bm, nblk)
    x_lin = _dispatch(hn, loc, tabs, blocks[1], nblk * bm, rloc)
    y_lin = _experts(x_lin, *blocks, expert_w_gate_up[0], expert_b_gate_up[0], expert_w_down[0],
                     expert_b_down[0], bm)
    out = _combine(h, y_lin, loc, gates, tabs, norm_final_g, rloc)
    return out.reshape(batch, seq, d)
```

```python
import functools
import math

import jax
import jax.numpy as jnp
import numpy as np
from jax import lax
from jax.experimental import pallas as pl
from jax.experimental.pallas import tpu as pltpu

GLA_HEADS = 4
GLA_GATE_RANK = 16
GLA_TAU = 16.0
GLA_CHUNK = 64
ATT_GROUPS = ((128, 1), (512, 4), (2048, 16))
ATT_HEADS_PER_GROUP = 8
ATT_HEAD_DIM = 64
ATT_BLOCK = 128
ROT_DIM = ATT_HEAD_DIM // 4
ROPE_THETA = 500000.0
N_EXPERTS = 32
TOP_K = 4
SWIGLU_LIMIT = 7.0
SWIGLU_ALPHA = 1.702
EPS = 1e-5

LANES = 128
MXU_DIM = 256
VMEM_LIMIT = 56 * 1024 * 1024

F32 = jnp.float32
BF16 = jnp.bfloat16


def _cparams(sem):
    return pltpu.CompilerParams(dimension_semantics=sem, vmem_limit_bytes=VMEM_LIMIT)


def _dot(a, b):
    return jnp.dot(a, b, preferred_element_type=F32)


def _dot_nt(a, b):
    return lax.dot_general(a, b, (((1,), (1,)), ((), ())), preferred_element_type=F32)


def _dot_tn(a, b):
    return lax.dot_general(a, b, (((0,), (0,)), ((), ())), preferred_element_type=F32)


def _split2(a):
    hi = a.astype(BF16)
    lo = (a - hi.astype(F32)).astype(BF16)
    return hi, lo


def _inproj_kernel(x_ref, g_ref, w_ref, wglr_ref, cos_ref, sa_ref, sb_ref, proj_ref, glr_ref, *rest,
                   n_plain, dilations, q_scale, tn, tile_cols):
    dil_refs, (xn_ref, stage_ref) = rest[:len(dilations)], rest[len(dilations):]
    x = x_ref[...]
    y = x * lax.rsqrt(jnp.mean(x * x, axis=-1, keepdims=True) + EPS) * g_ref[...]
    xn_ref[...] = y.astype(BF16)
    yh, yl = _split2(y)
    wh, wl = _split2(wglr_ref[...])
    glr_ref[...] = _dot(yh, wh) + _dot(yh, wl) + _dot(yl, wh)

    tm = xn_ref.shape[0]
    half = ROT_DIM // 2
    nslab = tn // LANES

    def slabs(j, kind):
        acc = _dot(xn_ref[...], w_ref[:, tile_cols[j]:tile_cols[j] + tn])
        for s in range(nslab):
            t = acc[:, s * LANES:(s + 1) * LANES]
            if kind < 2:
                t = t * cos_ref[...] + pltpu.roll(t, LANES - half, 1) * sa_ref[...] \
                    + pltpu.roll(t, half, 1) * sb_ref[...]
                if kind == 0:
                    t = t * q_scale
            yield s, t

    q0 = n_plain - 3
    for j in range(n_plain):
        for s, t in slabs(j, j - q0 if j >= q0 else 2):
            proj_ref[:, j * tn + s * LANES:j * tn + (s + 1) * LANES] = t.astype(BF16)

    for gi, dil in enumerate(dilations):
        for kind in range(3):
            j = n_plain + 3 * gi + kind
            buf = (3 * gi + kind) % stage_ref.shape[0]
            for s, t in slabs(j, kind):
                stage_ref[buf, s] = t
            for r in range(dil):
                for s in range(nslab):
                    dil_refs[gi][0, r, :, kind * tn + s * LANES:kind * tn + (s + 1) * LANES] = \
                        stage_ref[buf, s, pl.ds(r, tm // dil, stride=dil), :].astype(BF16)


def _rope_tables(seq):
    half = ROT_DIM // 2
    inv_freq = ROPE_THETA ** (-np.arange(half, dtype=np.float32) * np.float32(2.0 / ROT_DIM))
    pos = jnp.arange(seq, dtype=F32)
    ang = pos[:, None] * jnp.asarray(inv_freq, F32)[None, :]
    cos = jnp.cos(ang)
    sin = jnp.sin(ang)
    lane = np.arange(LANES) % ATT_HEAD_DIM
    idx = np.where(lane < ROT_DIM, lane % half, 0)
    in_rot = jnp.asarray(lane < ROT_DIM)
    first = jnp.asarray(lane < half)
    second = jnp.asarray((lane >= half) & (lane < ROT_DIM))
    cos_t = jnp.where(in_rot[None, :], cos[:, idx], 1.0)
    sin_g = sin[:, idx]
    sa = jnp.where(first[None, :], -sin_g, 0.0)
    sb = jnp.where(second[None, :], sin_g, 0.0)
    return cos_t.astype(F32), sa.astype(F32), sb.astype(F32)


def _in_projection(x2, norm_g, w_main, w_glr, batch, seq, n_plain, dilations, tile_cols):
    n, d = x2.shape
    tn = ATT_HEADS_PER_GROUP * ATT_HEAD_DIM
    tm = min(512, seq)
    width = w_main.shape[1]
    assert n % tm == 0 and seq % tm == 0 and len(tile_cols) == n_plain + 3 * len(dilations)
    assert all(c % LANES == 0 and c + tn <= width for c in tile_cols)
    assert all(tm % (dil * 16) == 0 for dil in dilations)
    cos_t, sa, sb = _rope_tables(seq)
    spt = seq // tm
    kern = functools.partial(_inproj_kernel, n_plain=n_plain, dilations=tuple(dilations),
                             q_scale=ATT_HEAD_DIM ** -0.5, tn=tn, tile_cols=tile_cols)
    out_specs = [
        pl.BlockSpec((tm, n_plain * tn), lambda i: (i, 0)),
        pl.BlockSpec((tm, GLA_GATE_RANK), lambda i: (i, 0)),
    ]
    out_shape = [
        jax.ShapeDtypeStruct((n, n_plain * tn), BF16),
        jax.ShapeDtypeStruct((n, GLA_GATE_RANK), F32),
    ]
    for dil in dilations:
        out_specs.append(pl.BlockSpec((1, dil, tm // dil, 3 * tn), lambda i: (i // spt, 0, i % spt, 0)))
        out_shape.append(jax.ShapeDtypeStruct((batch, dil, seq // dil, 3 * tn), BF16))
    const = lambda i: (0, 0)
    return pl.pallas_call(
        kern,
        grid=(n // tm,),
        in_specs=[
            pl.BlockSpec((tm, d), lambda i: (i, 0)),
            pl.BlockSpec((1, d), const),
            pl.BlockSpec((d, width), const, pipeline_mode=pl.Buffered(1)),
            pl.BlockSpec((d, GLA_GATE_RANK), const),
            pl.BlockSpec((tm, LANES), lambda i: (i % spt, 0)),
            pl.BlockSpec((tm, LANES), lambda i: (i % spt, 0)),
            pl.BlockSpec((tm, LANES), lambda i: (i % spt, 0)),
        ],
        out_specs=out_specs,
        out_shape=out_shape,
        scratch_shapes=[pltpu.VMEM((tm, d), BF16), pltpu.VMEM((2, tn // LANES, tm, LANES), F32)],
        compiler_params=_cparams(("arbitrary",)),
        name="in_projection",
    )(x2, norm_g.reshape(1, d), w_main, w_glr, cos_t, sa, sb)


def _gla_kernel(q_ref, k_ref, v_ref, r_ref, glr_ref, up_ref, bias_ref, ng_ref, tri_ref, y_ref, state_ref,
                *, dk, dv, ts):
    t = pl.program_id(2)

    @pl.when(t == 0)
    def _():
        state_ref[...] = jnp.zeros_like(state_ref)

    c = GLA_CHUNK
    hps = state_ref.shape[0]
    gh, gl = _split2(glr_ref[...])
    uh, ul = _split2(up_ref[...])
    z = _dot(gh, uh) + _dot(gh, ul) + _dot(gl, uh) + bias_ref[...]
    logdec = (jnp.minimum(z, 0.0) - jnp.log1p(jnp.exp(-jnp.abs(z)))) * (1.0 / GLA_TAU)
    ldh, ldl = _split2(logdec)
    tri = tri_ref[...]
    row = lax.broadcasted_iota(jnp.int32, (c, c), 0)
    col = lax.broadcasted_iota(jnp.int32, (c, c), 1)
    causal = col <= row
    scale = dk ** -0.5
    ng = ng_ref[...]
    states = [state_ref[hh] for hh in range(hps)]
    for ci in range(ts // c):
        sl = slice(ci * c, (ci + 1) * c)
        b2 = _dot(tri, ldh[sl]) + _dot(tri, ldl[sl])
        for hh in range(hps):
            ks = slice(hh * dk, (hh + 1) * dk)
            vs = slice(hh * dv, (hh + 1) * dv)
            b = b2[:, ks]
            bl = b[c - 1:c, :]
            eb = jnp.exp(b)
            enb = jnp.exp(-b)
            ebl = jnp.exp(bl)
            qf = q_ref[sl, ks].astype(F32)
            kf = k_ref[sl, ks].astype(F32)
            qe = (qf * scale * eb).astype(BF16)
            ke = (kf * enb).astype(BF16)
            kd = (kf * enb * ebl).astype(BF16)
            vb = v_ref[sl, vs]
            a = jnp.where(causal, _dot_nt(qe, ke), 0.0).astype(BF16)
            o = _dot(a, vb) + _dot(qe, states[hh].astype(BF16))
            dec = jnp.transpose(jnp.broadcast_to(ebl, (dk, dk)))
            dec_full = jnp.concatenate([dec] * (dv // dk), axis=1)
            states[hh] = dec_full * states[hh] + _dot_tn(kd, vb)
            o = o * lax.rsqrt(jnp.mean(o * o, axis=-1, keepdims=True) + EPS) * ng
            rf = r_ref[sl, vs].astype(F32)
            o = o * (rf * jax.nn.sigmoid(rf))
            y_ref[sl, vs] = o.astype(BF16)
    for hh in range(hps):
        state_ref[hh] = states[hh]


def _gla(proj, glr, gate_up, gate_bias, norm_g, batch, seq, layout):
    n = proj.shape[0]
    dk = (layout['gk'] - layout['gq']) // GLA_HEADS
    dv = (layout['gr'] - layout['gv']) // GLA_HEADS
    ts = min(512, seq)
    assert dk == LANES and dv % dk == 0 and seq % ts == 0 and ts % GLA_CHUNK == 0
    spt = seq // ts
    hps = GLA_HEADS
    wk, wv = hps * dk, hps * dv
    assert GLA_HEADS % hps == 0 and all(layout[s] % wk == 0 for s in ('gq', 'gk'))
    assert all(layout[s] % wv == 0 for s in ('gv', 'gr'))
    qb, kb = layout['gq'] // wk, layout['gk'] // wk
    vb, rb = layout['gv'] // wv, layout['gr'] // wv
    tri = jnp.asarray(np.tril(np.ones((GLA_CHUNK, GLA_CHUNK), np.float32)), BF16)
    kern = functools.partial(_gla_kernel, dk=dk, dv=dv, ts=ts)
    row = lambda b, h, t: b * spt + t
    return pl.pallas_call(
        kern,
        grid=(batch, GLA_HEADS // hps, spt),
        in_specs=[
            pl.BlockSpec((ts, wk), lambda b, h, t: (row(b, h, t), qb + h)),
            pl.BlockSpec((ts, wk), lambda b, h, t: (row(b, h, t), kb + h)),
            pl.BlockSpec((ts, wv), lambda b, h, t: (row(b, h, t), vb + h)),
            pl.BlockSpec((ts, wv), lambda b, h, t: (row(b, h, t), rb + h)),
            pl.BlockSpec((ts, GLA_GATE_RANK), lambda b, h, t: (row(b, h, t), 0)),
            pl.BlockSpec((GLA_GATE_RANK, wk), lambda b, h, t: (0, h)),
            pl.BlockSpec((1, wk), lambda b, h, t: (0, h)),
            pl.BlockSpec((1, dv), lambda b, h, t: (0, 0)),
            pl.BlockSpec((GLA_CHUNK, GLA_CHUNK), lambda b, h, t: (0, 0)),
        ],
        out_specs=pl.BlockSpec((ts, wv), lambda b, h, t: (row(b, h, t), h)),
        out_shape=jax.ShapeDtypeStruct((n, GLA_HEADS * dv), BF16),
        scratch_shapes=[pltpu.VMEM((hps, dk, dv), F32)],
        compiler_params=_cparams(("arbitrary", "arbitrary", "arbitrary")),
        name="gla",
    )(proj, proj, proj, proj, glr, gate_up, gate_bias.reshape(1, -1), norm_g.reshape(1, -1), tri)


def _att_kernel(q_ref, k_ref, v_ref, o_ref, lse_ref, *, nb, pairs, unroll):
    blk = ATT_BLOCK
    hd = ATT_HEAD_DIM
    row = lax.broadcasted_iota(jnp.int32, (2 * blk, 2 * blk), 0) % blk
    col = lax.broadcasted_iota(jnp.int32, (2 * blk, 2 * blk), 1)
    dist = row + blk - col
    band = jnp.logical_and(dist >= 0, dist <= blk)
    causal = band[:, blk:]
    head0 = lax.broadcasted_iota(jnp.int32, (blk, LANES), 1) < hd
    neg = -jnp.inf

    def block(j, first):
        r0 = pl.multiple_of(j * blk, blk)
        for p in range(pairs):
            cs = slice(p * LANES, (p + 1) * LANES)
            q2 = q_ref[0, 0, pl.ds(r0, blk), cs]
            zero = jnp.zeros_like(q2)
            qs = jnp.concatenate([jnp.where(head0, q2, zero), jnp.where(head0, zero, q2)], axis=0)
            if first:
                kk = k_ref[0, 0, pl.ds(r0, blk), cs]
                vv = v_ref[0, 0, pl.ds(r0, blk), cs]
                s = jnp.where(causal, _dot_nt(qs, kk), neg)
            else:
                k0 = pl.multiple_of(r0 - blk, blk)
                kk = k_ref[0, 0, pl.ds(k0, 2 * blk), cs]
                vv = v_ref[0, 0, pl.ds(k0, 2 * blk), cs]
                s = jnp.where(band, _dot_nt(qs, kk), neg)
            mx = jnp.max(s, axis=-1, keepdims=True)
            e = jnp.exp(s - mx).astype(BF16)
            one = jnp.ones_like(vv)
            hv = lax.broadcasted_iota(jnp.int32, vv.shape, 1) < hd
            out0 = _dot(e[:blk], jnp.where(hv, vv, one))
            out1 = _dot(e[blk:], jnp.where(hv, one, vv))
            num = jnp.where(head0, out0, out1)
            den = pltpu.roll(jnp.where(head0, out1, out0), hd, 1)
            o_ref[0, 0, pl.ds(r0, blk), cs] = (num / den).astype(o_ref.dtype)
            mxb = jnp.where(head0, mx[:blk], mx[blk:])
            lse_ref[0, 0, pl.ds(r0, blk), cs] = mxb + jnp.log(den)

    block(0, True)

    def body(j, carry):
        block(j, False)
        return carry

    lax.fori_loop(1, nb, body, 0, unroll=unroll)


def _dilated_attention(qkv, dilation, col0):
    batch, dil, sub, width = qkv.shape
    hw = ATT_HEADS_PER_GROUP * ATT_HEAD_DIM
    assert dil == dilation and sub % ATT_BLOCK == 0 and col0 % hw == 0
    nb = sub // ATT_BLOCK
    pairs = 1 if sub >= 4096 else hw // LANES
    unroll = max(1, min(4 // pairs, nb - 1))
    wpb = pairs * LANES
    steps = hw // wpb
    kern = functools.partial(_att_kernel, nb=nb, pairs=pairs, unroll=unroll)

    def in_spec(kind):
        base = (col0 + kind * hw) // wpb
        return pl.BlockSpec((1, 1, sub, wpb), lambda b, r, p: (b, r, 0, base + p))

    out_spec = pl.BlockSpec((1, 1, sub, wpb), lambda b, r, p: (b, r, 0, p))
    return pl.pallas_call(
        kern,
        grid=(batch, dil, steps),
        in_specs=[in_spec(0), in_spec(1), in_spec(2)],
        out_specs=[out_spec, out_spec],
        out_shape=[
            jax.ShapeDtypeStruct((batch, dil, sub, hw), BF16),
            jax.ShapeDtypeStruct((batch, dil, sub, hw), F32),
        ],
        compiler_params=_cparams(("arbitrary", "arbitrary", "arbitrary")),
        name=f"dilated_attention_d{dilation}",
    )(qkv, qkv, qkv)


def _merge_kernel(x_ref, yg_ref, o1_ref, o2_ref, o3_ref, l1_ref, l2_ref, l3_ref,
                  gg0_ref, gg1_ref, ga0_ref, ga1_ref, wbg_ref, wba_ref, wmo_ref, nf_ref, rw_ref, rb_ref, triu_ref,
                  ltri_ref, h_ref, hn_ref, tg_ref, lc_ref, cum_ref, cnt_ref, carry_ref, os_ref, ls_ref):
    i = pl.program_id(0)
    ne = N_EXPERTS

    @pl.when(i == 0)
    def _():
        carry_ref[...] = jnp.zeros_like(carry_ref)

    def token_order(src_ref, stage_ref):
        dil, sub, w = src_ref.shape[1:]
        if dil == 1:
            return src_ref[0, 0].astype(F32)
        for r in range(dil):
            val = src_ref[0, r].astype(F32)
            for s in range(w // LANES):
                stage_ref[s, pl.ds(r, sub, stride=dil), :] = val[:, s * LANES:(s + 1) * LANES]
        return jnp.concatenate([stage_ref[s] for s in range(w // LANES)], axis=1)

    l1, o1 = token_order(l1_ref, None), token_order(o1_ref, None)
    l2, o2 = token_order(l2_ref, ls_ref.at[0]), token_order(o2_ref, os_ref.at[0])
    l3, o3 = token_order(l3_ref, ls_ref.at[1]), token_order(o3_ref, os_ref.at[1])
    m = jnp.maximum(jnp.maximum(l1, l2), l3)
    w1, w2, w3 = jnp.exp(l1 - m), jnp.exp(l2 - m), jnp.exp(l3 - m)
    y_att = (w1 * o1 + w2 * o2 + w3 * o3) / (w1 + w2 + w3)

    m_gla = _dot(yg_ref[...], wbg_ref[...])
    m_att = _dot(y_att.astype(BF16), wba_ref[...])
    gg = jnp.concatenate([gg0_ref[...], gg1_ref[...]], axis=1).astype(F32)
    ga = jnp.concatenate([ga0_ref[...], ga1_ref[...]], axis=1).astype(F32)
    merged = jax.nn.sigmoid(gg) * m_gla + jax.nn.sigmoid(ga) * m_att
    h = x_ref[...] + _dot(merged.astype(BF16), wmo_ref[...])
    h_ref[...] = h
    hn = h * lax.rsqrt(jnp.mean(h * h, axis=-1, keepdims=True) + EPS) * nf_ref[...]
    hn_ref[...] = hn.astype(BF16)

    hh, hl = _split2(hn)
    rh, rl = _split2(rw_ref[...])
    logit = _dot_nt(rh, hh) + _dot_nt(rh, hl) + _dot_nt(rl, hh) + rb_ref[...]
    tm = logit.shape[1]
    eid = lax.broadcasted_iota(jnp.int32, (ne, tm), 0)
    member = jnp.zeros((ne, tm), jnp.bool_)
    vals, idxs = [], []
    for _ in range(TOP_K):
        mx = jnp.max(logit, axis=0, keepdims=True)
        idx = jnp.min(jnp.where(logit == mx, eid, ne), axis=0, keepdims=True)
        sel = eid == idx
        member = jnp.logical_or(member, sel)
        logit = jnp.where(sel, -jnp.inf, logit)
        vals.append(mx)
        idxs.append(idx)
    ex = [jnp.exp(v - vals[0]) for v in vals]
    tot = ex[0] + ex[1] + ex[2] + ex[3]
    memf = jnp.where(member, 1.0, 0.0)
    lens = jnp.sum(memf, axis=1, keepdims=True)
    lens_b = jnp.broadcast_to(lens, (ne, LANES))
    off = _dot(ltri_ref[...], lens_b.astype(BF16))
    pos = _dot(memf.astype(BF16), triu_ref[...]) + off[:, 0:1]
    for k in range(TOP_K):
        tg_ref[0, k:k + 1, :] = ex[k] / tot
        lc_ref[0, k:k + 1, :] = jnp.sum(jnp.where(eid == idxs[k], pos, 0.0), axis=0, keepdims=True).astype(jnp.int32)
    carry = carry_ref[...]
    cum_ref[0] = carry.astype(jnp.int32)
    carry = carry + lens_b
    carry_ref[...] = carry
    cnt_ref[...] = carry.astype(jnp.int32)


def _merge(x2, y_gla, att, proj, layout, wbg, wba, wmo, norm_ffn_g, router_w, router_b, seq, tm):
    n, d = x2.shape
    hw = ATT_HEADS_PER_GROUP * ATT_HEAD_DIM
    half = d // 2
    assert n % tm == 0 and seq % tm == 0
    assert tm <= 256, "per-tile expert counts go through a bf16 matmul operand: exact up to 256"
    assert layout['gate_gla'] % half == 0 and layout['gate_att'] % half == 0
    assert all(tm % (o.shape[1] * 16) == 0 for o, _ in att) and att[0][0].shape[1] == 1
    nt = n // tm
    spt = seq // tm
    ggb = layout['gate_gla'] // half
    gab = layout['gate_att'] // half
    triu = jnp.asarray(np.triu(np.ones((tm, tm), np.float32), 1), BF16)
    row = lambda i: (i, 0)
    const = lambda i: (0, 0)

    def att_spec(a):
        dil = a.shape[1]
        return pl.BlockSpec((1, dil, tm // dil, hw), lambda i: (i // spt, 0, i % spt, 0))

    (o1, l1), (o2, l2), (o3, l3) = att
    in_specs = [
        pl.BlockSpec((tm, d), row),
        pl.BlockSpec((tm, y_gla.shape[1]), row),
        att_spec(o1), att_spec(o2), att_spec(o3), att_spec(l1), att_spec(l2), att_spec(l3),
        pl.BlockSpec((tm, half), lambda i: (i, ggb)), pl.BlockSpec((tm, half), lambda i: (i, ggb + 1)),
        pl.BlockSpec((tm, half), lambda i: (i, gab)), pl.BlockSpec((tm, half), lambda i: (i, gab + 1)),
        pl.BlockSpec(wbg.shape, const), pl.BlockSpec(wba.shape, const), pl.BlockSpec(wmo.shape, const),
        pl.BlockSpec((1, d), const),
        pl.BlockSpec((N_EXPERTS, d), const),
        pl.BlockSpec((N_EXPERTS, 1), const),
        pl.BlockSpec((tm, tm), const),
        pl.BlockSpec((N_EXPERTS, N_EXPERTS), const),
    ]
    out_specs = [
        pl.BlockSpec((tm, d), row),
        pl.BlockSpec((tm, d), row),
        pl.BlockSpec((1, TOP_K, tm), lambda i: (i, 0, 0)),
        pl.BlockSpec((1, TOP_K, tm), lambda i: (i, 0, 0)),
        pl.BlockSpec((1, N_EXPERTS, LANES), lambda i: (i, 0, 0)),
        pl.BlockSpec((N_EXPERTS, LANES), const),
    ]
    out_shape = [
        jax.ShapeDtypeStruct((n, d), F32),
        jax.ShapeDtypeStruct((n, d), BF16),
        jax.ShapeDtypeStruct((nt, TOP_K, tm), F32),
        jax.ShapeDtypeStruct((nt, TOP_K, tm), jnp.int32),
        jax.ShapeDtypeStruct((nt, N_EXPERTS, LANES), jnp.int32),
        jax.ShapeDtypeStruct((N_EXPERTS, LANES), jnp.int32),
    ]
    ltri = jnp.asarray(np.tril(np.ones((N_EXPERTS, N_EXPERTS), np.float32), -1), BF16)
    return pl.pallas_call(
        _merge_kernel,
        grid=(nt,),
        in_specs=in_specs,
        out_specs=out_specs,
        out_shape=out_shape,
        scratch_shapes=[pltpu.VMEM((N_EXPERTS, LANES), F32), pltpu.VMEM((2, hw // LANES, tm, LANES), F32),
                        pltpu.VMEM((2, hw // LANES, tm, LANES), F32)],
        compiler_params=_cparams(("arbitrary",)),
        name="merge_router",
    )(x2, y_gla, o1, o2, o3, l1, l2, l3, proj, proj, proj, proj, wbg, wba, wmo,
      norm_ffn_g.reshape(1, d), router_w.T, router_b.reshape(N_EXPERTS, 1), triu, ltri)


LIN_SUB = 8


def _lin_pack(val):
    return [val[:, c * LANES:(c + 1) * LANES] for c in range(LIN_SUB)]


def _lin_unpack(slabs):
    return jnp.concatenate(slabs, axis=1)


def _dispatch_kernel(off_ref, len_ref, dst_ref, zdst_ref, zlen_ref, nact_ref, hn_ref, loc_ref, x_hbm,
                     xs_ref, zero_ref, sem, zsem, *, rloc):
    i = pl.program_id(0)
    nt = pl.num_programs(0)
    slot = i % 2
    ne = N_EXPERTS

    def drain(s):
        rows = rloc * LIN_SUB
        pltpu.make_async_copy(xs_ref.at[s], x_hbm.at[pl.ds(0, rows)], sem.at[s]).wait()

    @pl.when(i == 0)
    def _():
        zero_ref[...] = jnp.zeros_like(zero_ref)

        def fill_copy(e):
            rows = zlen_ref[e] * LIN_SUB
            dst = pl.multiple_of(zdst_ref[e] * LIN_SUB, 8)
            return rows, pltpu.make_async_copy(zero_ref.at[pl.ds(0, rows)], x_hbm.at[pl.ds(dst, rows)], zsem)

        def fill(e, carry):
            rows, copy = fill_copy(e)
            pl.when(rows > 0)(copy.start)
            return carry

        def fill_done(e, carry):
            rows, copy = fill_copy(e)
            pl.when(rows > 0)(copy.wait)
            return carry

        def block_copy(b):
            rows = zero_ref.shape[0]
            return pltpu.make_async_copy(zero_ref, x_hbm.at[pl.ds(pl.multiple_of(b * rows, rows), rows)], zsem)

        def fill_block(b, carry):
            block_copy(b).start()
            return carry

        def fill_block_done(b, carry):
            block_copy(b).wait()
            return carry

        nblk = x_hbm.shape[0] // zero_ref.shape[0]
        lax.fori_loop(0, ne, fill, 0)
        lax.fori_loop(nact_ref[0], nblk, fill_block, 0)
        lax.fori_loop(0, ne, fill_done, 0)
        lax.fori_loop(nact_ref[0], nblk, fill_block_done, 0)

    @pl.when(i >= 2)
    def _():
        drain(slot)

    loc = loc_ref[0]
    tt = loc.shape[1]
    row_id = lax.broadcasted_iota(jnp.int32, (rloc, tt), 0)
    hit = row_id == loc[0:1, :]
    for k in range(1, TOP_K):
        hit = jnp.logical_or(hit, row_id == loc[k:k + 1, :])
    onehot = jnp.where(hit, 1.0, 0.0).astype(BF16)
    xs = _dot(onehot, hn_ref[...])
    for c, slab in enumerate(_lin_pack(xs)):
        xs_ref[slot, pl.ds(c, rloc, stride=LIN_SUB), :] = slab

    def send(e, carry):
        rows = len_ref[i * ne + e] * LIN_SUB

        @pl.when(rows > 0)
        def _():
            src = pl.multiple_of(off_ref[i * ne + e] * LIN_SUB, 8)
            dst = pl.multiple_of(dst_ref[i * ne + e] * LIN_SUB, 8)
            pltpu.make_async_copy(xs_ref.at[slot, pl.ds(src, rows)], x_hbm.at[pl.ds(dst, rows)], sem.at[slot]).start()
        return carry

    lax.fori_loop(0, ne, send, 0)

    @pl.when(i == nt - 1)
    def _():
        @pl.when(i >= 1)
        def _():
            drain(1 - slot)
        drain(slot)


def _dispatch(hn, loc, tabs, nact, p_rows, rloc):
    n, d = hn.shape
    nt, _, tt = loc.shape
    assert d == LIN_SUB * LANES and p_rows % MOE_BLOCK_ROWS == 0
    kern = functools.partial(_dispatch_kernel, rloc=rloc)
    grid_spec = pltpu.PrefetchScalarGridSpec(
        num_scalar_prefetch=6,
        grid=(nt,),
        in_specs=[
            pl.BlockSpec((tt, d), lambda i, *_: (i, 0)),
            pl.BlockSpec((1, TOP_K, tt), lambda i, *_: (i, 0, 0)),
        ],
        out_specs=pl.BlockSpec(memory_space=pl.ANY),
        scratch_shapes=[
            pltpu.VMEM((2, rloc * LIN_SUB, LANES), F32),
            pltpu.VMEM((MOE_BLOCK_ROWS * LIN_SUB, LANES), F32),
            pltpu.SemaphoreType.DMA((2,)),
            pltpu.SemaphoreType.DMA(()),
        ],
    )
    return pl.pallas_call(
        kern,
        grid_spec=grid_spec,
        out_shape=jax.ShapeDtypeStruct((p_rows * LIN_SUB, LANES), F32),
        compiler_params=_cparams(("arbitrary",)),
        name="moe_dispatch",
    )(tabs['off'], tabs['len'], tabs['dst'], tabs['zdst'], tabs['zlen'], nact, hn, loc)


def _expert_kernel(bexp_ref, nact_ref, first_ref, next_ref, x_ref, wgu_hbm, wd_hbm, bgu_ref, bd_ref, perm_ref,
                   y_ref, wgu_f, wd_f, wgu_s, wd_s, sem, *, bm):
    grp = MXU_DIM
    ff2 = wgu_s.shape[1]
    halfg = grp // 2
    brows = bm * LIN_SUB
    bps = x_ref.shape[0] // brows

    def fetch(ex):
        return (pltpu.make_async_copy(wgu_hbm.at[ex], wgu_f, sem.at[0]),
                pltpu.make_async_copy(wd_hbm.at[ex], wd_f, sem.at[1]))

    @pl.when(pl.program_id(0) == 0)
    def _():
        for copy in fetch(bexp_ref[0]):
            copy.start()

    def block(b, r0):
        e = bexp_ref[b]
        active = b < nact_ref[0]

        @pl.when(jnp.logical_and(active, first_ref[b] == 1))
        def _():
            for copy in fetch(e):
                copy.wait()
            perm = perm_ref[...]
            for g in range(ff2 // grp):
                cols = slice(g * grp, (g + 1) * grp)
                wgu_s[:, cols] = _dot(wgu_f[:, cols].astype(BF16), perm).astype(BF16)
            wd_s[...] = wd_f[...].astype(BF16)
            nxt = next_ref[b]

            @pl.when(nxt >= 0)
            def _():
                for copy in fetch(nxt):
                    copy.start()

        @pl.when(active)
        def _():
            x = _lin_unpack([x_ref[pl.ds(r0 + c, bm, stride=LIN_SUB), :] for c in range(LIN_SUB)]).astype(BF16)
            acts = []
            for g in range(ff2 // grp):
                cols = slice(g * grp, (g + 1) * grp)
                hg = _dot(x, wgu_s[:, cols]) + bgu_ref[e, :, cols]
                gate = jnp.minimum(hg[:, :halfg], SWIGLU_LIMIT)
                up = jnp.clip(hg[:, halfg:], -SWIGLU_LIMIT, SWIGLU_LIMIT)
                acts.append(((up + 1.0) * (gate * jax.nn.sigmoid(SWIGLU_ALPHA * gate))).astype(BF16))
            act = jnp.concatenate(acts, axis=1)
            y = _dot(act, wd_s[...]) + bd_ref[e]
            for c, slab in enumerate(_lin_pack(y)):
                y_ref[pl.ds(r0 + c, bm, stride=LIN_SUB), :] = slab

        @pl.when(jnp.logical_not(active))
        def _():
            y_ref[pl.ds(r0, brows), :] = jnp.zeros((brows, y_ref.shape[1]), y_ref.dtype)

    for sub in range(bps):
        block(pl.program_id(0) * bps + sub, sub * brows)


def _gate_up_permutation():
    grp = MXU_DIM
    halfg = grp // 2
    out = np.arange(grp)
    src = np.where(out < halfg, 2 * out, 2 * (out - halfg) + 1)
    p = np.zeros((grp, grp), np.float32)
    p[src, out] = 1.0
    return jnp.asarray(p, BF16)


def _experts(x_lin, bexp, nact, first, nxt, w_gate_up, b_gate_up, w_down, b_down, bm):
    ne, d, ff2 = w_gate_up.shape
    ff = ff2 // 2
    grp = MXU_DIM
    assert ff2 % grp == 0 and d == LIN_SUB * LANES
    nblk = x_lin.shape[0] // (bm * LIN_SUB)
    bps = 2 if nblk % 2 == 0 else 1
    srows = bps * bm * LIN_SUB
    bgu = b_gate_up.reshape(ne, ff2 // grp, grp // 2, 2).transpose(0, 1, 3, 2).reshape(ne, 1, ff2)
    grid_spec = pltpu.PrefetchScalarGridSpec(
        num_scalar_prefetch=4,
        grid=(nblk // bps,),
        in_specs=[
            pl.BlockSpec((srows, LANES), lambda i, *_: (i, 0)),
            pl.BlockSpec(memory_space=pl.ANY),
            pl.BlockSpec(memory_space=pl.ANY),
            pl.BlockSpec(memory_space=pltpu.VMEM),
            pl.BlockSpec(memory_space=pltpu.VMEM),
            pl.BlockSpec((grp, grp), lambda i, *_: (0, 0)),
        ],
        out_specs=pl.BlockSpec((srows, LANES), lambda i, *_: (i, 0)),
        scratch_shapes=[pltpu.VMEM((d, ff2), F32), pltpu.VMEM((ff, d), F32),
                        pltpu.VMEM((d, ff2), BF16), pltpu.VMEM((ff, d), BF16),
                        pltpu.SemaphoreType.DMA((2,))],
    )
    return pl.pallas_call(
        functools.partial(_expert_kernel, bm=bm),
        grid_spec=grid_spec,
        out_shape=jax.ShapeDtypeStruct(x_lin.shape, F32),
        compiler_params=_cparams(("arbitrary",)),
        name="moe_experts",
    )(bexp, nact, first, nxt, x_lin, w_gate_up, w_down, bgu, b_down.reshape(ne, 1, d), _gate_up_permutation())


def _combine_kernel(off_ref, len_ref, dst_ref, h_ref, loc_ref, gate_ref, g_ref, y_hbm, out_ref,
                    ybuf, sem, *, rloc):
    i = pl.program_id(0)
    nt = pl.num_programs(0)
    slot = i % 2
    ne = N_EXPERTS

    def fetch(step, s):
        def run(e, carry):
            rows = len_ref[step * ne + e] * LIN_SUB

            @pl.when(rows > 0)
            def _():
                src = pl.multiple_of(dst_ref[step * ne + e] * LIN_SUB, 8)
                dst = pl.multiple_of(off_ref[step * ne + e] * LIN_SUB, 8)
                pltpu.make_async_copy(y_hbm.at[pl.ds(src, rows)], ybuf.at[s, pl.ds(dst, rows)], sem.at[s]).start()
            return carry

        lax.fori_loop(0, ne, run, 0)

    @pl.when(i == 0)
    def _():
        fetch(0, 0)

    @pl.when(i + 1 < nt)
    def _():
        fetch(i + 1, 1 - slot)

    pltpu.make_async_copy(y_hbm.at[pl.ds(0, rloc * LIN_SUB)], ybuf.at[slot], sem.at[slot]).wait()

    y = _lin_unpack([ybuf[slot, pl.ds(c, rloc, stride=LIN_SUB), :] for c in range(LIN_SUB)]).astype(BF16)
    loc = loc_ref[0]
    gate = gate_ref[0]
    tt = loc.shape[1]
    row_id = lax.broadcasted_iota(jnp.int32, (rloc, tt), 0)
    w = jnp.zeros((rloc, tt), F32)
    for k in range(TOP_K):
        w = w + jnp.where(row_id == loc[k:k + 1, :], gate[k:k + 1, :], 0.0)
    h = h_ref[...] + _dot_tn(w.astype(BF16), y)
    out_ref[...] = h * lax.rsqrt(jnp.mean(h * h, axis=-1, keepdims=True) + EPS) * g_ref[...]


def _combine(h, y_lin, loc, gates, tabs, norm_g, rloc):
    n, d = h.shape
    nt, _, tt = loc.shape
    kern = functools.partial(_combine_kernel, rloc=rloc)
    grid_spec = pltpu.PrefetchScalarGridSpec(
        num_scalar_prefetch=3,
        grid=(nt,),
        in_specs=[
            pl.BlockSpec((tt, d), lambda i, *_: (i, 0)),
            pl.BlockSpec((1, TOP_K, tt), lambda i, *_: (i, 0, 0)),
            pl.BlockSpec((1, TOP_K, tt), lambda i, *_: (i, 0, 0)),
            pl.BlockSpec((1, d), lambda i, *_: (0, 0)),
            pl.BlockSpec(memory_space=pl.ANY),
        ],
        out_specs=pl.BlockSpec((tt, d), lambda i, *_: (i, 0)),
        scratch_shapes=[
            pltpu.VMEM((2, rloc * LIN_SUB, LANES), F32),
            pltpu.SemaphoreType.DMA((2,)),
        ],
    )
    return pl.pallas_call(
        kern,
        grid_spec=grid_spec,
        out_shape=jax.ShapeDtypeStruct((n, d), F32),
        compiler_params=_cparams(("arbitrary",)),
        name="moe_combine",
    )(tabs['off'], tabs['len'], tabs['dst'], h, loc, gates, norm_g.reshape(1, d), y_lin)


def _routing_tables(cum, counts, bm, nblk):
    padded = (counts + bm - 1) // bm * bm
    pend = jnp.cumsum(padded)
    pstart = pend - padded
    blk = jnp.arange(nblk, dtype=jnp.int32)
    bexp_raw = jnp.sum((pend // bm)[None, :] <= blk[:, None], axis=1).astype(jnp.int32)
    nact = (pend[-1] // bm).astype(jnp.int32)
    last = jnp.minimum(bexp_raw[jnp.maximum(nact - 1, 0)], N_EXPERTS - 1)
    bexp = jnp.where(blk < nact, jnp.minimum(bexp_raw, N_EXPERTS - 1), last)
    prev = jnp.concatenate([jnp.full((1,), -1, jnp.int32), bexp[:-1]])
    first = jnp.logical_and(bexp != prev, blk < nact)
    later = jnp.logical_and(first[None, :], blk[None, :] > blk[:, None])
    nxt_blk = jnp.min(jnp.where(later, blk[None, :], nblk), axis=1)
    nxt = jnp.where(nxt_blk < nblk, bexp[jnp.minimum(nxt_blk, nblk - 1)], -1)
    cum_full = jnp.concatenate([cum, counts[None, :]], axis=0)
    run_len = cum_full[1:] - cum_full[:-1]
    run_off = jnp.cumsum(run_len, axis=1) - run_len
    tabs = {
        'off': run_off.reshape(-1), 'len': run_len.reshape(-1),
        'dst': (pstart[None, :] + cum_full[:-1]).reshape(-1),
        'zdst': pstart + counts, 'zlen': padded - counts,
    }
    blocks = (bexp, nact.reshape(1), first.astype(jnp.int32), nxt.astype(jnp.int32))
    return {k: v.astype(jnp.int32) for k, v in tabs.items()}, blocks


MOE_BLOCK_ROWS = 256
TOKEN_TILE = 256


def kernel(x, norm_mix_g, w_in, gla_gate_up, gla_gate_bias, gla_norm_g, w_branch_gla, w_branch_att, w_mix_out,
           norm_ffn_g, router_w, router_b, expert_w_gate_up, expert_b_gate_up, expert_w_down, expert_b_down,
           norm_final_g):
    batch, seq, d = x.shape
    assert norm_mix_g.shape[0] == 1, "single layer"
    n = batch * seq
    gla_k = gla_gate_up.shape[2]
    gla_v = w_branch_gla.shape[1]
    att_w = len(ATT_GROUPS) * ATT_HEADS_PER_GROUP * ATT_HEAD_DIM
    hw = ATT_HEADS_PER_GROUP * ATT_HEAD_DIM
    ngrp = len(ATT_GROUPS)
    assert att_w == ngrp * hw and all(s % hw == 0 for s in (gla_k, gla_v, d))
    assert ATT_GROUPS[0][1] == 1 and all(win // dil == ATT_BLOCK for win, dil in ATT_GROUPS)
    src, off = {}, 0
    for name, size in (('gq', gla_k), ('gk', gla_k), ('gv', gla_v), ('gr', gla_v), ('glr', GLA_GATE_RANK),
                       ('aq', att_w), ('ak', att_w), ('av', att_w), ('gate_gla', d), ('gate_att', d)):
        src[name] = (off, size)
        off += size
    assert off == w_in.shape[2]
    w = w_in[0]
    glr_at, glr_n = src['glr']
    w_main = jnp.concatenate([w[:, :glr_at], w[:, glr_at + glr_n:]], axis=1).astype(BF16)
    w_glr = w[:, glr_at:glr_at + glr_n]

    def col(name, group=0):
        start, _ = src[name]
        return (start if start < glr_at else start - glr_n) + group * hw

    plain = (('gq', gla_k), ('gk', gla_k), ('gv', gla_v), ('gr', gla_v), ('gate_gla', d), ('gate_att', d),
             ('aq', hw), ('ak', hw), ('av', hw))
    layout, tile_cols = {}, []
    for name, size in plain:
        layout[name] = len(tile_cols) * hw
        tile_cols += [col(name) + t * hw for t in range(size // hw)]
    n_plain = len(tile_cols)
    tile_cols += [col(kind, g) for g in range(1, ngrp) for kind in ('aq', 'ak', 'av')]
    dilations = [dil for _, dil in ATT_GROUPS[1:]]

    x2 = x.reshape(n, d)
    proj, glr, *qkv_dil = _in_projection(x2, norm_mix_g[0], w_main, w_glr, batch, seq, n_plain, dilations,
                                         tuple(tile_cols))
    y_gla = _gla(proj, glr, gla_gate_up[0], gla_gate_bias[0], gla_norm_g[0], batch, seq, layout)
    att = [_dilated_attention(proj.reshape(batch, 1, seq, proj.shape[1]), 1, layout['aq'])]
    att += [_dilated_attention(a, dil, 0) for a, dil in zip(qkv_dil, dilations)]
    tt = min(TOKEN_TILE, n)
    h, hn, gates, loc, cum, cnt = _merge(
        x2, y_gla, att, proj, layout, w_branch_gla[0].astype(BF16), w_branch_att[0].astype(BF16),
        w_mix_out[0].astype(BF16), norm_ffn_g[0], router_w[0], router_b[0], seq, tt)

    bm = MOE_BLOCK_ROWS
    rloc = TOP_K * tt
    nblk = -(-(TOP_K * n) // bm) + N_EXPERTS
    tabs, blocks = _routing_tables(cum[:, :, 0], cnt[:, 0], bm, nblk)
    x_lin = _dispatch(hn, loc, tabs, blocks[1], nblk * bm, rloc)
    y_lin = _experts(x_lin, *blocks, expert_w_gate_up[0], expert_b_gate_up[0], expert_w_down[0],
                     expert_b_down[0], bm)
    out = _combine(h, y_lin, loc, gates, tabs, norm_final_g, rloc)
    return out.reshape(batch, seq, d)
```

```python
import functools
import math

import jax
import jax.numpy as jnp
import numpy as np
from jax import lax
from jax.experimental import pallas as pl
from jax.experimental.pallas import tpu as pltpu

GLA_HEADS = 4
GLA_GATE_RANK = 16
GLA_TAU = 16.0
GLA_CHUNK = 64
ATT_GROUPS = ((128, 1), (512, 4), (2048, 16))
ATT_HEADS_PER_GROUP = 8
ATT_HEAD_DIM = 64
ATT_BLOCK = 128
ROT_DIM = ATT_HEAD_DIM // 4
ROPE_THETA = 500000.0
N_EXPERTS = 32
TOP_K = 4
SWIGLU_LIMIT = 7.0
SWIGLU_ALPHA = 1.702
EPS = 1e-5

LANES = 128
MXU_DIM = 256
VMEM_LIMIT = 56 * 1024 * 1024

F32 = jnp.float32
BF16 = jnp.bfloat16


def _cparams(sem):
    return pltpu.CompilerParams(dimension_semantics=sem, vmem_limit_bytes=VMEM_LIMIT)


def _dot(a, b):
    return jnp.dot(a, b, preferred_element_type=F32)


def _dot_nt(a, b):
    return lax.dot_general(a, b, (((1,), (1,)), ((), ())), preferred_element_type=F32)


def _dot_tn(a, b):
    return lax.dot_general(a, b, (((0,), (0,)), ((), ())), preferred_element_type=F32)


def _split2(a):
    hi = a.astype(BF16)
    lo = (a - hi.astype(F32)).astype(BF16)
    return hi, lo


def _inproj_kernel(x_ref, g_ref, w_ref, wglr_ref, cos_ref, sa_ref, sb_ref, proj_ref, glr_ref, *rest,
                   n_plain, dilations, q_scale, tn, tile_cols):
    dil_refs, (xn_ref, stage_ref) = rest[:len(dilations)], rest[len(dilations):]
    x = x_ref[...]
    y = x * lax.rsqrt(jnp.mean(x * x, axis=-1, keepdims=True) + EPS) * g_ref[...]
    xn_ref[...] = y.astype(BF16)
    yh, yl = _split2(y)
    wh, wl = _split2(wglr_ref[...])
    glr_ref[...] = _dot(yh, wh) + _dot(yh, wl) + _dot(yl, wh)

    tm = xn_ref.shape[0]
    half = ROT_DIM // 2
    nslab = tn // LANES

    def slabs(j, kind):
        acc = _dot(xn_ref[...], w_ref[:, tile_cols[j]:tile_cols[j] + tn])
        for s in range(nslab):
            t = acc[:, s * LANES:(s + 1) * LANES]
            if kind < 2:
                t = t * cos_ref[...] + pltpu.roll(t, LANES - half, 1) * sa_ref[...] \
                    + pltpu.roll(t, half, 1) * sb_ref[...]
                if kind == 0:
                    t = t * q_scale
            yield s, t

    q0 = n_plain - 3
    for j in range(n_plain):
        for s, t in slabs(j, j - q0 if j >= q0 else 2):
            proj_ref[:, j * tn + s * LANES:j * tn + (s + 1) * LANES] = t.astype(BF16)

    for gi, dil in enumerate(dilations):
        for kind in range(3):
            j = n_plain + 3 * gi + kind
            buf = (3 * gi + kind) % stage_ref.shape[0]
            for s, t in slabs(j, kind):
                stage_ref[buf, s] = t
            for r in range(dil):
                for s in range(nslab):
                    dil_refs[gi][0, r, :, kind * tn + s * LANES:kind * tn + (s + 1) * LANES] = \
                        stage_ref[buf, s, pl.ds(r, tm // dil, stride=dil), :].astype(BF16)


def _rope_tables(seq):
    half = ROT_DIM // 2
    inv_freq = ROPE_THETA ** (-np.arange(half, dtype=np.float32) * np.float32(2.0 / ROT_DIM))
    pos = jnp.arange(seq, dtype=F32)
    ang = pos[:, None] * jnp.asarray(inv_freq, F32)[None, :]
    cos = jnp.cos(ang)
    sin = jnp.sin(ang)
    lane = np.arange(LANES) % ATT_HEAD_DIM
    idx = np.where(lane < ROT_DIM, lane % half, 0)
    in_rot = jnp.asarray(lane < ROT_DIM)
    first = jnp.asarray(lane < half)
    second = jnp.asarray((lane >= half) & (lane < ROT_DIM))
    cos_t = jnp.where(in_rot[None, :], cos[:, idx], 1.0)
    sin_g = sin[:, idx]
    sa = jnp.where(first[None, :], -sin_g, 0.0)
    sb = jnp.where(second[None, :], sin_g, 0.0)
    return cos_t.astype(F32), sa.astype(F32), sb.astype(F32)


def _in_projection(x2, norm_g, w_main, w_glr, batch, seq, n_plain, dilations, tile_cols):
    n, d = x2.shape
    tn = ATT_HEADS_PER_GROUP * ATT_HEAD_DIM
    tm = min(512, seq)
    width = w_main.shape[1]
    assert n % tm == 0 and seq % tm == 0 and len(tile_cols) == n_plain + 3 * len(dilations)
    assert all(c % LANES == 0 and c + tn <= width for c in tile_cols)
    assert all(tm % (dil * 16) == 0 for dil in dilations)
    cos_t, sa, sb = _rope_tables(seq)
    spt = seq // tm
    kern = functools.partial(_inproj_kernel, n_plain=n_plain, dilations=tuple(dilations),
                             q_scale=ATT_HEAD_DIM ** -0.5, tn=tn, tile_cols=tile_cols)
    out_specs = [
        pl.BlockSpec((tm, n_plain * tn), lambda i: (i, 0)),
        pl.BlockSpec((tm, GLA_GATE_RANK), lambda i: (i, 0)),
    ]
    out_shape = [
        jax.ShapeDtypeStruct((n, n_plain * tn), BF16),
        jax.ShapeDtypeStruct((n, GLA_GATE_RANK), F32),
    ]
    for dil in dilations:
        out_specs.append(pl.BlockSpec((1, dil, tm // dil, 3 * tn), lambda i: (i // spt, 0, i % spt, 0)))
        out_shape.append(jax.ShapeDtypeStruct((batch, dil, seq // dil, 3 * tn), BF16))
    const = lambda i: (0, 0)
    return pl.pallas_call(
        kern,
        grid=(n // tm,),
        in_specs=[
            pl.BlockSpec((tm, d), lambda i: (i, 0)),
            pl.BlockSpec((1, d), const),
            pl.BlockSpec((d, width), const, pipeline_mode=pl.Buffered(1)),
            pl.BlockSpec((d, GLA_GATE_RANK), const),
            pl.BlockSpec((tm, LANES), lambda i: (i % spt, 0)),
            pl.BlockSpec((tm, LANES), lambda i: (i % spt, 0)),
            pl.BlockSpec((tm, LANES), lambda i: (i % spt, 0)),
        ],
        out_specs=out_specs,
        out_shape=out_shape,
        scratch_shapes=[pltpu.VMEM((tm, d), BF16), pltpu.VMEM((2, tn // LANES, tm, LANES), F32)],
        compiler_params=_cparams(("arbitrary",)),
        name="in_projection",
    )(x2, norm_g.reshape(1, d), w_main, w_glr, cos_t, sa, sb)


def _gla_kernel(q_ref, k_ref, v_ref, r_ref, glr_ref, up_ref, bias_ref, ng_ref, tri_ref, y_ref, state_ref,
                *, dk, dv, ts):
    t = pl.program_id(2)

    @pl.when(t == 0)
    def _():
        state_ref[...] = jnp.zeros_like(state_ref)

    c = GLA_CHUNK
    hps = state_ref.shape[0]
    gh, gl = _split2(glr_ref[...])
    uh, ul = _split2(up_ref[...])
    z = _dot(gh, uh) + _dot(gh, ul) + _dot(gl, uh) + bias_ref[...]
    logdec = (jnp.minimum(z, 0.0) - jnp.log1p(jnp.exp(-jnp.abs(z)))) * (1.0 / GLA_TAU)
    ldh, ldl = _split2(logdec)
    tri = tri_ref[...]
    row = lax.broadcasted_iota(jnp.int32, (c, c), 0)
    col = lax.broadcasted_iota(jnp.int32, (c, c), 1)
    causal = col <= row
    scale = dk ** -0.5
    ng = ng_ref[...]
    states = [state_ref[hh] for hh in range(hps)]
    for ci in range(ts // c):
        sl = slice(ci * c, (ci + 1) * c)
        b2 = _dot(tri, ldh[sl]) + _dot(tri, ldl[sl])
        for hh in range(hps):
            ks = slice(hh * dk, (hh + 1) * dk)
            vs = slice(hh * dv, (hh + 1) * dv)
            b = b2[:, ks]
            bl = b[c - 1:c, :]
            eb = jnp.exp(b)
            enb = jnp.exp(-b)
            ebl = jnp.exp(bl)
            qf = q_ref[sl, ks].astype(F32)
            kf = k_ref[sl, ks].astype(F32)
            qe = (qf * scale * eb).astype(BF16)
            ke = (kf * enb).astype(BF16)
            kd = (kf * enb * ebl).astype(BF16)
            vb = v_ref[sl, vs]
            a = jnp.where(causal, _dot_nt(qe, ke), 0.0).astype(BF16)
            o = _dot(a, vb) + _dot(qe, states[hh].astype(BF16))
            dec = jnp.transpose(jnp.broadcast_to(ebl, (dk, dk)))
            dec_full = jnp.concatenate([dec] * (dv // dk), axis=1)
            states[hh] = dec_full * states[hh] + _dot_tn(kd, vb)
            o = o * lax.rsqrt(jnp.mean(o * o, axis=-1, keepdims=True) + EPS) * ng
            rf = r_ref[sl, vs].astype(F32)
            o = o * (rf * jax.nn.sigmoid(rf))
            y_ref[sl, vs] = o.astype(BF16)
    for hh in range(hps):
        state_ref[hh] = states[hh]


def _gla(proj, glr, gate_up, gate_bias, norm_g, batch, seq, layout):
    n = proj.shape[0]
    dk = (layout['gk'] - layout['gq']) // GLA_HEADS
    dv = (layout['gr'] - layout['gv']) // GLA_HEADS
    ts = min(512, seq)
    assert dk == LANES and dv % dk == 0 and seq % ts == 0 and ts % GLA_CHUNK == 0
    spt = seq // ts
    hps = GLA_HEADS
    wk, wv = hps * dk, hps * dv
    assert GLA_HEADS % hps == 0 and all(layout[s] % wk == 0 for s in ('gq', 'gk'))
    assert all(layout[s] % wv == 0 for s in ('gv', 'gr'))
    qb, kb = layout['gq'] // wk, layout['gk'] // wk
    vb, rb = layout['gv'] // wv, layout['gr'] // wv
    tri = jnp.asarray(np.tril(np.ones((GLA_CHUNK, GLA_CHUNK), np.float32)), BF16)
    kern = functools.partial(_gla_kernel, dk=dk, dv=dv, ts=ts)
    row = lambda b, h, t: b * spt + t
    return pl.pallas_call(
        kern,
        grid=(batch, GLA_HEADS // hps, spt),
        in_specs=[
            pl.BlockSpec((ts, wk), lambda b, h, t: (row(b, h, t), qb + h)),
            pl.BlockSpec((ts, wk), lambda b, h, t: (row(b, h, t), kb + h)),
            pl.BlockSpec((ts, wv), lambda b, h, t: (row(b, h, t), vb + h)),
            pl.BlockSpec((ts, wv), lambda b, h, t: (row(b, h, t), rb + h)),
            pl.BlockSpec((ts, GLA_GATE_RANK), lambda b, h, t: (row(b, h, t), 0)),
            pl.BlockSpec((GLA_GATE_RANK, wk), lambda b, h, t: (0, h)),
            pl.BlockSpec((1, wk), lambda b, h, t: (0, h)),
            pl.BlockSpec((1, dv), lambda b, h, t: (0, 0)),
            pl.BlockSpec((GLA_CHUNK, GLA_CHUNK), lambda b, h, t: (0, 0)),
        ],
        out_specs=pl.BlockSpec((ts, wv), lambda b, h, t: (row(b, h, t), h)),
        out_shape=jax.ShapeDtypeStruct((n, GLA_HEADS * dv), BF16),
        scratch_shapes=[pltpu.VMEM((hps, dk, dv), F32)],
        compiler_params=_cparams(("arbitrary", "arbitrary", "arbitrary")),
        name="gla",
    )(proj, proj, proj, proj, glr, gate_up, gate_bias.reshape(1, -1), norm_g.reshape(1, -1), tri)


LSE_REP = LANES // ATT_HEADS_PER_GROUP


def _att_kernel(q_ref, k_ref, v_ref, o_ref, lse_ref, *, nb, pairs, unroll):
    blk = ATT_BLOCK
    hd = ATT_HEAD_DIM
    row = lax.broadcasted_iota(jnp.int32, (2 * blk, 2 * blk), 0) % blk
    col = lax.broadcasted_iota(jnp.int32, (2 * blk, 2 * blk), 1)
    dist = row + blk - col
    band = jnp.logical_and(dist >= 0, dist <= blk)
    causal = band[:, blk:]
    lane = lax.broadcasted_iota(jnp.int32, (blk, LANES), 1)
    head0 = lane < hd
    lane_head = lane // LSE_REP
    neg = -jnp.inf

    def block(j, first):
        r0 = pl.multiple_of(j * blk, blk)
        lse = jnp.zeros((blk, LANES), F32)
        for p in range(pairs):
            cs = slice(p * LANES, (p + 1) * LANES)
            q2 = q_ref[0, 0, pl.ds(r0, blk), cs]
            zero = jnp.zeros_like(q2)
            qs = jnp.concatenate([jnp.where(head0, q2, zero), jnp.where(head0, zero, q2)], axis=0)
            if first:
                kk = k_ref[0, 0, pl.ds(r0, blk), cs]
                vv = v_ref[0, 0, pl.ds(r0, blk), cs]
                s = jnp.where(causal, _dot_nt(qs, kk), neg)
            else:
                k0 = pl.multiple_of(r0 - blk, blk)
                kk = k_ref[0, 0, pl.ds(k0, 2 * blk), cs]
                vv = v_ref[0, 0, pl.ds(k0, 2 * blk), cs]
                s = jnp.where(band, _dot_nt(qs, kk), neg)
            mx = jnp.max(s, axis=-1, keepdims=True)
            e = jnp.exp(s - mx).astype(BF16)
            one = jnp.ones_like(vv)
            hv = lax.broadcasted_iota(jnp.int32, vv.shape, 1) < hd
            out0 = _dot(e[:blk], jnp.where(hv, vv, one))
            out1 = _dot(e[blk:], jnp.where(hv, one, vv))
            num = jnp.where(head0, out0, out1)
            den_x = jnp.where(head0, out1, out0)
            den = pltpu.roll(den_x, hd, 1)
            o_ref[0, 0, pl.ds(r0, blk), cs] = (num / den).astype(o_ref.dtype)
            lse0 = mx[:blk] + jnp.log(jnp.where(head0, den, den_x))
            lse1 = mx[blk:] + jnp.log(jnp.where(head0, den_x, den))
            lse = jnp.where(lane_head == 2 * p, lse0, jnp.where(lane_head == 2 * p + 1, lse1, lse))
        lse_ref[0, 0, pl.ds(r0, blk), :] = lse

    block(0, True)

    def body(j, carry):
        block(j, False)
        return carry

    lax.fori_loop(1, nb, body, 0, unroll=unroll)


def _dilated_attention(qkv, dilation, col0):
    batch, dil, sub, width = qkv.shape
    hw = ATT_HEADS_PER_GROUP * ATT_HEAD_DIM
    assert dil == dilation and sub % ATT_BLOCK == 0 and col0 % hw == 0
    nb = sub // ATT_BLOCK
    pairs = hw // LANES
    assert 2 * pairs * LSE_REP == LANES
    kern = functools.partial(_att_kernel, nb=nb, pairs=pairs, unroll=1)

    def in_spec(kind):
        base = (col0 + kind * hw) // hw
        return pl.BlockSpec((1, 1, sub, hw), lambda b, r: (b, r, 0, base))

    return pl.pallas_call(
        kern,
        grid=(batch, dil),
        in_specs=[in_spec(0), in_spec(1), in_spec(2)],
        out_specs=[pl.BlockSpec((1, 1, sub, hw), lambda b, r: (b, r, 0, 0)),
                   pl.BlockSpec((1, 1, sub, LANES), lambda b, r: (b, r, 0, 0))],
        out_shape=[
            jax.ShapeDtypeStruct((batch, dil, sub, hw), BF16),
            jax.ShapeDtypeStruct((batch, dil, sub, LANES), F32),
        ],
        compiler_params=_cparams(("arbitrary", "arbitrary")),
        name=f"dilated_attention_d{dilation}",
    )(qkv, qkv, qkv)


def _merge_kernel(x_ref, yg_ref, o1_ref, o2_ref, o3_ref, l1_ref, l2_ref, l3_ref,
                  gg0_ref, gg1_ref, ga0_ref, ga1_ref, wbg_ref, wba_ref, wmo_ref, nf_ref, rw_ref, rb_ref, triu_ref,
                  ltri_ref, spread_ref, h_ref, hn_ref, tg_ref, lc_ref, cum_ref, cnt_ref, carry_ref, os_ref, ls_ref):
    i = pl.program_id(0)
    ne = N_EXPERTS

    @pl.when(i == 0)
    def _():
        carry_ref[...] = jnp.zeros_like(carry_ref)

    def token_order(src_ref, stage_ref):
        dil, sub, w = src_ref.shape[1:]
        if dil == 1:
            return src_ref[0, 0].astype(F32)
        for r in range(dil):
            val = src_ref[0, r].astype(F32)
            for s in range(w // LANES):
                stage_ref[s, pl.ds(r, sub, stride=dil), :] = val[:, s * LANES:(s + 1) * LANES]
        return jnp.concatenate([stage_ref[s] for s in range(w // LANES)], axis=1)

    l1, o1 = token_order(l1_ref, None), token_order(o1_ref, None)
    l2, o2 = token_order(l2_ref, ls_ref.at[0]), token_order(o2_ref, os_ref.at[0])
    l3, o3 = token_order(l3_ref, ls_ref.at[1]), token_order(o3_ref, os_ref.at[1])
    m = jnp.maximum(jnp.maximum(l1, l2), l3)
    w1, w2, w3 = jnp.exp(l1 - m), jnp.exp(l2 - m), jnp.exp(l3 - m)
    inv = 1.0 / (w1 + w2 + w3)
    spread = spread_ref[...]

    def per_lane(w):
        hi, lo = _split2(w * inv)
        return _dot(hi, spread) + _dot(lo, spread)

    y_att = per_lane(w1) * o1 + per_lane(w2) * o2 + per_lane(w3) * o3

    m_gla = _dot(yg_ref[...], wbg_ref[...])
    m_att = _dot(y_att.astype(BF16), wba_ref[...])
    gg = jnp.concatenate([gg0_ref[...], gg1_ref[...]], axis=1).astype(F32)
    ga = jnp.concatenate([ga0_ref[...], ga1_ref[...]], axis=1).astype(F32)
    merged = (0.5 + 0.5 * jnp.tanh(0.5 * gg)) * m_gla + (0.5 + 0.5 * jnp.tanh(0.5 * ga)) * m_att
    h = x_ref[...] + _dot(merged.astype(BF16), wmo_ref[...])
    h_ref[...] = h
    hn = h * lax.rsqrt(jnp.mean(h * h, axis=-1, keepdims=True) + EPS) * nf_ref[...]
    hn_ref[...] = hn.astype(BF16)

    hh, hl = _split2(hn)
    rh, rl = _split2(rw_ref[...])
    logit = _dot_nt(rh, hh) + _dot_nt(rh, hl) + _dot_nt(rl, hh) + rb_ref[...]
    tm = logit.shape[1]
    eid = lax.broadcasted_iota(jnp.int32, (ne, tm), 0)
    member = jnp.zeros((ne, tm), jnp.bool_)
    vals, idxs = [], []
    for _ in range(TOP_K):
        mx = jnp.max(logit, axis=0, keepdims=True)
        idx = jnp.min(jnp.where(logit == mx, eid, ne), axis=0, keepdims=True)
        sel = eid == idx
        member = jnp.logical_or(member, sel)
        logit = jnp.where(sel, -jnp.inf, logit)
        vals.append(mx)
        idxs.append(idx)
    ex = [jnp.exp(v - vals[0]) for v in vals]
    tot = ex[0] + ex[1] + ex[2] + ex[3]
    memf = jnp.where(member, 1.0, 0.0)
    lens = jnp.sum(memf, axis=1, keepdims=True)
    lens_b = jnp.broadcast_to(lens, (ne, LANES))
    off = _dot(ltri_ref[...], lens_b.astype(BF16))
    pos = _dot(memf.astype(BF16), triu_ref[...]) + off[:, 0:1]
    for k in range(TOP_K):
        tg_ref[0, k:k + 1, :] = ex[k] / tot
        lc_ref[0, k:k + 1, :] = jnp.sum(jnp.where(eid == idxs[k], pos, 0.0), axis=0, keepdims=True).astype(jnp.int32)
    carry = carry_ref[...]
    cum_ref[0] = carry.astype(jnp.int32)
    carry = carry + lens_b
    carry_ref[...] = carry
    cnt_ref[...] = carry.astype(jnp.int32)


def _merge(x2, y_gla, att, proj, layout, wbg, wba, wmo, norm_ffn_g, router_w, router_b, seq, tm):
    n, d = x2.shape
    hw = ATT_HEADS_PER_GROUP * ATT_HEAD_DIM
    half = d // 2
    assert n % tm == 0 and seq % tm == 0
    assert tm <= 256, "per-tile expert counts go through a bf16 matmul operand: exact up to 256"
    assert layout['gate_gla'] % half == 0 and layout['gate_att'] % half == 0
    assert all(tm % (o.shape[1] * 16) == 0 for o, _ in att) and att[0][0].shape[1] == 1
    nt = n // tm
    spt = seq // tm
    ggb = layout['gate_gla'] // half
    gab = layout['gate_att'] // half
    triu = jnp.asarray(np.triu(np.ones((tm, tm), np.float32), 1), BF16)
    row = lambda i: (i, 0)
    const = lambda i: (0, 0)

    def att_spec(a):
        dil = a.shape[1]
        return pl.BlockSpec((1, dil, tm // dil, a.shape[3]), lambda i: (i // spt, 0, i % spt, 0))

    (o1, l1), (o2, l2), (o3, l3) = att
    in_specs = [
        pl.BlockSpec((tm, d), row),
        pl.BlockSpec((tm, y_gla.shape[1]), row),
        att_spec(o1), att_spec(o2), att_spec(o3), att_spec(l1), att_spec(l2), att_spec(l3),
        pl.BlockSpec((tm, half), lambda i: (i, ggb)), pl.BlockSpec((tm, half), lambda i: (i, ggb + 1)),
        pl.BlockSpec((tm, half), lambda i: (i, gab)), pl.BlockSpec((tm, half), lambda i: (i, gab + 1)),
        pl.BlockSpec(wbg.shape, const), pl.BlockSpec(wba.shape, const), pl.BlockSpec(wmo.shape, const),
        pl.BlockSpec((1, d), const),
        pl.BlockSpec((N_EXPERTS, d), const),
        pl.BlockSpec((N_EXPERTS, 1), const),
        pl.BlockSpec((tm, tm), const),
        pl.BlockSpec((N_EXPERTS, N_EXPERTS), const),
        pl.BlockSpec((LANES, hw), const),
    ]
    out_specs = [
        pl.BlockSpec((tm, d), row),
        pl.BlockSpec((tm, d), row),
        pl.BlockSpec((1, TOP_K, tm), lambda i: (i, 0, 0)),
        pl.BlockSpec((1, TOP_K, tm), lambda i: (i, 0, 0)),
        pl.BlockSpec((1, N_EXPERTS, LANES), lambda i: (i, 0, 0)),
        pl.BlockSpec((N_EXPERTS, LANES), const),
    ]
    out_shape = [
        jax.ShapeDtypeStruct((n, d), F32),
        jax.ShapeDtypeStruct((n, d), BF16),
        jax.ShapeDtypeStruct((nt, TOP_K, tm), F32),
        jax.ShapeDtypeStruct((nt, TOP_K, tm), jnp.int32),
        jax.ShapeDtypeStruct((nt, N_EXPERTS, LANES), jnp.int32),
        jax.ShapeDtypeStruct((N_EXPERTS, LANES), jnp.int32),
    ]
    ltri = jnp.asarray(np.tril(np.ones((N_EXPERTS, N_EXPERTS), np.float32), -1), BF16)
    spread = jnp.asarray(np.arange(LANES)[:, None] == (np.arange(hw)[None, :] // ATT_HEAD_DIM) * LSE_REP, BF16)
    return pl.pallas_call(
        _merge_kernel,
        grid=(nt,),
        in_specs=in_specs,
        out_specs=out_specs,
        out_shape=out_shape,
        scratch_shapes=[pltpu.VMEM((N_EXPERTS, LANES), F32), pltpu.VMEM((2, hw // LANES, tm, LANES), F32),
                        pltpu.VMEM((2, 1, tm, LANES), F32)],
        compiler_params=_cparams(("arbitrary",)),
        name="merge_router",
    )(x2, y_gla, o1, o2, o3, l1, l2, l3, proj, proj, proj, proj, wbg, wba, wmo,
      norm_ffn_g.reshape(1, d), router_w.T, router_b.reshape(N_EXPERTS, 1), triu, ltri, spread)


LIN_SUB = 8


def _lin_pack(val):
    return [val[:, c * LANES:(c + 1) * LANES] for c in range(LIN_SUB)]


def _lin_unpack(slabs):
    return jnp.concatenate(slabs, axis=1)


def _dispatch_kernel(off_ref, len_ref, dst_ref, zdst_ref, zlen_ref, nact_ref, hn_ref, loc_ref, x_hbm,
                     xs_ref, zero_ref, sem, zsem, *, rloc):
    i = pl.program_id(0)
    nt = pl.num_programs(0)
    slot = i % 2
    ne = N_EXPERTS

    def drain(s):
        rows = rloc * LIN_SUB
        pltpu.make_async_copy(xs_ref.at[s], x_hbm.at[pl.ds(0, rows)], sem.at[s]).wait()

    @pl.when(i == 0)
    def _():
        zero_ref[...] = jnp.zeros_like(zero_ref)

        def fill_copy(e):
            rows = zlen_ref[e] * LIN_SUB
            dst = pl.multiple_of(zdst_ref[e] * LIN_SUB, 8)
            return rows, pltpu.make_async_copy(zero_ref.at[pl.ds(0, rows)], x_hbm.at[pl.ds(dst, rows)], zsem)

        def fill(e, carry):
            rows, copy = fill_copy(e)
            pl.when(rows > 0)(copy.start)
            return carry

        def fill_done(e, carry):
            rows, copy = fill_copy(e)
            pl.when(rows > 0)(copy.wait)
            return carry

        def block_copy(b):
            rows = zero_ref.shape[0]
            return pltpu.make_async_copy(zero_ref, x_hbm.at[pl.ds(pl.multiple_of(b * rows, rows), rows)], zsem)

        def fill_block(b, carry):
            block_copy(b).start()
            return carry

        def fill_block_done(b, carry):
            block_copy(b).wait()
            return carry

        nblk = x_hbm.shape[0] // zero_ref.shape[0]
        lax.fori_loop(0, ne, fill, 0)
        lax.fori_loop(nact_ref[0], nblk, fill_block, 0)
        lax.fori_loop(0, ne, fill_done, 0)
        lax.fori_loop(nact_ref[0], nblk, fill_block_done, 0)

    @pl.when(i >= 2)
    def _():
        drain(slot)

    loc = loc_ref[0]
    tt = loc.shape[1]
    row_id = lax.broadcasted_iota(jnp.int32, (rloc, tt), 0)
    hit = row_id == loc[0:1, :]
    for k in range(1, TOP_K):
        hit = jnp.logical_or(hit, row_id == loc[k:k + 1, :])
    onehot = jnp.where(hit, 1.0, 0.0).astype(BF16)
    xs = _dot(onehot, hn_ref[...])
    for c, slab in enumerate(_lin_pack(xs)):
        xs_ref[slot, pl.ds(c, rloc, stride=LIN_SUB), :] = slab

    def send(e, carry):
        rows = len_ref[i * ne + e] * LIN_SUB

        @pl.when(rows > 0)
        def _():
            src = pl.multiple_of(off_ref[i * ne + e] * LIN_SUB, 8)
            dst = pl.multiple_of(dst_ref[i * ne + e] * LIN_SUB, 8)
            pltpu.make_async_copy(xs_ref.at[slot, pl.ds(src, rows)], x_hbm.at[pl.ds(dst, rows)], sem.at[slot]).start()
        return carry

    lax.fori_loop(0, ne, send, 0)

    @pl.when(i == nt - 1)
    def _():
        @pl.when(i >= 1)
        def _():
            drain(1 - slot)
        drain(slot)


def _dispatch(hn, loc, tabs, nact, p_rows, rloc):
    n, d = hn.shape
    nt, _, tt = loc.shape
    assert d == LIN_SUB * LANES and p_rows % MOE_BLOCK_ROWS == 0
    kern = functools.partial(_dispatch_kernel, rloc=rloc)
    grid_spec = pltpu.PrefetchScalarGridSpec(
        num_scalar_prefetch=6,
        grid=(nt,),
        in_specs=[
            pl.BlockSpec((tt, d), lambda i, *_: (i, 0)),
            pl.BlockSpec((1, TOP_K, tt), lambda i, *_: (i, 0, 0)),
        ],
        out_specs=pl.BlockSpec(memory_space=pl.ANY),
        scratch_shapes=[
            pltpu.VMEM((2, rloc * LIN_SUB, LANES), F32),
            pltpu.VMEM((MOE_BLOCK_ROWS * LIN_SUB, LANES), F32),
            pltpu.SemaphoreType.DMA((2,)),
            pltpu.SemaphoreType.DMA(()),
        ],
    )
    return pl.pallas_call(
        kern,
        grid_spec=grid_spec,
        out_shape=jax.ShapeDtypeStruct((p_rows * LIN_SUB, LANES), F32),
        compiler_params=_cparams(("arbitrary",)),
        name="moe_dispatch",
    )(tabs['off'], tabs['len'], tabs['dst'], tabs['zdst'], tabs['zlen'], nact, hn, loc)


def _expert_kernel(bexp_ref, nact_ref, first_ref, next_ref, x_ref, wgu_hbm, wd_hbm, bgu_ref, bd_ref, perm_ref,
                   y_ref, wgu_f, wd_f, wgu_s, wd_s, sem, *, bm):
    grp = MXU_DIM
    ff2 = wgu_s.shape[1]
    halfg = grp // 2
    brows = bm * LIN_SUB
    bps = x_ref.shape[0] // brows

    def fetch(ex):
        return (pltpu.make_async_copy(wgu_hbm.at[ex], wgu_f, sem.at[0]),
                pltpu.make_async_copy(wd_hbm.at[ex], wd_f, sem.at[1]))

    @pl.when(pl.program_id(0) == 0)
    def _():
        for copy in fetch(bexp_ref[0]):
            copy.start()

    def block(b, r0):
        e = bexp_ref[b]
        active = b < nact_ref[0]

        @pl.when(jnp.logical_and(active, first_ref[b] == 1))
        def _():
            for copy in fetch(e):
                copy.wait()
            perm = perm_ref[...]
            for g in range(ff2 // grp):
                cols = slice(g * grp, (g + 1) * grp)
                wgu_s[:, cols] = _dot(wgu_f[:, cols].astype(BF16), perm).astype(BF16)
            wd_s[...] = wd_f[...].astype(BF16)
            nxt = next_ref[b]

            @pl.when(nxt >= 0)
            def _():
                for copy in fetch(nxt):
                    copy.start()

        @pl.when(active)
        def _():
            x = _lin_unpack([x_ref[pl.ds(r0 + c, bm, stride=LIN_SUB), :] for c in range(LIN_SUB)]).astype(BF16)
            acts = []
            for g in range(ff2 // grp):
                cols = slice(g * grp, (g + 1) * grp)
                hg = _dot(x, wgu_s[:, cols]) + bgu_ref[e, :, cols]
                gate = jnp.minimum(hg[:, :halfg], SWIGLU_LIMIT)
                up = jnp.clip(hg[:, halfg:], -SWIGLU_LIMIT, SWIGLU_LIMIT)
                acts.append(((up + 1.0) * (gate * jax.nn.sigmoid(SWIGLU_ALPHA * gate))).astype(BF16))
            act = jnp.concatenate(acts, axis=1)
            y = _dot(act, wd_s[...]) + bd_ref[e]
            for c, slab in enumerate(_lin_pack(y)):
                y_ref[pl.ds(r0 + c, bm, stride=LIN_SUB), :] = slab

        @pl.when(jnp.logical_not(active))
        def _():
            y_ref[pl.ds(r0, brows), :] = jnp.zeros((brows, y_ref.shape[1]), y_ref.dtype)

    for sub in range(bps):
        block(pl.program_id(0) * bps + sub, sub * brows)


def _gate_up_permutation():
    grp = MXU_DIM
    halfg = grp // 2
    out = np.arange(grp)
    src = np.where(out < halfg, 2 * out, 2 * (out - halfg) + 1)
    p = np.zeros((grp, grp), np.float32)
    p[src, out] = 1.0
    return jnp.asarray(p, BF16)


def _experts(x_lin, bexp, nact, first, nxt, w_gate_up, b_gate_up, w_down, b_down, bm):
    ne, d, ff2 = w_gate_up.shape
    ff = ff2 // 2
    grp = MXU_DIM
    assert ff2 % grp == 0 and d == LIN_SUB * LANES
    nblk = x_lin.shape[0] // (bm * LIN_SUB)
    bps = 2 if nblk % 2 == 0 else 1
    srows = bps * bm * LIN_SUB
    bgu = b_gate_up.reshape(ne, ff2 // grp, grp // 2, 2).transpose(0, 1, 3, 2).reshape(ne, 1, ff2)
    grid_spec = pltpu.PrefetchScalarGridSpec(
        num_scalar_prefetch=4,
        grid=(nblk // bps,),
        in_specs=[
            pl.BlockSpec((srows, LANES), lambda i, *_: (i, 0)),
            pl.BlockSpec(memory_space=pl.ANY),
            pl.BlockSpec(memory_space=pl.ANY),
            pl.BlockSpec(memory_space=pltpu.VMEM),
            pl.BlockSpec(memory_space=pltpu.VMEM),
            pl.BlockSpec((grp, grp), lambda i, *_: (0, 0)),
        ],
        out_specs=pl.BlockSpec((srows, LANES), lambda i, *_: (i, 0)),
        scratch_shapes=[pltpu.VMEM((d, ff2), F32), pltpu.VMEM((ff, d), F32),
                        pltpu.VMEM((d, ff2), BF16), pltpu.VMEM((ff, d), BF16),
                        pltpu.SemaphoreType.DMA((2,))],
    )
    return pl.pallas_call(
        functools.partial(_expert_kernel, bm=bm),
        grid_spec=grid_spec,
        out_shape=jax.ShapeDtypeStruct(x_lin.shape, F32),
        compiler_params=_cparams(("arbitrary",)),
        name="moe_experts",
    )(bexp, nact, first, nxt, x_lin, w_gate_up, w_down, bgu, b_down.reshape(ne, 1, d), _gate_up_permutation())


def _combine_kernel(off_ref, len_ref, dst_ref, h_ref, loc_ref, gate_ref, g_ref, y_hbm, out_ref,
                    ybuf, sem, *, rloc):
    i = pl.program_id(0)
    nt = pl.num_programs(0)
    slot = i % 2
    ne = N_EXPERTS

    def fetch(step, s):
        def run(e, carry):
            rows = len_ref[step * ne + e] * LIN_SUB

            @pl.when(rows > 0)
            def _():
                src = pl.multiple_of(dst_ref[step * ne + e] * LIN_SUB, 8)
                dst = pl.multiple_of(off_ref[step * ne + e] * LIN_SUB, 8)
                pltpu.make_async_copy(y_hbm.at[pl.ds(src, rows)], ybuf.at[s, pl.ds(dst, rows)], sem.at[s]).start()
            return carry

        lax.fori_loop(0, ne, run, 0)

    @pl.when(i == 0)
    def _():
        fetch(0, 0)

    @pl.when(i + 1 < nt)
    def _():
        fetch(i + 1, 1 - slot)

    pltpu.make_async_copy(y_hbm.at[pl.ds(0, rloc * LIN_SUB)], ybuf.at[slot], sem.at[slot]).wait()

    y = _lin_unpack([ybuf[slot, pl.ds(c, rloc, stride=LIN_SUB), :] for c in range(LIN_SUB)]).astype(BF16)
    loc = loc_ref[0]
    gate = gate_ref[0]
    tt = loc.shape[1]
    row_id = lax.broadcasted_iota(jnp.int32, (rloc, tt), 0)
    w = jnp.zeros((rloc, tt), F32)
    for k in range(TOP_K):
        w = w + jnp.where(row_id == loc[k:k + 1, :], gate[k:k + 1, :], 0.0)
    h = h_ref[...] + _dot_tn(w.astype(BF16), y)
    out_ref[...] = h * lax.rsqrt(jnp.mean(h * h, axis=-1, keepdims=True) + EPS) * g_ref[...]


def _combine(h, y_lin, loc, gates, tabs, norm_g, rloc):
    n, d = h.shape
    nt, _, tt = loc.shape
    kern = functools.partial(_combine_kernel, rloc=rloc)
    grid_spec = pltpu.PrefetchScalarGridSpec(
        num_scalar_prefetch=3,
        grid=(nt,),
        in_specs=[
            pl.BlockSpec((tt, d), lambda i, *_: (i, 0)),
            pl.BlockSpec((1, TOP_K, tt), lambda i, *_: (i, 0, 0)),
            pl.BlockSpec((1, TOP_K, tt), lambda i, *_: (i, 0, 0)),
            pl.BlockSpec((1, d), lambda i, *_: (0, 0)),
            pl.BlockSpec(memory_space=pl.ANY),
        ],
        out_specs=pl.BlockSpec((tt, d), lambda i, *_: (i, 0)),
        scratch_shapes=[
            pltpu.VMEM((2, rloc * LIN_SUB, LANES), F32),
            pltpu.SemaphoreType.DMA((2,)),
        ],
    )
    return pl.pallas_call(
        kern,
        grid_spec=grid_spec,
        out_shape=jax.ShapeDtypeStruct((n, d), F32),
        compiler_params=_cparams(("arbitrary",)),
        name="moe_combine",
    )(tabs['off'], tabs['len'], tabs['dst'], h, loc, gates, norm_g.reshape(1, d), y_lin)


def _routing_tables(cum, counts, bm, nblk):
    padded = (counts + bm - 1) // bm * bm
    pend = jnp.cumsum(padded)
    pstart = pend - padded
    blk = jnp.arange(nblk, dtype=jnp.int32)
    bexp_raw = jnp.sum((pend // bm)[None, :] <= blk[:, None], axis=1).astype(jnp.int32)
    nact = (pend[-1] // bm).astype(jnp.int32)
    last = jnp.minimum(bexp_raw[jnp.maximum(nact - 1, 0)], N_EXPERTS - 1)
    bexp = jnp.where(blk < nact, jnp.minimum(bexp_raw, N_EXPERTS - 1), last)
    prev = jnp.concatenate([jnp.full((1,), -1, jnp.int32), bexp[:-1]])
    first = jnp.logical_and(bexp != prev, blk < nact)
    later = jnp.logical_and(first[None, :], blk[None, :] > blk[:, None])
    nxt_blk = jnp.min(jnp.where(later, blk[None, :], nblk), axis=1)
    nxt = jnp.where(nxt_blk < nblk, bexp[jnp.minimum(nxt_blk, nblk - 1)], -1)
    cum_full = jnp.concatenate([cum, counts[None, :]], axis=0)
    run_len = cum_full[1:] - cum_full[:-1]
    run_off = jnp.cumsum(run_len, axis=1) - run_len
    tabs = {
        'off': run_off.reshape(-1), 'len': run_len.reshape(-1),
        'dst': (pstart[None, :] + cum_full[:-1]).reshape(-1),
        'zdst': pstart + counts, 'zlen': padded - counts,
    }
    blocks = (bexp, nact.reshape(1), first.astype(jnp.int32), nxt.astype(jnp.int32))
    return {k: v.astype(jnp.int32) for k, v in tabs.items()}, blocks


MOE_BLOCK_ROWS = 256
TOKEN_TILE = 256


def kernel(x, norm_mix_g, w_in, gla_gate_up, gla_gate_bias, gla_norm_g, w_branch_gla, w_branch_att, w_mix_out,
           norm_ffn_g, router_w, router_b, expert_w_gate_up, expert_b_gate_up, expert_w_down, expert_b_down,
           norm_final_g):
    batch, seq, d = x.shape
    assert norm_mix_g.shape[0] == 1, "single layer"
    n = batch * seq
    gla_k = gla_gate_up.shape[2]
    gla_v = w_branch_gla.shape[1]
    att_w = len(ATT_GROUPS) * ATT_HEADS_PER_GROUP * ATT_HEAD_DIM
    hw = ATT_HEADS_PER_GROUP * ATT_HEAD_DIM
    ngrp = len(ATT_GROUPS)
    assert att_w == ngrp * hw and all(s % hw == 0 for s in (gla_k, gla_v, d))
    assert ATT_GROUPS[0][1] == 1 and all(win // dil == ATT_BLOCK for win, dil in ATT_GROUPS)
    src, off = {}, 0
    for name, size in (('gq', gla_k), ('gk', gla_k), ('gv', gla_v), ('gr', gla_v), ('glr', GLA_GATE_RANK),
                       ('aq', att_w), ('ak', att_w), ('av', att_w), ('gate_gla', d), ('gate_att', d)):
        src[name] = (off, size)
        off += size
    assert off == w_in.shape[2]
    w = w_in[0]
    glr_at, glr_n = src['glr']
    w_main = jnp.concatenate([w[:, :glr_at], w[:, glr_at + glr_n:]], axis=1).astype(BF16)
    w_glr = w[:, glr_at:glr_at + glr_n]

    def col(name, group=0):
        start, _ = src[name]
        return (start if start < glr_at else start - glr_n) + group * hw

    plain = (('gq', gla_k), ('gk', gla_k), ('gv', gla_v), ('gr', gla_v), ('gate_gla', d), ('gate_att', d),
             ('aq', hw), ('ak', hw), ('av', hw))
    layout, tile_cols = {}, []
    for name, size in plain:
        layout[name] = len(tile_cols) * hw
        tile_cols += [col(name) + t * hw for t in range(size // hw)]
    n_plain = len(tile_cols)
    tile_cols += [col(kind, g) for g in range(1, ngrp) for kind in ('aq', 'ak', 'av')]
    dilations = [dil for _, dil in ATT_GROUPS[1:]]

    x2 = x.reshape(n, d)
    proj, glr, *qkv_dil = _in_projection(x2, norm_mix_g[0], w_main, w_glr, batch, seq, n_plain, dilations,
                                         tuple(tile_cols))
    y_gla = _gla(proj, glr, gla_gate_up[0], gla_gate_bias[0], gla_norm_g[0], batch, seq, layout)
    att = [_dilated_attention(proj.reshape(batch, 1, seq, proj.shape[1]), 1, layout['aq'])]
    att += [_dilated_attention(a, dil, 0) for a, dil in zip(qkv_dil, dilations)]
    tt = min(TOKEN_TILE, n)
    h, hn, gates, loc, cum, cnt = _merge(
        x2, y_gla, att, proj, layout, w_branch_gla[0].astype(BF16), w_branch_att[0].astype(BF16),
        w_mix_out[0].astype(BF16), norm_ffn_g[0], router_w[0], router_b[0], seq, tt)

    bm = MOE_BLOCK_ROWS
    rloc = TOP_K * tt
    nblk = -(-(TOP_K * n) // bm) + N_EXPERTS
    tabs, blocks = _routing_tables(cum[:, :, 0], cnt[:, 0], bm, nblk)
    x_lin = _dispatch(hn, loc, tabs, blocks[1], nblk * bm, rloc)
    y_lin = _experts(x_lin, *blocks, expert_w_gate_up[0], expert_b_gate_up[0], expert_w_down[0],
                     expert_b_down[0], bm)
    out = _combine(h, y_lin, loc, gates, tabs, norm_final_g, rloc)
    return out.reshape(batch, seq, d)
```

```python
import functools
import math

import jax
import jax.numpy as jnp
import numpy as np
from jax import lax
from jax.experimental import pallas as pl
from jax.experimental.pallas import tpu as pltpu

GLA_HEADS = 4
GLA_GATE_RANK = 16
GLA_TAU = 16.0
GLA_CHUNK = 64
ATT_GROUPS = ((128, 1), (512, 4), (2048, 16))
ATT_HEADS_PER_GROUP = 8
ATT_HEAD_DIM = 64
ATT_BLOCK = 128
ROT_DIM = ATT_HEAD_DIM // 4
ROPE_THETA = 500000.0
N_EXPERTS = 32
TOP_K = 4
SWIGLU_LIMIT = 7.0
SWIGLU_ALPHA = 1.702
EPS = 1e-5

LANES = 128
MXU_DIM = 256
VMEM_LIMIT = 56 * 1024 * 1024

F32 = jnp.float32
BF16 = jnp.bfloat16


def _cparams(sem):
    return pltpu.CompilerParams(dimension_semantics=sem, vmem_limit_bytes=VMEM_LIMIT)


def _dot(a, b):
    return jnp.dot(a, b, preferred_element_type=F32)


def _dot_nt(a, b):
    return lax.dot_general(a, b, (((1,), (1,)), ((), ())), preferred_element_type=F32)


def _dot_tn(a, b):
    return lax.dot_general(a, b, (((0,), (0,)), ((), ())), preferred_element_type=F32)


def _split2(a):
    hi = a.astype(BF16)
    lo = (a - hi.astype(F32)).astype(BF16)
    return hi, lo


def _inproj_kernel(x_ref, g_ref, w_hbm, wtail_hbm, wglr_ref, cos_ref, sa_ref, sb_ref, proj_ref, glr_ref, *rest,
                   n_plain, dilations, q_scale, tn, tile_src):
    dil_refs, (xn_ref, stage_ref, w_ref, wst_ref, wsem) = rest[:len(dilations)], rest[len(dilations):]
    nslab = tn // LANES

    @pl.when(pl.program_id(0) == 0)
    def _():
        def window(j):
            from_tail, start, width, _ = tile_src[j]
            src = wtail_hbm if from_tail else w_hbm
            return pltpu.make_async_copy(src.at[:, pl.ds(start, width)], wst_ref.at[j % 2, :, pl.ds(0, width)],
                                         wsem.at[j % 2])

        window(0).start()
        for j in range(len(tile_src)):
            if j + 1 < len(tile_src):
                window(j + 1).start()
            window(j).wait()
            shift = tile_src[j][3]
            lane = lax.broadcasted_iota(jnp.int32, (w_ref.shape[0], LANES), 1)
            for s in range(nslab):
                cur = wst_ref[j % 2, :, s * LANES:(s + 1) * LANES]
                if shift:
                    nxt = wst_ref[j % 2, :, (s + 1) * LANES:(s + 2) * LANES]
                    cur = jnp.where(lane < LANES - shift, pltpu.roll(cur, LANES - shift, 1),
                                    pltpu.roll(nxt, LANES - shift, 1))
                w_ref[:, j * tn + s * LANES:j * tn + (s + 1) * LANES] = cur.astype(BF16)

    x = x_ref[...]
    y = x * lax.rsqrt(jnp.mean(x * x, axis=-1, keepdims=True) + EPS) * g_ref[...]
    xn_ref[...] = y.astype(BF16)
    yh, yl = _split2(y)
    wh, wl = _split2(wglr_ref[...])
    glr_ref[...] = _dot(yh, wh) + _dot(yh, wl) + _dot(yl, wh)

    tm = xn_ref.shape[0]
    half = ROT_DIM // 2

    def slabs(j, kind):
        acc = _dot(xn_ref[...], w_ref[:, j * tn:(j + 1) * tn])
        for s in range(nslab):
            t = acc[:, s * LANES:(s + 1) * LANES]
            if kind < 2:
                t = t * cos_ref[...] + pltpu.roll(t, LANES - half, 1) * sa_ref[...] \
                    + pltpu.roll(t, half, 1) * sb_ref[...]
                if kind == 0:
                    t = t * q_scale
            yield s, t

    for gi, dil in enumerate(dilations):
        for kind in range(3):
            j = n_plain + 3 * gi + kind
            buf = (3 * gi + kind) % stage_ref.shape[0]
            for s, t in slabs(j, kind):
                stage_ref[buf, s] = t
            for r in range(dil):
                for s in range(nslab):
                    dil_refs[gi][0, r, :, kind * tn + s * LANES:kind * tn + (s + 1) * LANES] = \
                        stage_ref[buf, s, pl.ds(r, tm // dil, stride=dil), :].astype(BF16)

    q0 = n_plain - 3
    for j in range(n_plain):
        for s, t in slabs(j, j - q0 if j >= q0 else 2):
            proj_ref[:, j * tn + s * LANES:j * tn + (s + 1) * LANES] = t.astype(BF16)


def _rope_tables(seq):
    half = ROT_DIM // 2
    inv_freq = ROPE_THETA ** (-np.arange(half, dtype=np.float32) * np.float32(2.0 / ROT_DIM))
    pos = jnp.arange(seq, dtype=F32)
    ang = pos[:, None] * jnp.asarray(inv_freq, F32)[None, :]
    cos = jnp.cos(ang)
    sin = jnp.sin(ang)
    lane = np.arange(LANES) % ATT_HEAD_DIM
    idx = np.where(lane < ROT_DIM, lane % half, 0)
    in_rot = jnp.asarray(lane < ROT_DIM)
    first = jnp.asarray(lane < half)
    second = jnp.asarray((lane >= half) & (lane < ROT_DIM))
    cos_t = jnp.where(in_rot[None, :], cos[:, idx], 1.0)
    sin_g = sin[:, idx]
    sa = jnp.where(first[None, :], -sin_g, 0.0)
    sb = jnp.where(second[None, :], sin_g, 0.0)
    return cos_t.astype(F32), sa.astype(F32), sb.astype(F32)


def _in_projection(x2, norm_g, w, w_glr, batch, seq, n_plain, dilations, tile_cols):
    n, d = x2.shape
    tn = ATT_HEADS_PER_GROUP * ATT_HEAD_DIM
    tm = min(512, seq)
    width = w.shape[1]
    ntile = len(tile_cols)
    assert n % tm == 0 and seq % tm == 0 and ntile == n_plain + 3 * len(dilations)
    assert all(tm % (dil * 16) == 0 for dil in dilations)
    tail_w = tn + LANES
    tail0 = width - tail_w
    w_tail = w[:, tail0:]
    tile_src = []
    for c in tile_cols:
        shift = c % LANES
        if shift == 0:
            tile_src.append((False, c, tn, 0))
        elif c - shift + tail_w <= width:
            tile_src.append((False, c - shift, tail_w, shift))
        else:
            assert (c - tail0) % LANES == 0 and c - tail0 + tn <= tail_w
            tile_src.append((True, c - tail0, tn, 0))
    cos_t, sa, sb = _rope_tables(seq)
    spt = seq // tm
    kern = functools.partial(_inproj_kernel, n_plain=n_plain, dilations=tuple(dilations),
                             q_scale=ATT_HEAD_DIM ** -0.5, tn=tn, tile_src=tuple(tile_src))
    out_specs = [
        pl.BlockSpec((tm, n_plain * tn), lambda i: (i, 0)),
        pl.BlockSpec((tm, GLA_GATE_RANK), lambda i: (i, 0)),
    ]
    out_shape = [
        jax.ShapeDtypeStruct((n, n_plain * tn), BF16),
        jax.ShapeDtypeStruct((n, GLA_GATE_RANK), F32),
    ]
    for dil in dilations:
        out_specs.append(pl.BlockSpec((1, dil, tm // dil, 3 * tn), lambda i: (i // spt, 0, i % spt, 0)))
        out_shape.append(jax.ShapeDtypeStruct((batch, dil, seq // dil, 3 * tn), BF16))
    const = lambda i: (0, 0)
    return pl.pallas_call(
        kern,
        grid=(n // tm,),
        in_specs=[
            pl.BlockSpec((tm, d), lambda i: (i, 0)),
            pl.BlockSpec((1, d), const),
            pl.BlockSpec(memory_space=pl.ANY),
            pl.BlockSpec(memory_space=pl.ANY),
            pl.BlockSpec((d, GLA_GATE_RANK), const),
            pl.BlockSpec((tm, LANES), lambda i: (i % spt, 0)),
            pl.BlockSpec((tm, LANES), lambda i: (i % spt, 0)),
            pl.BlockSpec((tm, LANES), lambda i: (i % spt, 0)),
        ],
        out_specs=out_specs,
        out_shape=out_shape,
        scratch_shapes=[pltpu.VMEM((tm, d), BF16), pltpu.VMEM((2, tn // LANES, tm, LANES), F32),
                        pltpu.VMEM((d, ntile * tn), BF16),
                        pltpu.VMEM((2, d, tail_w), F32), pltpu.SemaphoreType.DMA((2,))],
        compiler_params=_cparams(("arbitrary",)),
        name="in_projection",
    )(x2, norm_g.reshape(1, d), w, w_tail, w_glr, cos_t, sa, sb)


def _gla_kernel(q_ref, k_ref, v_ref, r_ref, glr_ref, up_ref, bias_ref, ng_ref, tri_ref, y_ref, state_ref,
                *, dk, dv, ts):
    t = pl.program_id(2)

    @pl.when(t == 0)
    def _():
        state_ref[...] = jnp.zeros_like(state_ref)

    c = GLA_CHUNK
    hps = state_ref.shape[0]
    gh, gl = _split2(glr_ref[...])
    uh, ul = _split2(up_ref[...])
    z = _dot(gh, uh) + _dot(gh, ul) + _dot(gl, uh) + bias_ref[...]
    logdec = (jnp.minimum(z, 0.0) - jnp.log1p(jnp.exp(-jnp.abs(z)))) * (1.0 / GLA_TAU)
    ldh, ldl = _split2(logdec)
    tri = tri_ref[...]
    row = lax.broadcasted_iota(jnp.int32, (c, c), 0)
    col = lax.broadcasted_iota(jnp.int32, (c, c), 1)
    causal = col <= row
    scale = dk ** -0.5
    ng = ng_ref[...]
    states = [state_ref[hh] for hh in range(hps)]
    for ci in range(ts // c):
        sl = slice(ci * c, (ci + 1) * c)
        b2 = _dot(tri, ldh[sl]) + _dot(tri, ldl[sl])
        for hh in range(hps):
            ks = slice(hh * dk, (hh + 1) * dk)
            vs = slice(hh * dv, (hh + 1) * dv)
            b = b2[:, ks]
            bl = b[c - 1:c, :]
            eb = jnp.exp(b)
            enb = jnp.exp(-b)
            ebl = jnp.exp(bl)
            qf = q_ref[sl, ks].astype(F32)
            kf = k_ref[sl, ks].astype(F32)
            qe = (qf * scale * eb).astype(BF16)
            ke = (kf * enb).astype(BF16)
            kd = (kf * enb * ebl).astype(BF16)
            vb = v_ref[sl, vs]
            a = jnp.where(causal, _dot_nt(qe, ke), 0.0).astype(BF16)
            o = _dot(a, vb) + _dot(qe, states[hh].astype(BF16))
            dec = jnp.transpose(jnp.broadcast_to(ebl, (dk, dk)))
            dec_full = jnp.concatenate([dec] * (dv // dk), axis=1)
            states[hh] = dec_full * states[hh] + _dot_tn(kd, vb)
            o = o * lax.rsqrt(jnp.mean(o * o, axis=-1, keepdims=True) + EPS) * ng
            rf = r_ref[sl, vs].astype(F32)
            o = o * (rf * jax.nn.sigmoid(rf))
            y_ref[sl, vs] = o.astype(BF16)
    for hh in range(hps):
        state_ref[hh] = states[hh]


def _gla(proj, glr, gate_up, gate_bias, norm_g, batch, seq, layout):
    n = proj.shape[0]
    dk = (layout['gk'] - layout['gq']) // GLA_HEADS
    dv = (layout['gr'] - layout['gv']) // GLA_HEADS
    ts = min(512, seq)
    assert dk == LANES and dv % dk == 0 and seq % ts == 0 and ts % GLA_CHUNK == 0
    spt = seq // ts
    hps = GLA_HEADS
    wk, wv = hps * dk, hps * dv
    assert GLA_HEADS % hps == 0 and all(layout[s] % wk == 0 for s in ('gq', 'gk'))
    assert all(layout[s] % wv == 0 for s in ('gv', 'gr'))
    qb, kb = layout['gq'] // wk, layout['gk'] // wk
    vb, rb = layout['gv'] // wv, layout['gr'] // wv
    tri = jnp.asarray(np.tril(np.ones((GLA_CHUNK, GLA_CHUNK), np.float32)), BF16)
    kern = functools.partial(_gla_kernel, dk=dk, dv=dv, ts=ts)
    row = lambda b, h, t: b * spt + t
    return pl.pallas_call(
        kern,
        grid=(batch, GLA_HEADS // hps, spt),
        in_specs=[
            pl.BlockSpec((ts, wk), lambda b, h, t: (row(b, h, t), qb + h)),
            pl.BlockSpec((ts, wk), lambda b, h, t: (row(b, h, t), kb + h)),
            pl.BlockSpec((ts, wv), lambda b, h, t: (row(b, h, t), vb + h)),
            pl.BlockSpec((ts, wv), lambda b, h, t: (row(b, h, t), rb + h)),
            pl.BlockSpec((ts, GLA_GATE_RANK), lambda b, h, t: (row(b, h, t), 0)),
            pl.BlockSpec((GLA_GATE_RANK, wk), lambda b, h, t: (0, h)),
            pl.BlockSpec((1, wk), lambda b, h, t: (0, h)),
            pl.BlockSpec((1, dv), lambda b, h, t: (0, 0)),
            pl.BlockSpec((GLA_CHUNK, GLA_CHUNK), lambda b, h, t: (0, 0)),
        ],
        out_specs=pl.BlockSpec((ts, wv), lambda b, h, t: (row(b, h, t), h)),
        out_shape=jax.ShapeDtypeStruct((n, GLA_HEADS * dv), BF16),
        scratch_shapes=[pltpu.VMEM((hps, dk, dv), F32)],
        compiler_params=_cparams(("arbitrary", "arbitrary", "arbitrary")),
        name="gla",
    )(proj, proj, proj, proj, glr, gate_up, gate_bias.reshape(1, -1), norm_g.reshape(1, -1), tri)


LSE_REP = LANES // ATT_HEADS_PER_GROUP


def _att_kernel(q_ref, k_ref, v_ref, o_ref, lse_ref, *, nb, pairs, unroll):
    blk = ATT_BLOCK
    hd = ATT_HEAD_DIM
    row = lax.broadcasted_iota(jnp.int32, (2 * blk, 2 * blk), 0) % blk
    col = lax.broadcasted_iota(jnp.int32, (2 * blk, 2 * blk), 1)
    dist = row + blk - col
    band = jnp.logical_and(dist >= 0, dist <= blk)
    causal = band[:, blk:]
    lane = lax.broadcasted_iota(jnp.int32, (blk, LANES), 1)
    head0 = lane < hd
    lane_head = lane // LSE_REP
    neg = -jnp.inf

    def block(j, first):
        r0 = pl.multiple_of(j * blk, blk)
        lse = jnp.zeros((blk, LANES), F32)
        for p in range(pairs):
            cs = slice(p * LANES, (p + 1) * LANES)
            q2 = q_ref[0, 0, pl.ds(r0, blk), cs]
            zero = jnp.zeros_like(q2)
            qs = jnp.concatenate([jnp.where(head0, q2, zero), jnp.where(head0, zero, q2)], axis=0)
            if first:
                kk = k_ref[0, 0, pl.ds(r0, blk), cs]
                vv = v_ref[0, 0, pl.ds(r0, blk), cs]
                s = jnp.where(causal, _dot_nt(qs, kk), neg)
            else:
                k0 = pl.multiple_of(r0 - blk, blk)
                kk = k_ref[0, 0, pl.ds(k0, 2 * blk), cs]
                vv = v_ref[0, 0, pl.ds(k0, 2 * blk), cs]
                s = jnp.where(band, _dot_nt(qs, kk), neg)
            mx = jnp.max(s, axis=-1, keepdims=True)
            e = jnp.exp(s - mx).astype(BF16)
            one = jnp.ones_like(vv)
            hv = lax.broadcasted_iota(jnp.int32, vv.shape, 1) < hd
            out0 = _dot(e[:blk], jnp.where(hv, vv, one))
            out1 = _dot(e[blk:], jnp.where(hv, one, vv))
            num = jnp.where(head0, out0, out1)
            den_x = jnp.where(head0, out1, out0)
            den = pltpu.roll(den_x, hd, 1)
            o_ref[0, 0, pl.ds(r0, blk), cs] = (num / den).astype(o_ref.dtype)
            lse0 = mx[:blk] + jnp.log(jnp.where(head0, den, den_x))
            lse1 = mx[blk:] + jnp.log(jnp.where(head0, den_x, den))
            lse = jnp.where(lane_head == 2 * p, lse0, jnp.where(lane_head == 2 * p + 1, lse1, lse))
        lse_ref[0, 0, pl.ds(r0, blk), :] = lse

    block(0, True)

    def body(j, carry):
        block(j, False)
        return carry

    lax.fori_loop(1, nb, body, 0, unroll=unroll)


def _dilated_attention(qkv, dilation, col0):
    batch, dil, sub, width = qkv.shape
    hw = ATT_HEADS_PER_GROUP * ATT_HEAD_DIM
    assert dil == dilation and sub % ATT_BLOCK == 0 and col0 % hw == 0
    nb = sub // ATT_BLOCK
    pairs = hw // LANES
    assert 2 * pairs * LSE_REP == LANES
    kern = functools.partial(_att_kernel, nb=nb, pairs=pairs, unroll=1)

    def in_spec(kind):
        base = (col0 + kind * hw) // hw
        return pl.BlockSpec((1, 1, sub, hw), lambda b, r: (b, r, 0, base))

    return pl.pallas_call(
        kern,
        grid=(batch, dil),
        in_specs=[in_spec(0), in_spec(1), in_spec(2)],
        out_specs=[pl.BlockSpec((1, 1, sub, hw), lambda b, r: (b, r, 0, 0)),
                   pl.BlockSpec((1, 1, sub, LANES), lambda b, r: (b, r, 0, 0))],
        out_shape=[
            jax.ShapeDtypeStruct((batch, dil, sub, hw), BF16),
            jax.ShapeDtypeStruct((batch, dil, sub, LANES), F32),
        ],
        compiler_params=_cparams(("arbitrary", "arbitrary")),
        name=f"dilated_attention_d{dilation}",
    )(qkv, qkv, qkv)


def _merge_kernel(x_ref, yg_ref, o1_ref, o2_ref, o3_ref, l1_ref, l2_ref, l3_ref,
                  gg0_ref, gg1_ref, ga0_ref, ga1_ref, wbg_ref, wba_ref, wmo_ref, nf_ref, rw_ref, rb_ref, triu_ref,
                  ltri_ref, spread_ref, h_ref, hn_ref, tg_ref, lc_ref, cum_ref, cnt_ref, carry_ref, os_ref, ls_ref):
    i = pl.program_id(0)
    ne = N_EXPERTS

    @pl.when(i == 0)
    def _():
        carry_ref[...] = jnp.zeros_like(carry_ref)

    def token_order(src_ref, stage_ref):
        dil, sub, w = src_ref.shape[1:]
        if dil == 1:
            return src_ref[0, 0].astype(F32)
        for r in range(dil):
            val = src_ref[0, r].astype(F32)
            for s in range(w // LANES):
                stage_ref[s, pl.ds(r, sub, stride=dil), :] = val[:, s * LANES:(s + 1) * LANES]
        return jnp.concatenate([stage_ref[s] for s in range(w // LANES)], axis=1)

    l1, o1 = token_order(l1_ref, None), token_order(o1_ref, None)
    l2, o2 = token_order(l2_ref, ls_ref.at[0]), token_order(o2_ref, os_ref.at[0])
    l3, o3 = token_order(l3_ref, ls_ref.at[1]), token_order(o3_ref, os_ref.at[1])
    m = jnp.maximum(jnp.maximum(l1, l2), l3)
    w1, w2, w3 = jnp.exp(l1 - m), jnp.exp(l2 - m), jnp.exp(l3 - m)
    inv = 1.0 / (w1 + w2 + w3)
    spread = spread_ref[...]

    def per_lane(w):
        hi, lo = _split2(w * inv)
        return _dot(hi, spread) + _dot(lo, spread)

    y_att = per_lane(w1) * o1 + per_lane(w2) * o2 + per_lane(w3) * o3

    m_gla = _dot(yg_ref[...], wbg_ref[...])
    m_att = _dot(y_att.astype(BF16), wba_ref[...])
    gg = jnp.concatenate([gg0_ref[...], gg1_ref[...]], axis=1).astype(F32)
    ga = jnp.concatenate([ga0_ref[...], ga1_ref[...]], axis=1).astype(F32)
    merged = (0.5 + 0.5 * jnp.tanh(0.5 * gg)) * m_gla + (0.5 + 0.5 * jnp.tanh(0.5 * ga)) * m_att
    h = x_ref[...] + _dot(merged.astype(BF16), wmo_ref[...])
    h_ref[...] = h
    hn = h * lax.rsqrt(jnp.mean(h * h, axis=-1, keepdims=True) + EPS) * nf_ref[...]
    hn_ref[...] = hn.astype(BF16)

    hh, hl = _split2(hn)
    rh, rl = _split2(rw_ref[...])
    logit = _dot_nt(rh, hh) + _dot_nt(rh, hl) + _dot_nt(rl, hh) + rb_ref[...]
    tm = logit.shape[1]
    eid = lax.broadcasted_iota(jnp.int32, (ne, tm), 0)
    member = jnp.zeros((ne, tm), jnp.bool_)
    vals, idxs = [], []
    for _ in range(TOP_K):
        mx = jnp.max(logit, axis=0, keepdims=True)
        idx = jnp.min(jnp.where(logit == mx, eid, ne), axis=0, keepdims=True)
        sel = eid == idx
        member = jnp.logical_or(member, sel)
        logit = jnp.where(sel, -jnp.inf, logit)
        vals.append(mx)
        idxs.append(idx)
    ex = [jnp.exp(v - vals[0]) for v in vals]
    tot = ex[0] + ex[1] + ex[2] + ex[3]
    memf = jnp.where(member, 1.0, 0.0)
    lens = jnp.sum(memf, axis=1, keepdims=True)
    lens_b = jnp.broadcast_to(lens, (ne, LANES))
    off = _dot(ltri_ref[...], lens_b.astype(BF16))
    pos = _dot(memf.astype(BF16), triu_ref[...]) + off[:, 0:1]
    for k in range(TOP_K):
        tg_ref[0, k:k + 1, :] = ex[k] / tot
        lc_ref[0, k:k + 1, :] = jnp.sum(jnp.where(eid == idxs[k], pos, 0.0), axis=0, keepdims=True).astype(jnp.int32)
    carry = carry_ref[...]
    cum_ref[0] = carry.astype(jnp.int32)
    carry = carry + lens_b
    carry_ref[...] = carry
    cnt_ref[...] = carry.astype(jnp.int32)


def _merge(x2, y_gla, att, proj, layout, wbg, wba, wmo, norm_ffn_g, router_w, router_b, seq, tm):
    n, d = x2.shape
    hw = ATT_HEADS_PER_GROUP * ATT_HEAD_DIM
    half = d // 2
    assert n % tm == 0 and seq % tm == 0
    assert tm <= 256, "per-tile expert counts go through a bf16 matmul operand: exact up to 256"
    assert layout['gate_gla'] % half == 0 and layout['gate_att'] % half == 0
    assert all(tm % (o.shape[1] * 16) == 0 for o, _ in att) and att[0][0].shape[1] == 1
    nt = n // tm
    spt = seq // tm
    ggb = layout['gate_gla'] // half
    gab = layout['gate_att'] // half
    triu = jnp.asarray(np.triu(np.ones((tm, tm), np.float32), 1), BF16)
    row = lambda i: (i, 0)
    const = lambda i: (0, 0)

    def att_spec(a):
        dil = a.shape[1]
        return pl.BlockSpec((1, dil, tm // dil, a.shape[3]), lambda i: (i // spt, 0, i % spt, 0))

    (o1, l1), (o2, l2), (o3, l3) = att
    in_specs = [
        pl.BlockSpec((tm, d), row),
        pl.BlockSpec((tm, y_gla.shape[1]), row),
        att_spec(o1), att_spec(o2), att_spec(o3), att_spec(l1), att_spec(l2), att_spec(l3),
        pl.BlockSpec((tm, half), lambda i: (i, ggb)), pl.BlockSpec((tm, half), lambda i: (i, ggb + 1)),
        pl.BlockSpec((tm, half), lambda i: (i, gab)), pl.BlockSpec((tm, half), lambda i: (i, gab + 1)),
        pl.BlockSpec(wbg.shape, const), pl.BlockSpec(wba.shape, const), pl.BlockSpec(wmo.shape, const),
        pl.BlockSpec((1, d), const),
        pl.BlockSpec((N_EXPERTS, d), const),
        pl.BlockSpec((N_EXPERTS, 1), const),
        pl.BlockSpec((tm, tm), const),
        pl.BlockSpec((N_EXPERTS, N_EXPERTS), const),
        pl.BlockSpec((LANES, hw), const),
    ]
    out_specs = [
        pl.BlockSpec((tm, d), row),
        pl.BlockSpec((tm, d), row),
        pl.BlockSpec((1, TOP_K, tm), lambda i: (i, 0, 0)),
        pl.BlockSpec((1, TOP_K, tm), lambda i: (i, 0, 0)),
        pl.BlockSpec((1, N_EXPERTS, LANES), lambda i: (i, 0, 0)),
        pl.BlockSpec((N_EXPERTS, LANES), const),
    ]
    out_shape = [
        jax.ShapeDtypeStruct((n, d), F32),
        jax.ShapeDtypeStruct((n, d), BF16),
        jax.ShapeDtypeStruct((nt, TOP_K, tm), F32),
        jax.ShapeDtypeStruct((nt, TOP_K, tm), jnp.int32),
        jax.ShapeDtypeStruct((nt, N_EXPERTS, LANES), jnp.int32),
        jax.ShapeDtypeStruct((N_EXPERTS, LANES), jnp.int32),
    ]
    ltri = jnp.asarray(np.tril(np.ones((N_EXPERTS, N_EXPERTS), np.float32), -1), BF16)
    spread = jnp.asarray(np.arange(LANES)[:, None] == (np.arange(hw)[None, :] // ATT_HEAD_DIM) * LSE_REP, BF16)
    return pl.pallas_call(
        _merge_kernel,
        grid=(nt,),
        in_specs=in_specs,
        out_specs=out_specs,
        out_shape=out_shape,
        scratch_shapes=[pltpu.VMEM((N_EXPERTS, LANES), F32), pltpu.VMEM((2, hw // LANES, tm, LANES), F32),
                        pltpu.VMEM((2, 1, tm, LANES), F32)],
        compiler_params=_cparams(("arbitrary",)),
        name="merge_router",
    )(x2, y_gla, o1, o2, o3, l1, l2, l3, proj, proj, proj, proj, wbg, wba, wmo,
      norm_ffn_g.reshape(1, d), router_w.T, router_b.reshape(N_EXPERTS, 1), triu, ltri, spread)


LIN_SUB = 8


def _lin_pack(val):
    return [val[:, c * LANES:(c + 1) * LANES] for c in range(LIN_SUB)]


def _lin_unpack(slabs):
    return jnp.concatenate(slabs, axis=1)


def _dispatch_kernel(off_ref, len_ref, dst_ref, zdst_ref, zlen_ref, nact_ref, hn_ref, loc_ref, x_hbm,
                     xs_ref, zero_ref, sem, zsem, *, rloc):
    i = pl.program_id(0)
    nt = pl.num_programs(0)
    slot = i % 2
    ne = N_EXPERTS

    def drain(s):
        rows = rloc * LIN_SUB
        pltpu.make_async_copy(xs_ref.at[s], x_hbm.at[pl.ds(0, rows)], sem.at[s]).wait()

    @pl.when(i == 0)
    def _():
        zero_ref[...] = jnp.zeros_like(zero_ref)

        def fill_copy(e):
            rows = zlen_ref[e] * LIN_SUB
            dst = pl.multiple_of(zdst_ref[e] * LIN_SUB, 8)
            return rows, pltpu.make_async_copy(zero_ref.at[pl.ds(0, rows)], x_hbm.at[pl.ds(dst, rows)], zsem)

        def fill(e, carry):
            rows, copy = fill_copy(e)
            pl.when(rows > 0)(copy.start)
            return carry

        def fill_done(e, carry):
            rows, copy = fill_copy(e)
            pl.when(rows > 0)(copy.wait)
            return carry

        def block_copy(b):
            rows = zero_ref.shape[0]
            return pltpu.make_async_copy(zero_ref, x_hbm.at[pl.ds(pl.multiple_of(b * rows, rows), rows)], zsem)

        def fill_block(b, carry):
            block_copy(b).start()
            return carry

        def fill_block_done(b, carry):
            block_copy(b).wait()
            return carry

        nblk = x_hbm.shape[0] // zero_ref.shape[0]
        lax.fori_loop(0, ne, fill, 0)
        lax.fori_loop(nact_ref[0], nblk, fill_block, 0)
        lax.fori_loop(0, ne, fill_done, 0)
        lax.fori_loop(nact_ref[0], nblk, fill_block_done, 0)

    @pl.when(i >= 2)
    def _():
        drain(slot)

    loc = loc_ref[0]
    tt = loc.shape[1]
    row_id = lax.broadcasted_iota(jnp.int32, (rloc, tt), 0)
    hit = row_id == loc[0:1, :]
    for k in range(1, TOP_K):
        hit = jnp.logical_or(hit, row_id == loc[k:k + 1, :])
    onehot = jnp.where(hit, 1.0, 0.0).astype(BF16)
    xs = _dot(onehot, hn_ref[...])
    for c, slab in enumerate(_lin_pack(xs)):
        xs_ref[slot, pl.ds(c, rloc, stride=LIN_SUB), :] = slab

    def send(e, carry):
        rows = len_ref[i * ne + e] * LIN_SUB

        @pl.when(rows > 0)
        def _():
            src = pl.multiple_of(off_ref[i * ne + e] * LIN_SUB, 8)
            dst = pl.multiple_of(dst_ref[i * ne + e] * LIN_SUB, 8)
            pltpu.make_async_copy(xs_ref.at[slot, pl.ds(src, rows)], x_hbm.at[pl.ds(dst, rows)], sem.at[slot]).start()
        return carry

    lax.fori_loop(0, ne, send, 0)

    @pl.when(i == nt - 1)
    def _():
        @pl.when(i >= 1)
        def _():
            drain(1 - slot)
        drain(slot)


def _dispatch(hn, loc, tabs, nact, p_rows, rloc):
    n, d = hn.shape
    nt, _, tt = loc.shape
    assert d == LIN_SUB * LANES and p_rows % MOE_BLOCK_ROWS == 0
    kern = functools.partial(_dispatch_kernel, rloc=rloc)
    grid_spec = pltpu.PrefetchScalarGridSpec(
        num_scalar_prefetch=6,
        grid=(nt,),
        in_specs=[
            pl.BlockSpec((tt, d), lambda i, *_: (i, 0)),
            pl.BlockSpec((1, TOP_K, tt), lambda i, *_: (i, 0, 0)),
        ],
        out_specs=pl.BlockSpec(memory_space=pl.ANY),
        scratch_shapes=[
            pltpu.VMEM((2, rloc * LIN_SUB, LANES), F32),
            pltpu.VMEM((MOE_BLOCK_ROWS * LIN_SUB, LANES), F32),
            pltpu.SemaphoreType.DMA((2,)),
            pltpu.SemaphoreType.DMA(()),
        ],
    )
    return pl.pallas_call(
        kern,
        grid_spec=grid_spec,
        out_shape=jax.ShapeDtypeStruct((p_rows * LIN_SUB, LANES), F32),
        compiler_params=_cparams(("arbitrary",)),
        name="moe_dispatch",
    )(tabs['off'], tabs['len'], tabs['dst'], tabs['zdst'], tabs['zlen'], nact, hn, loc)


def _expert_kernel(bexp_ref, nact_ref, first_ref, next_ref, x_ref, wgu_hbm, wd_hbm, bgu_ref, bd_ref, perm_ref,
                   y_ref, wgu_f, wd_f, wgu_s, wd_s, sem, *, bm):
    grp = MXU_DIM
    ff2 = wgu_s.shape[1]
    halfg = grp // 2
    brows = bm * LIN_SUB
    bps = x_ref.shape[0] // brows

    def fetch(ex):
        return (pltpu.make_async_copy(wgu_hbm.at[ex], wgu_f, sem.at[0]),
                pltpu.make_async_copy(wd_hbm.at[ex], wd_f, sem.at[1]))

    @pl.when(pl.program_id(0) == 0)
    def _():
        for copy in fetch(bexp_ref[0]):
            copy.start()

    def block(b, r0):
        e = bexp_ref[b]
        active = b < nact_ref[0]

        @pl.when(jnp.logical_and(active, first_ref[b] == 1))
        def _():
            for copy in fetch(e):
                copy.wait()
            perm = perm_ref[...]
            for g in range(ff2 // grp):
                cols = slice(g * grp, (g + 1) * grp)
                wgu_s[:, cols] = _dot(wgu_f[:, cols].astype(BF16), perm).astype(BF16)
            wd_s[...] = wd_f[...].astype(BF16)
            nxt = next_ref[b]

            @pl.when(nxt >= 0)
            def _():
                for copy in fetch(nxt):
                    copy.start()

        @pl.when(active)
        def _():
            x = _lin_unpack([x_ref[pl.ds(r0 + c, bm, stride=LIN_SUB), :] for c in range(LIN_SUB)]).astype(BF16)
            acts = []
            for g in range(ff2 // grp):
                cols = slice(g * grp, (g + 1) * grp)
                hg = _dot(x, wgu_s[:, cols]) + bgu_ref[e, :, cols]
                gate = jnp.minimum(hg[:, :halfg], SWIGLU_LIMIT)
                up = jnp.clip(hg[:, halfg:], -SWIGLU_LIMIT, SWIGLU_LIMIT)
                acts.append(((up + 1.0) * (gate * jax.nn.sigmoid(SWIGLU_ALPHA * gate))).astype(BF16))
            act = jnp.concatenate(acts, axis=1)
            y = _dot(act, wd_s[...]) + bd_ref[e]
            for c, slab in enumerate(_lin_pack(y)):
                y_ref[pl.ds(r0 + c, bm, stride=LIN_SUB), :] = slab

        @pl.when(jnp.logical_not(active))
        def _():
            y_ref[pl.ds(r0, brows), :] = jnp.zeros((brows, y_ref.shape[1]), y_ref.dtype)

    for sub in range(bps):
        block(pl.program_id(0) * bps + sub, sub * brows)


def _gate_up_permutation():
    grp = MXU_DIM
    halfg = grp // 2
    out = np.arange(grp)
    src = np.where(out < halfg, 2 * out, 2 * (out - halfg) + 1)
    p = np.zeros((grp, grp), np.float32)
    p[src, out] = 1.0
    return jnp.asarray(p, BF16)


def _experts(x_lin, bexp, nact, first, nxt, w_gate_up, b_gate_up, w_down, b_down, bm):
    ne, d, ff2 = w_gate_up.shape
    ff = ff2 // 2
    grp = MXU_DIM
    assert ff2 % grp == 0 and d == LIN_SUB * LANES
    nblk = x_lin.shape[0] // (bm * LIN_SUB)
    bps = 2 if nblk % 2 == 0 else 1
    srows = bps * bm * LIN_SUB
    bgu = b_gate_up.reshape(ne, ff2 // grp, grp // 2, 2).transpose(0, 1, 3, 2).reshape(ne, 1, ff2)
    grid_spec = pltpu.PrefetchScalarGridSpec(
        num_scalar_prefetch=4,
        grid=(nblk // bps,),
        in_specs=[
            pl.BlockSpec((srows, LANES), lambda i, *_: (i, 0)),
            pl.BlockSpec(memory_space=pl.ANY),
            pl.BlockSpec(memory_space=pl.ANY),
            pl.BlockSpec(memory_space=pltpu.VMEM),
            pl.BlockSpec(memory_space=pltpu.VMEM),
            pl.BlockSpec((grp, grp), lambda i, *_: (0, 0)),
        ],
        out_specs=pl.BlockSpec((srows, LANES), lambda i, *_: (i, 0)),
        scratch_shapes=[pltpu.VMEM((d, ff2), F32), pltpu.VMEM((ff, d), F32),
                        pltpu.VMEM((d, ff2), BF16), pltpu.VMEM((ff, d), BF16),
                        pltpu.SemaphoreType.DMA((2,))],
    )
    return pl.pallas_call(
        functools.partial(_expert_kernel, bm=bm),
        grid_spec=grid_spec,
        out_shape=jax.ShapeDtypeStruct(x_lin.shape, F32),
        compiler_params=_cparams(("arbitrary",)),
        name="moe_experts",
    )(bexp, nact, first, nxt, x_lin, w_gate_up, w_down, bgu, b_down.reshape(ne, 1, d), _gate_up_permutation())


def _combine_kernel(off_ref, len_ref, dst_ref, h_ref, loc_ref, gate_ref, g_ref, y_hbm, out_ref,
                    ybuf, sem, *, rloc):
    i = pl.program_id(0)
    nt = pl.num_programs(0)
    slot = i % 2
    ne = N_EXPERTS

    def fetch(step, s):
        def run(e, carry):
            rows = len_ref[step * ne + e] * LIN_SUB

            @pl.when(rows > 0)
            def _():
                src = pl.multiple_of(dst_ref[step * ne + e] * LIN_SUB, 8)
                dst = pl.multiple_of(off_ref[step * ne + e] * LIN_SUB, 8)
                pltpu.make_async_copy(y_hbm.at[pl.ds(src, rows)], ybuf.at[s, pl.ds(dst, rows)], sem.at[s]).start()
            return carry

        lax.fori_loop(0, ne, run, 0)

    @pl.when(i == 0)
    def _():
        fetch(0, 0)

    @pl.when(i + 1 < nt)
    def _():
        fetch(i + 1, 1 - slot)

    pltpu.make_async_copy(y_hbm.at[pl.ds(0, rloc * LIN_SUB)], ybuf.at[slot], sem.at[slot]).wait()

    y = _lin_unpack([ybuf[slot, pl.ds(c, rloc, stride=LIN_SUB), :] for c in range(LIN_SUB)]).astype(BF16)
    loc = loc_ref[0]
    gate = gate_ref[0]
    tt = loc.shape[1]
    row_id = lax.broadcasted_iota(jnp.int32, (rloc, tt), 0)
    w = jnp.zeros((rloc, tt), F32)
    for k in range(TOP_K):
        w = w + jnp.where(row_id == loc[k:k + 1, :], gate[k:k + 1, :], 0.0)
    h = h_ref[...] + _dot_tn(w.astype(BF16), y)
    out_ref[...] = h * lax.rsqrt(jnp.mean(h * h, axis=-1, keepdims=True) + EPS) * g_ref[...]


def _combine(h, y_lin, loc, gates, tabs, norm_g, rloc):
    n, d = h.shape
    nt, _, tt = loc.shape
    kern = functools.partial(_combine_kernel, rloc=rloc)
    grid_spec = pltpu.PrefetchScalarGridSpec(
        num_scalar_prefetch=3,
        grid=(nt,),
        in_specs=[
            pl.BlockSpec((tt, d), lambda i, *_: (i, 0)),
            pl.BlockSpec((1, TOP_K, tt), lambda i, *_: (i, 0, 0)),
            pl.BlockSpec((1, TOP_K, tt), lambda i, *_: (i, 0, 0)),
            pl.BlockSpec((1, d), lambda i, *_: (0, 0)),
            pl.BlockSpec(memory_space=pl.ANY),
        ],
        out_specs=pl.BlockSpec((tt, d), lambda i, *_: (i, 0)),
        scratch_shapes=[
            pltpu.VMEM((2, rloc * LIN_SUB, LANES), F32),
            pltpu.SemaphoreType.DMA((2,)),
        ],
    )
    return pl.pallas_call(
        kern,
        grid_spec=grid_spec,
        out_shape=jax.ShapeDtypeStruct((n, d), F32),
        compiler_params=_cparams(("arbitrary",)),
        name="moe_combine",
    )(tabs['off'], tabs['len'], tabs['dst'], h, loc, gates, norm_g.reshape(1, d), y_lin)


def _routing_tables(cum, counts, bm, nblk):
    padded = (counts + bm - 1) // bm * bm
    pend = jnp.cumsum(padded)
    pstart = pend - padded
    blk = jnp.arange(nblk, dtype=jnp.int32)
    bexp_raw = jnp.sum((pend // bm)[None, :] <= blk[:, None], axis=1).astype(jnp.int32)
    nact = (pend[-1] // bm).astype(jnp.int32)
    last = jnp.minimum(bexp_raw[jnp.maximum(nact - 1, 0)], N_EXPERTS - 1)
    bexp = jnp.where(blk < nact, jnp.minimum(bexp_raw, N_EXPERTS - 1), last)
    prev = jnp.concatenate([jnp.full((1,), -1, jnp.int32), bexp[:-1]])
    first = jnp.logical_and(bexp != prev, blk < nact)
    later = jnp.logical_and(first[None, :], blk[None, :] > blk[:, None])
    nxt_blk = jnp.min(jnp.where(later, blk[None, :], nblk), axis=1)
    nxt = jnp.where(nxt_blk < nblk, bexp[jnp.minimum(nxt_blk, nblk - 1)], -1)
    cum_full = jnp.concatenate([cum, counts[None, :]], axis=0)
    run_len = cum_full[1:] - cum_full[:-1]
    run_off = jnp.cumsum(run_len, axis=1) - run_len
    tabs = {
        'off': run_off.reshape(-1), 'len': run_len.reshape(-1),
        'dst': (pstart[None, :] + cum_full[:-1]).reshape(-1),
        'zdst': pstart + counts, 'zlen': padded - counts,
    }
    blocks = (bexp, nact.reshape(1), first.astype(jnp.int32), nxt.astype(jnp.int32))
    return {k: v.astype(jnp.int32) for k, v in tabs.items()}, blocks


MOE_BLOCK_ROWS = 256
TOKEN_TILE = 256


def kernel(x, norm_mix_g, w_in, gla_gate_up, gla_gate_bias, gla_norm_g, w_branch_gla, w_branch_att, w_mix_out,
           norm_ffn_g, router_w, router_b, expert_w_gate_up, expert_b_gate_up, expert_w_down, expert_b_down,
           norm_final_g):
    batch, seq, d = x.shape
    assert norm_mix_g.shape[0] == 1, "single layer"
    n = batch * seq
    gla_k = gla_gate_up.shape[2]
    gla_v = w_branch_gla.shape[1]
    att_w = len(ATT_GROUPS) * ATT_HEADS_PER_GROUP * ATT_HEAD_DIM
    hw = ATT_HEADS_PER_GROUP * ATT_HEAD_DIM
    ngrp = len(ATT_GROUPS)
    assert att_w == ngrp * hw and all(s % hw == 0 for s in (gla_k, gla_v, d))
    assert ATT_GROUPS[0][1] == 1 and all(win // dil == ATT_BLOCK for win, dil in ATT_GROUPS)
    src, off = {}, 0
    for name, size in (('gq', gla_k), ('gk', gla_k), ('gv', gla_v), ('gr', gla_v), ('glr', GLA_GATE_RANK),
                       ('aq', att_w), ('ak', att_w), ('av', att_w), ('gate_gla', d), ('gate_att', d)):
        src[name] = (off, size)
        off += size
    assert off == w_in.shape[2]
    w = w_in[0]
    glr_at, glr_n = src['glr']
    w_glr = w[:, glr_at:glr_at + glr_n]

    def col(name, group=0):
        return src[name][0] + group * hw

    plain = (('gq', gla_k), ('gk', gla_k), ('gv', gla_v), ('gr', gla_v), ('gate_gla', d), ('gate_att', d),
             ('aq', hw), ('ak', hw), ('av', hw))
    layout, tile_cols = {}, []
    for name, size in plain:
        layout[name] = len(tile_cols) * hw
        tile_cols += [col(name) + t * hw for t in range(size // hw)]
    n_plain = len(tile_cols)
    tile_cols += [col(kind, g) for g in range(1, ngrp) for kind in ('aq', 'ak', 'av')]
    dilations = [dil for _, dil in ATT_GROUPS[1:]]

    x2 = x.reshape(n, d)
    proj, glr, *qkv_dil = _in_projection(x2, norm_mix_g[0], w, w_glr, batch, seq, n_plain, dilations,
                                         tuple(tile_cols))
    y_gla = _gla(proj, glr, gla_gate_up[0], gla_gate_bias[0], gla_norm_g[0], batch, seq, layout)
    att = [_dilated_attention(proj.reshape(batch, 1, seq, proj.shape[1]), 1, layout['aq'])]
    att += [_dilated_attention(a, dil, 0) for a, dil in zip(qkv_dil, dilations)]
    tt = min(TOKEN_TILE, n)
    h, hn, gates, loc, cum, cnt = _merge(
        x2, y_gla, att, proj, layout, w_branch_gla[0].astype(BF16), w_branch_att[0].astype(BF16),
        w_mix_out[0].astype(BF16), norm_ffn_g[0], router_w[0], router_b[0], seq, tt)

    bm = MOE_BLOCK_ROWS
    rloc = TOP_K * tt
    nblk = -(-(TOP_K * n) // bm) + N_EXPERTS
    tabs, blocks = _routing_tables(cum[:, :, 0], cnt[:, 0], bm, nblk)
    x_lin = _dispatch(hn, loc, tabs, blocks[1], nblk * bm, rloc)
    y_lin = _experts(x_lin, *blocks, expert_w_gate_up[0], expert_b_gate_up[0], expert_w_down[0],
                     expert_b_down[0], bm)
    out = _combine(h, y_lin, loc, gates, tabs, norm_final_g, rloc)
    return out.reshape(batch, seq, d)
```

```python
import functools
import math

import jax
import jax.numpy as jnp
import numpy as np
from jax import lax
from jax.experimental import pallas as pl
from jax.experimental.pallas import tpu as pltpu

GLA_HEADS = 4
GLA_GATE_RANK = 16
GLA_TAU = 16.0
GLA_CHUNK = 64
ATT_GROUPS = ((128, 1), (512, 4), (2048, 16))
ATT_HEADS_PER_GROUP = 8
ATT_HEAD_DIM = 64
ATT_BLOCK = 128
ROT_DIM = ATT_HEAD_DIM // 4
ROPE_THETA = 500000.0
N_EXPERTS = 32
TOP_K = 4
SWIGLU_LIMIT = 7.0
SWIGLU_ALPHA = 1.702
EPS = 1e-5

LANES = 128
MXU_DIM = 256
VMEM_LIMIT = 56 * 1024 * 1024

F32 = jnp.float32
BF16 = jnp.bfloat16


def _cparams(sem):
    return pltpu.CompilerParams(dimension_semantics=sem, vmem_limit_bytes=VMEM_LIMIT)


def _dot(a, b):
    return jnp.dot(a, b, preferred_element_type=F32)


def _dot_nt(a, b):
    return lax.dot_general(a, b, (((1,), (1,)), ((), ())), preferred_element_type=F32)


def _dot_tn(a, b):
    return lax.dot_general(a, b, (((0,), (0,)), ((), ())), preferred_element_type=F32)


def _split2(a):
    hi = a.astype(BF16)
    lo = (a - hi.astype(F32)).astype(BF16)
    return hi, lo


def _inproj_kernel(x_ref, g_ref, wt_hbm, wglr_ref, cos_ref, sa_ref, sb_ref, proj_ref, glr_ref, *rest,
                   n_plain, dilations, q_scale, tn, tile_rows):
    dil_refs, (xn_ref, stage_ref, w_ref, wst_ref, wsem) = rest[:len(dilations)], rest[len(dilations):]
    nslab = tn // LANES

    @pl.when(pl.program_id(0) == 0)
    def _():
        def window(j):
            return pltpu.make_async_copy(wt_hbm.at[pl.ds(tile_rows[j], tn)], wst_ref.at[j % 2], wsem.at[j % 2])

        window(0).start()
        for j in range(len(tile_rows)):
            if j + 1 < len(tile_rows):
                window(j + 1).start()
            window(j).wait()
            for s in range(nslab):
                w_ref[:, j * tn + s * LANES:j * tn + (s + 1) * LANES] = \
                    jnp.transpose(wst_ref[j % 2, s * LANES:(s + 1) * LANES, :]).astype(BF16)

    x = x_ref[...]
    y = x * lax.rsqrt(jnp.mean(x * x, axis=-1, keepdims=True) + EPS) * g_ref[...]
    xn_ref[...] = y.astype(BF16)
    yh, yl = _split2(y)
    wh, wl = _split2(wglr_ref[...])
    glr_ref[...] = _dot_nt(yh, wh) + _dot_nt(yh, wl) + _dot_nt(yl, wh)

    tm = xn_ref.shape[0]
    half = ROT_DIM // 2

    def slabs(j, kind):
        acc = _dot(xn_ref[...], w_ref[:, j * tn:(j + 1) * tn])
        for s in range(nslab):
            t = acc[:, s * LANES:(s + 1) * LANES]
            if kind < 2:
                t = t * cos_ref[...] + pltpu.roll(t, LANES - half, 1) * sa_ref[...] \
                    + pltpu.roll(t, half, 1) * sb_ref[...]
                if kind == 0:
                    t = t * q_scale
            yield s, t

    for gi, dil in enumerate(dilations):
        for kind in range(3):
            j = n_plain + 3 * gi + kind
            buf = (3 * gi + kind) % stage_ref.shape[0]
            for s, t in slabs(j, kind):
                stage_ref[buf, s] = t
            for r in range(dil):
                for s in range(nslab):
                    dil_refs[gi][0, r, :, kind * tn + s * LANES:kind * tn + (s + 1) * LANES] = \
                        stage_ref[buf, s, pl.ds(r, tm // dil, stride=dil), :].astype(BF16)

    q0 = n_plain - 3
    for j in range(n_plain):
        for s, t in slabs(j, j - q0 if j >= q0 else 2):
            proj_ref[:, j * tn + s * LANES:j * tn + (s + 1) * LANES] = t.astype(BF16)


def _rope_tables(seq):
    half = ROT_DIM // 2
    inv_freq = ROPE_THETA ** (-np.arange(half, dtype=np.float32) * np.float32(2.0 / ROT_DIM))
    pos = jnp.arange(seq, dtype=F32)
    ang = pos[:, None] * jnp.asarray(inv_freq, F32)[None, :]
    cos = jnp.cos(ang)
    sin = jnp.sin(ang)
    lane = np.arange(LANES) % ATT_HEAD_DIM
    idx = np.where(lane < ROT_DIM, lane % half, 0)
    in_rot = jnp.asarray(lane < ROT_DIM)
    first = jnp.asarray(lane < half)
    second = jnp.asarray((lane >= half) & (lane < ROT_DIM))
    cos_t = jnp.where(in_rot[None, :], cos[:, idx], 1.0)
    sin_g = sin[:, idx]
    sa = jnp.where(first[None, :], -sin_g, 0.0)
    sb = jnp.where(second[None, :], sin_g, 0.0)
    return cos_t.astype(F32), sa.astype(F32), sb.astype(F32)


def _in_projection(x2, norm_g, wt, glr_row, batch, seq, n_plain, dilations, tile_rows):
    n, d = x2.shape
    tn = ATT_HEADS_PER_GROUP * ATT_HEAD_DIM
    tm = min(512, seq)
    ntile = len(tile_rows)
    assert n % tm == 0 and seq % tm == 0 and ntile == n_plain + 3 * len(dilations)
    assert all(tm % (dil * 16) == 0 for dil in dilations)
    assert glr_row % GLA_GATE_RANK == 0 and all(r % 8 == 0 and r + tn <= wt.shape[0] for r in tile_rows)
    cos_t, sa, sb = _rope_tables(seq)
    spt = seq // tm
    kern = functools.partial(_inproj_kernel, n_plain=n_plain, dilations=tuple(dilations),
                             q_scale=ATT_HEAD_DIM ** -0.5, tn=tn, tile_rows=tuple(tile_rows))
    out_specs = [
        pl.BlockSpec((tm, n_plain * tn), lambda i: (i, 0)),
        pl.BlockSpec((tm, GLA_GATE_RANK), lambda i: (i, 0)),
    ]
    out_shape = [
        jax.ShapeDtypeStruct((n, n_plain * tn), BF16),
        jax.ShapeDtypeStruct((n, GLA_GATE_RANK), F32),
    ]
    for dil in dilations:
        out_specs.append(pl.BlockSpec((1, dil, tm // dil, 3 * tn), lambda i: (i // spt, 0, i % spt, 0)))
        out_shape.append(jax.ShapeDtypeStruct((batch, dil, seq // dil, 3 * tn), BF16))
    const = lambda i: (0, 0)
    return pl.pallas_call(
        kern,
        grid=(n // tm,),
        in_specs=[
            pl.BlockSpec((tm, d), lambda i: (i, 0)),
            pl.BlockSpec((1, d), const),
            pl.BlockSpec(memory_space=pl.ANY),
            pl.BlockSpec((GLA_GATE_RANK, d), lambda i: (glr_row // GLA_GATE_RANK, 0)),
            pl.BlockSpec((tm, LANES), lambda i: (i % spt, 0)),
            pl.BlockSpec((tm, LANES), lambda i: (i % spt, 0)),
            pl.BlockSpec((tm, LANES), lambda i: (i % spt, 0)),
        ],
        out_specs=out_specs,
        out_shape=out_shape,
        scratch_shapes=[pltpu.VMEM((tm, d), BF16), pltpu.VMEM((2, tn // LANES, tm, LANES), F32),
                        pltpu.VMEM((d, ntile * tn), BF16),
                        pltpu.VMEM((2, tn, d), F32), pltpu.SemaphoreType.DMA((2,))],
        compiler_params=_cparams(("arbitrary",)),
        name="in_projection",
    )(x2, norm_g.reshape(1, d), wt, wt, cos_t, sa, sb)


def _gla_kernel(q_ref, k_ref, v_ref, r_ref, glr_ref, up_ref, bias_ref, ng_ref, tri_ref, y_ref, state_ref,
                *, dk, dv, ts):
    t = pl.program_id(2)

    @pl.when(t == 0)
    def _():
        state_ref[...] = jnp.zeros_like(state_ref)

    c = GLA_CHUNK
    hps = state_ref.shape[0]
    gh, gl = _split2(glr_ref[...])
    uh, ul = _split2(up_ref[...])
    z = _dot(gh, uh) + _dot(gh, ul) + _dot(gl, uh) + bias_ref[...]
    logdec = (jnp.minimum(z, 0.0) - jnp.log1p(jnp.exp(-jnp.abs(z)))) * (1.0 / GLA_TAU)
    ldh, ldl = _split2(logdec)
    tri = tri_ref[...]
    row = lax.broadcasted_iota(jnp.int32, (c, c), 0)
    col = lax.broadcasted_iota(jnp.int32, (c, c), 1)
    causal = col <= row
    scale = dk ** -0.5
    ng = ng_ref[...]
    states = [state_ref[hh] for hh in range(hps)]
    for ci in range(ts // c):
        sl = slice(ci * c, (ci + 1) * c)
        b2 = _dot(tri, ldh[sl]) + _dot(tri, ldl[sl])
        for hh in range(hps):
            ks = slice(hh * dk, (hh + 1) * dk)
            vs = slice(hh * dv, (hh + 1) * dv)
            b = b2[:, ks]
            bl = b[c - 1:c, :]
            eb = jnp.exp(b)
            enb = jnp.exp(-b)
            ebl = jnp.exp(bl)
            qf = q_ref[sl, ks].astype(F32)
            kf = k_ref[sl, ks].astype(F32)
            qe = (qf * scale * eb).astype(BF16)
            ke = (kf * enb).astype(BF16)
            kd = (kf * enb * ebl).astype(BF16)
            vb = v_ref[sl, vs]
            a = jnp.where(causal, _dot_nt(qe, ke), 0.0).astype(BF16)
            o = _dot(a, vb) + _dot(qe, states[hh].astype(BF16))
            dec = jnp.transpose(jnp.broadcast_to(ebl, (dk, dk)))
            dec_full = jnp.concatenate([dec] * (dv // dk), axis=1)
            states[hh] = dec_full * states[hh] + _dot_tn(kd, vb)
            o = o * lax.rsqrt(jnp.mean(o * o, axis=-1, keepdims=True) + EPS) * ng
            rf = r_ref[sl, vs].astype(F32)
            o = o * (rf * jax.nn.sigmoid(rf))
            y_ref[sl, vs] = o.astype(BF16)
    for hh in range(hps):
        state_ref[hh] = states[hh]


def _gla(proj, glr, gate_up, gate_bias, norm_g, batch, seq, layout):
    n = proj.shape[0]
    dk = (layout['gk'] - layout['gq']) // GLA_HEADS
    dv = (layout['gr'] - layout['gv']) // GLA_HEADS
    ts = min(512, seq)
    assert dk == LANES and dv % dk == 0 and seq % ts == 0 and ts % GLA_CHUNK == 0
    spt = seq // ts
    hps = GLA_HEADS
    wk, wv = hps * dk, hps * dv
    assert GLA_HEADS % hps == 0 and all(layout[s] % wk == 0 for s in ('gq', 'gk'))
    assert all(layout[s] % wv == 0 for s in ('gv', 'gr'))
    qb, kb = layout['gq'] // wk, layout['gk'] // wk
    vb, rb = layout['gv'] // wv, layout['gr'] // wv
    tri = jnp.asarray(np.tril(np.ones((GLA_CHUNK, GLA_CHUNK), np.float32)), BF16)
    kern = functools.partial(_gla_kernel, dk=dk, dv=dv, ts=ts)
    row = lambda b, h, t: b * spt + t
    return pl.pallas_call(
        kern,
        grid=(batch, GLA_HEADS // hps, spt),
        in_specs=[
            pl.BlockSpec((ts, wk), lambda b, h, t: (row(b, h, t), qb + h)),
            pl.BlockSpec((ts, wk), lambda b, h, t: (row(b, h, t), kb + h)),
            pl.BlockSpec((ts, wv), lambda b, h, t: (row(b, h, t), vb + h)),
            pl.BlockSpec((ts, wv), lambda b, h, t: (row(b, h, t), rb + h)),
            pl.BlockSpec((ts, GLA_GATE_RANK), lambda b, h, t: (row(b, h, t), 0)),
            pl.BlockSpec((GLA_GATE_RANK, wk), lambda b, h, t: (0, h)),
            pl.BlockSpec((1, wk), lambda b, h, t: (0, h)),
            pl.BlockSpec((1, dv), lambda b, h, t: (0, 0)),
            pl.BlockSpec((GLA_CHUNK, GLA_CHUNK), lambda b, h, t: (0, 0)),
        ],
        out_specs=pl.BlockSpec((ts, wv), lambda b, h, t: (row(b, h, t), h)),
        out_shape=jax.ShapeDtypeStruct((n, GLA_HEADS * dv), BF16),
        scratch_shapes=[pltpu.VMEM((hps, dk, dv), F32)],
        compiler_params=_cparams(("arbitrary", "arbitrary", "arbitrary")),
        name="gla",
    )(proj, proj, proj, proj, glr, gate_up, gate_bias.reshape(1, -1), norm_g.reshape(1, -1), tri)


LSE_REP = LANES // ATT_HEADS_PER_GROUP


def _att_kernel(q_ref, k_ref, v_ref, o_ref, lse_ref, *, nb, pairs, unroll):
    blk = ATT_BLOCK
    hd = ATT_HEAD_DIM
    row = lax.broadcasted_iota(jnp.int32, (2 * blk, 2 * blk), 0) % blk
    col = lax.broadcasted_iota(jnp.int32, (2 * blk, 2 * blk), 1)
    dist = row + blk - col
    band = jnp.logical_and(dist >= 0, dist <= blk)
    causal = band[:, blk:]
    lane = lax.broadcasted_iota(jnp.int32, (blk, LANES), 1)
    head0 = lane < hd
    lane_head = lane // LSE_REP
    neg = -jnp.inf

    def block(j, first):
        r0 = pl.multiple_of(j * blk, blk)
        lse = jnp.zeros((blk, LANES), F32)
        for p in range(pairs):
            cs = slice(p * LANES, (p + 1) * LANES)
            q2 = q_ref[0, 0, pl.ds(r0, blk), cs]
            zero = jnp.zeros_like(q2)
            qs = jnp.concatenate([jnp.where(head0, q2, zero), jnp.where(head0, zero, q2)], axis=0)
            if first:
                kk = k_ref[0, 0, pl.ds(r0, blk), cs]
                vv = v_ref[0, 0, pl.ds(r0, blk), cs]
                s = jnp.where(causal, _dot_nt(qs, kk), neg)
            else:
                k0 = pl.multiple_of(r0 - blk, blk)
                kk = k_ref[0, 0, pl.ds(k0, 2 * blk), cs]
                vv = v_ref[0, 0, pl.ds(k0, 2 * blk), cs]
                s = jnp.where(band, _dot_nt(qs, kk), neg)
            mx = jnp.max(s, axis=-1, keepdims=True)
            e = jnp.exp(s - mx).astype(BF16)
            one = jnp.ones_like(vv)
            hv = lax.broadcasted_iota(jnp.int32, vv.shape, 1) < hd
            out0 = _dot(e[:blk], jnp.where(hv, vv, one))
            out1 = _dot(e[blk:], jnp.where(hv, one, vv))
            num = jnp.where(head0, out0, out1)
            den_x = jnp.where(head0, out1, out0)
            den = pltpu.roll(den_x, hd, 1)
            o_ref[0, 0, pl.ds(r0, blk), cs] = (num / den).astype(o_ref.dtype)
            lse0 = mx[:blk] + jnp.log(jnp.where(head0, den, den_x))
            lse1 = mx[blk:] + jnp.log(jnp.where(head0, den_x, den))
            lse = jnp.where(lane_head == 2 * p, lse0, jnp.where(lane_head == 2 * p + 1, lse1, lse))
        lse_ref[0, 0, pl.ds(r0, blk), :] = lse

    block(0, True)

    def body(j, carry):
        block(j, False)
        return carry

    lax.fori_loop(1, nb, body, 0, unroll=unroll)


def _dilated_attention(qkv, dilation, col0):
    batch, dil, sub, width = qkv.shape
    hw = ATT_HEADS_PER_GROUP * ATT_HEAD_DIM
    assert dil == dilation and sub % ATT_BLOCK == 0 and col0 % hw == 0
    nb = sub // ATT_BLOCK
    pairs = hw // LANES
    assert 2 * pairs * LSE_REP == LANES
    kern = functools.partial(_att_kernel, nb=nb, pairs=pairs, unroll=1)

    def in_spec(kind):
        base = (col0 + kind * hw) // hw
        return pl.BlockSpec((1, 1, sub, hw), lambda b, r: (b, r, 0, base))

    return pl.pallas_call(
        kern,
        grid=(batch, dil),
        in_specs=[in_spec(0), in_spec(1), in_spec(2)],
        out_specs=[pl.BlockSpec((1, 1, sub, hw), lambda b, r: (b, r, 0, 0)),
                   pl.BlockSpec((1, 1, sub, LANES), lambda b, r: (b, r, 0, 0))],
        out_shape=[
            jax.ShapeDtypeStruct((batch, dil, sub, hw), BF16),
            jax.ShapeDtypeStruct((batch, dil, sub, LANES), F32),
        ],
        compiler_params=_cparams(("arbitrary", "arbitrary")),
        name=f"dilated_attention_d{dilation}",
    )(qkv, qkv, qkv)


def _merge_kernel(x_ref, yg_ref, o1_ref, o2_ref, o3_ref, l1_ref, l2_ref, l3_ref,
                  gg0_ref, gg1_ref, ga0_ref, ga1_ref, wbg_ref, wba_ref, wmo_ref, nf_ref, rw_ref, rb_ref, triu_ref,
                  ltri_ref, spread_ref, h_ref, hn_ref, tg_ref, lc_ref, cum_ref, cnt_ref, carry_ref, os_ref, ls_ref):
    i = pl.program_id(0)
    ne = N_EXPERTS

    @pl.when(i == 0)
    def _():
        carry_ref[...] = jnp.zeros_like(carry_ref)

    def token_order(src_ref, stage_ref):
        dil, sub, w = src_ref.shape[1:]
        if dil == 1:
            return src_ref[0, 0].astype(F32)
        for r in range(dil):
            val = src_ref[0, r].astype(F32)
            for s in range(w // LANES):
                stage_ref[s, pl.ds(r, sub, stride=dil), :] = val[:, s * LANES:(s + 1) * LANES]
        return jnp.concatenate([stage_ref[s] for s in range(w // LANES)], axis=1)

    l1, o1 = token_order(l1_ref, None), token_order(o1_ref, None)
    l2, o2 = token_order(l2_ref, ls_ref.at[0]), token_order(o2_ref, os_ref.at[0])
    l3, o3 = token_order(l3_ref, ls_ref.at[1]), token_order(o3_ref, os_ref.at[1])
    m = jnp.maximum(jnp.maximum(l1, l2), l3)
    w1, w2, w3 = jnp.exp(l1 - m), jnp.exp(l2 - m), jnp.exp(l3 - m)
    inv = 1.0 / (w1 + w2 + w3)
    spread = spread_ref[...]

    def per_lane(w):
        hi, lo = _split2(w * inv)
        return _dot(hi, spread) + _dot(lo, spread)

    y_att = per_lane(w1) * o1 + per_lane(w2) * o2 + per_lane(w3) * o3

    m_gla = _dot(yg_ref[...], wbg_ref[...])
    m_att = _dot(y_att.astype(BF16), wba_ref[...])
    gg = jnp.concatenate([gg0_ref[...], gg1_ref[...]], axis=1).astype(F32)
    ga = jnp.concatenate([ga0_ref[...], ga1_ref[...]], axis=1).astype(F32)
    merged = (0.5 + 0.5 * jnp.tanh(0.5 * gg)) * m_gla + (0.5 + 0.5 * jnp.tanh(0.5 * ga)) * m_att
    h = x_ref[...] + _dot(merged.astype(BF16), wmo_ref[...])
    h_ref[...] = h
    hn = h * lax.rsqrt(jnp.mean(h * h, axis=-1, keepdims=True) + EPS) * nf_ref[...]
    hn_ref[...] = hn.astype(BF16)

    hh, hl = _split2(hn)
    rh, rl = _split2(rw_ref[...])
    logit = _dot_nt(rh, hh) + _dot_nt(rh, hl) + _dot_nt(rl, hh) + rb_ref[...]
    tm = logit.shape[1]
    eid = lax.broadcasted_iota(jnp.int32, (ne, tm), 0)
    member = jnp.zeros((ne, tm), jnp.bool_)
    vals, idxs = [], []
    for _ in range(TOP_K):
        mx = jnp.max(logit, axis=0, keepdims=True)
        idx = jnp.min(jnp.where(logit == mx, eid, ne), axis=0, keepdims=True)
        sel = eid == idx
        member = jnp.logical_or(member, sel)
        logit = jnp.where(sel, -jnp.inf, logit)
        vals.append(mx)
        idxs.append(idx)
    ex = [jnp.exp(v - vals[0]) for v in vals]
    tot = ex[0] + ex[1] + ex[2] + ex[3]
    memf = jnp.where(member, 1.0, 0.0)
    lens = jnp.sum(memf, axis=1, keepdims=True)
    lens_b = jnp.broadcast_to(lens, (ne, LANES))
    off = _dot(ltri_ref[...], lens_b.astype(BF16))
    pos = _dot(memf.astype(BF16), triu_ref[...]) + off[:, 0:1]
    for k in range(TOP_K):
        tg_ref[0, k:k + 1, :] = ex[k] / tot
        lc_ref[0, k:k + 1, :] = jnp.sum(jnp.where(eid == idxs[k], pos, 0.0), axis=0, keepdims=True).astype(jnp.int32)
    carry = carry_ref[...]
    cum_ref[0] = carry.astype(jnp.int32)
    carry = carry + lens_b
    carry_ref[...] = carry
    cnt_ref[...] = carry.astype(jnp.int32)


def _merge(x2, y_gla, att, proj, layout, wbg, wba, wmo, norm_ffn_g, router_w, router_b, seq, tm):
    n, d = x2.shape
    hw = ATT_HEADS_PER_GROUP * ATT_HEAD_DIM
    half = d // 2
    assert n % tm == 0 and seq % tm == 0
    assert tm <= 256, "per-tile expert counts go through a bf16 matmul operand: exact up to 256"
    assert layout['gate_gla'] % half == 0 and layout['gate_att'] % half == 0
    assert all(tm % (o.shape[1] * 16) == 0 for o, _ in att) and att[0][0].shape[1] == 1
    nt = n // tm
    spt = seq // tm
    ggb = layout['gate_gla'] // half
    gab = layout['gate_att'] // half
    triu = jnp.asarray(np.triu(np.ones((tm, tm), np.float32), 1), BF16)
    row = lambda i: (i, 0)
    const = lambda i: (0, 0)

    def att_spec(a):
        dil = a.shape[1]
        return pl.BlockSpec((1, dil, tm // dil, a.shape[3]), lambda i: (i // spt, 0, i % spt, 0))

    (o1, l1), (o2, l2), (o3, l3) = att
    in_specs = [
        pl.BlockSpec((tm, d), row),
        pl.BlockSpec((tm, y_gla.shape[1]), row),
        att_spec(o1), att_spec(o2), att_spec(o3), att_spec(l1), att_spec(l2), att_spec(l3),
        pl.BlockSpec((tm, half), lambda i: (i, ggb)), pl.BlockSpec((tm, half), lambda i: (i, ggb + 1)),
        pl.BlockSpec((tm, half), lambda i: (i, gab)), pl.BlockSpec((tm, half), lambda i: (i, gab + 1)),
        pl.BlockSpec(wbg.shape, const), pl.BlockSpec(wba.shape, const), pl.BlockSpec(wmo.shape, const),
        pl.BlockSpec((1, d), const),
        pl.BlockSpec((N_EXPERTS, d), const),
        pl.BlockSpec((N_EXPERTS, 1), const),
        pl.BlockSpec((tm, tm), const),
        pl.BlockSpec((N_EXPERTS, N_EXPERTS), const),
        pl.BlockSpec((LANES, hw), const),
    ]
    out_specs = [
        pl.BlockSpec((tm, d), row),
        pl.BlockSpec((tm, d), row),
        pl.BlockSpec((1, TOP_K, tm), lambda i: (i, 0, 0)),
        pl.BlockSpec((1, TOP_K, tm), lambda i: (i, 0, 0)),
        pl.BlockSpec((1, N_EXPERTS, LANES), lambda i: (i, 0, 0)),
        pl.BlockSpec((N_EXPERTS, LANES), const),
    ]
    out_shape = [
        jax.ShapeDtypeStruct((n, d), F32),
        jax.ShapeDtypeStruct((n, d), BF16),
        jax.ShapeDtypeStruct((nt, TOP_K, tm), F32),
        jax.ShapeDtypeStruct((nt, TOP_K, tm), jnp.int32),
        jax.ShapeDtypeStruct((nt, N_EXPERTS, LANES), jnp.int32),
        jax.ShapeDtypeStruct((N_EXPERTS, LANES), jnp.int32),
    ]
    ltri = jnp.asarray(np.tril(np.ones((N_EXPERTS, N_EXPERTS), np.float32), -1), BF16)
    spread = jnp.asarray(np.arange(LANES)[:, None] == (np.arange(hw)[None, :] // ATT_HEAD_DIM) * LSE_REP, BF16)
    return pl.pallas_call(
        _merge_kernel,
        grid=(nt,),
        in_specs=in_specs,
        out_specs=out_specs,
        out_shape=out_shape,
        scratch_shapes=[pltpu.VMEM((N_EXPERTS, LANES), F32), pltpu.VMEM((2, hw // LANES, tm, LANES), F32),
                        pltpu.VMEM((2, 1, tm, LANES), F32)],
        compiler_params=_cparams(("arbitrary",)),
        name="merge_router",
    )(x2, y_gla, o1, o2, o3, l1, l2, l3, proj, proj, proj, proj, wbg, wba, wmo,
      norm_ffn_g.reshape(1, d), router_w.T, router_b.reshape(N_EXPERTS, 1), triu, ltri, spread)


LIN_SUB = 8


def _lin_pack(val):
    return [val[:, c * LANES:(c + 1) * LANES] for c in range(LIN_SUB)]


def _lin_unpack(slabs):
    return jnp.concatenate(slabs, axis=1)


def _dispatch_kernel(off_ref, len_ref, dst_ref, zdst_ref, zlen_ref, nact_ref, hn_ref, loc_ref, x_hbm,
                     xs_ref, zero_ref, sem, zsem, *, rloc):
    i = pl.program_id(0)
    nt = pl.num_programs(0)
    slot = i % 2
    ne = N_EXPERTS

    def drain(s):
        rows = rloc * LIN_SUB
        pltpu.make_async_copy(xs_ref.at[s], x_hbm.at[pl.ds(0, rows)], sem.at[s]).wait()

    @pl.when(i == 0)
    def _():
        zero_ref[...] = jnp.zeros_like(zero_ref)

        def fill_copy(e):
            rows = zlen_ref[e] * LIN_SUB
            dst = pl.multiple_of(zdst_ref[e] * LIN_SUB, 8)
            return rows, pltpu.make_async_copy(zero_ref.at[pl.ds(0, rows)], x_hbm.at[pl.ds(dst, rows)], zsem)

        def fill(e, carry):
            rows, copy = fill_copy(e)
            pl.when(rows > 0)(copy.start)
            return carry

        def fill_done(e, carry):
            rows, copy = fill_copy(e)
            pl.when(rows > 0)(copy.wait)
            return carry

        def block_copy(b):
            rows = zero_ref.shape[0]
            return pltpu.make_async_copy(zero_ref, x_hbm.at[pl.ds(pl.multiple_of(b * rows, rows), rows)], zsem)

        def fill_block(b, carry):
            block_copy(b).start()
            return carry

        def fill_block_done(b, carry):
            block_copy(b).wait()
            return carry

        nblk = x_hbm.shape[0] // zero_ref.shape[0]
        lax.fori_loop(0, ne, fill, 0)
        lax.fori_loop(nact_ref[0], nblk, fill_block, 0)
        lax.fori_loop(0, ne, fill_done, 0)
        lax.fori_loop(nact_ref[0], nblk, fill_block_done, 0)

    @pl.when(i >= 2)
    def _():
        drain(slot)

    loc = loc_ref[0]
    tt = loc.shape[1]
    row_id = lax.broadcasted_iota(jnp.int32, (rloc, tt), 0)
    hit = row_id == loc[0:1, :]
    for k in range(1, TOP_K):
        hit = jnp.logical_or(hit, row_id == loc[k:k + 1, :])
    onehot = jnp.where(hit, 1.0, 0.0).astype(BF16)
    xs = _dot(onehot, hn_ref[...])
    for c, slab in enumerate(_lin_pack(xs)):
        xs_ref[slot, pl.ds(c, rloc, stride=LIN_SUB), :] = slab

    def send(e, carry):
        rows = len_ref[i * ne + e] * LIN_SUB

        @pl.when(rows > 0)
        def _():
            src = pl.multiple_of(off_ref[i * ne + e] * LIN_SUB, 8)
            dst = pl.multiple_of(dst_ref[i * ne + e] * LIN_SUB, 8)
            pltpu.make_async_copy(xs_ref.at[slot, pl.ds(src, rows)], x_hbm.at[pl.ds(dst, rows)], sem.at[slot]).start()
        return carry

    lax.fori_loop(0, ne, send, 0)

    @pl.when(i == nt - 1)
    def _():
        @pl.when(i >= 1)
        def _():
            drain(1 - slot)
        drain(slot)


def _dispatch(hn, loc, tabs, nact, p_rows, rloc):
    n, d = hn.shape
    nt, _, tt = loc.shape
    assert d == LIN_SUB * LANES and p_rows % MOE_BLOCK_ROWS == 0
    kern = functools.partial(_dispatch_kernel, rloc=rloc)
    grid_spec = pltpu.PrefetchScalarGridSpec(
        num_scalar_prefetch=6,
        grid=(nt,),
        in_specs=[
            pl.BlockSpec((tt, d), lambda i, *_: (i, 0)),
            pl.BlockSpec((1, TOP_K, tt), lambda i, *_: (i, 0, 0)),
        ],
        out_specs=pl.BlockSpec(memory_space=pl.ANY),
        scratch_shapes=[
            pltpu.VMEM((2, rloc * LIN_SUB, LANES), F32),
            pltpu.VMEM((MOE_BLOCK_ROWS * LIN_SUB, LANES), F32),
            pltpu.SemaphoreType.DMA((2,)),
            pltpu.SemaphoreType.DMA(()),
        ],
    )
    return pl.pallas_call(
        kern,
        grid_spec=grid_spec,
        out_shape=jax.ShapeDtypeStruct((p_rows * LIN_SUB, LANES), F32),
        compiler_params=_cparams(("arbitrary",)),
        name="moe_dispatch",
    )(tabs['off'], tabs['len'], tabs['dst'], tabs['zdst'], tabs['zlen'], nact, hn, loc)


def _expert_kernel(bexp_ref, nact_ref, first_ref, next_ref, x_ref, wgu_hbm, wd_hbm, bgu_ref, bd_ref, perm_ref,
                   y_ref, wgu_f, wd_f, wgu_s, wd_s, sem, *, bm):
    grp = MXU_DIM
    ff2 = wgu_s.shape[1]
    halfg = grp // 2
    brows = bm * LIN_SUB
    bps = x_ref.shape[0] // brows

    def fetch(ex):
        return (pltpu.make_async_copy(wgu_hbm.at[ex], wgu_f, sem.at[0]),
                pltpu.make_async_copy(wd_hbm.at[ex], wd_f, sem.at[1]))

    @pl.when(pl.program_id(0) == 0)
    def _():
        for copy in fetch(bexp_ref[0]):
            copy.start()

    def block(b, r0):
        e = bexp_ref[b]
        active = b < nact_ref[0]

        @pl.when(jnp.logical_and(active, first_ref[b] == 1))
        def _():
            for copy in fetch(e):
                copy.wait()
            perm = perm_ref[...]
            for g in range(ff2 // grp):
                cols = slice(g * grp, (g + 1) * grp)
                wgu_s[:, cols] = _dot(wgu_f[:, cols].astype(BF16), perm).astype(BF16)
            wd_s[...] = wd_f[...].astype(BF16)
            nxt = next_ref[b]

            @pl.when(nxt >= 0)
            def _():
                for copy in fetch(nxt):
                    copy.start()

        @pl.when(active)
        def _():
            x = _lin_unpack([x_ref[pl.ds(r0 + c, bm, stride=LIN_SUB), :] for c in range(LIN_SUB)]).astype(BF16)
            acts = []
            for g in range(ff2 // grp):
                cols = slice(g * grp, (g + 1) * grp)
                hg = _dot(x, wgu_s[:, cols]) + bgu_ref[e, :, cols]
                gate = jnp.minimum(hg[:, :halfg], SWIGLU_LIMIT)
                up = jnp.clip(hg[:, halfg:], -SWIGLU_LIMIT, SWIGLU_LIMIT)
                acts.append(((up + 1.0) * (gate * jax.nn.sigmoid(SWIGLU_ALPHA * gate))).astype(BF16))
            act = jnp.concatenate(acts, axis=1)
            y = _dot(act, wd_s[...]) + bd_ref[e]
            for c, slab in enumerate(_lin_pack(y)):
                y_ref[pl.ds(r0 + c, bm, stride=LIN_SUB), :] = slab

        @pl.when(jnp.logical_not(active))
        def _():
            y_ref[pl.ds(r0, brows), :] = jnp.zeros((brows, y_ref.shape[1]), y_ref.dtype)

    for sub in range(bps):
        block(pl.program_id(0) * bps + sub, sub * brows)


def _gate_up_permutation():
    grp = MXU_DIM
    halfg = grp // 2
    out = np.arange(grp)
    src = np.where(out < halfg, 2 * out, 2 * (out - halfg) + 1)
    p = np.zeros((grp, grp), np.float32)
    p[src, out] = 1.0
    return jnp.asarray(p, BF16)


def _experts(x_lin, bexp, nact, first, nxt, w_gate_up, b_gate_up, w_down, b_down, bm):
    ne, d, ff2 = w_gate_up.shape
    ff = ff2 // 2
    grp = MXU_DIM
    assert ff2 % grp == 0 and d == LIN_SUB * LANES
    nblk = x_lin.shape[0] // (bm * LIN_SUB)
    bps = 2 if nblk % 2 == 0 else 1
    srows = bps * bm * LIN_SUB
    bgu = b_gate_up.reshape(ne, ff2 // grp, grp // 2, 2).transpose(0, 1, 3, 2).reshape(ne, 1, ff2)
    grid_spec = pltpu.PrefetchScalarGridSpec(
        num_scalar_prefetch=4,
        grid=(nblk // bps,),
        in_specs=[
            pl.BlockSpec((srows, LANES), lambda i, *_: (i, 0)),
            pl.BlockSpec(memory_space=pl.ANY),
            pl.BlockSpec(memory_space=pl.ANY),
            pl.BlockSpec(memory_space=pltpu.VMEM),
            pl.BlockSpec(memory_space=pltpu.VMEM),
            pl.BlockSpec((grp, grp), lambda i, *_: (0, 0)),
        ],
        out_specs=pl.BlockSpec((srows, LANES), lambda i, *_: (i, 0)),
        scratch_shapes=[pltpu.VMEM((d, ff2), F32), pltpu.VMEM((ff, d), F32),
                        pltpu.VMEM((d, ff2), BF16), pltpu.VMEM((ff, d), BF16),
                        pltpu.SemaphoreType.DMA((2,))],
    )
    return pl.pallas_call(
        functools.partial(_expert_kernel, bm=bm),
        grid_spec=grid_spec,
        out_shape=jax.ShapeDtypeStruct(x_lin.shape, F32),
        compiler_params=_cparams(("arbitrary",)),
        name="moe_experts",
    )(bexp, nact, first, nxt, x_lin, w_gate_up, w_down, bgu, b_down.reshape(ne, 1, d), _gate_up_permutation())


def _combine_kernel(off_ref, len_ref, dst_ref, h_ref, loc_ref, gate_ref, g_ref, y_hbm, out_ref,
                    ybuf, sem, *, rloc):
    i = pl.program_id(0)
    nt = pl.num_programs(0)
    slot = i % 2
    ne = N_EXPERTS

    def fetch(step, s):
        def run(e, carry):
            rows = len_ref[step * ne + e] * LIN_SUB

            @pl.when(rows > 0)
            def _():
                src = pl.multiple_of(dst_ref[step * ne + e] * LIN_SUB, 8)
                dst = pl.multiple_of(off_ref[step * ne + e] * LIN_SUB, 8)
                pltpu.make_async_copy(y_hbm.at[pl.ds(src, rows)], ybuf.at[s, pl.ds(dst, rows)], sem.at[s]).start()
            return carry

        lax.fori_loop(0, ne, run, 0)

    @pl.when(i == 0)
    def _():
        fetch(0, 0)

    @pl.when(i + 1 < nt)
    def _():
        fetch(i + 1, 1 - slot)

    pltpu.make_async_copy(y_hbm.at[pl.ds(0, rloc * LIN_SUB)], ybuf.at[slot], sem.at[slot]).wait()

    y = _lin_unpack([ybuf[slot, pl.ds(c, rloc, stride=LIN_SUB), :] for c in range(LIN_SUB)]).astype(BF16)
    loc = loc_ref[0]
    gate = gate_ref[0]
    tt = loc.shape[1]
    row_id = lax.broadcasted_iota(jnp.int32, (rloc, tt), 0)
    w = jnp.zeros((rloc, tt), F32)
    for k in range(TOP_K):
        w = w + jnp.where(row_id == loc[k:k + 1, :], gate[k:k + 1, :], 0.0)
    h = h_ref[...] + _dot_tn(w.astype(BF16), y)
    out_ref[...] = h * lax.rsqrt(jnp.mean(h * h, axis=-1, keepdims=True) + EPS) * g_ref[...]


def _combine(h, y_lin, loc, gates, tabs, norm_g, rloc):
    n, d = h.shape
    nt, _, tt = loc.shape
    kern = functools.partial(_combine_kernel, rloc=rloc)
    grid_spec = pltpu.PrefetchScalarGridSpec(
        num_scalar_prefetch=3,
        grid=(nt,),
        in_specs=[
            pl.BlockSpec((tt, d), lambda i, *_: (i, 0)),
            pl.BlockSpec((1, TOP_K, tt), lambda i, *_: (i, 0, 0)),
            pl.BlockSpec((1, TOP_K, tt), lambda i, *_: (i, 0, 0)),
            pl.BlockSpec((1, d), lambda i, *_: (0, 0)),
            pl.BlockSpec(memory_space=pl.ANY),
        ],
        out_specs=pl.BlockSpec((tt, d), lambda i, *_: (i, 0)),
        scratch_shapes=[
            pltpu.VMEM((2, rloc * LIN_SUB, LANES), F32),
            pltpu.SemaphoreType.DMA((2,)),
        ],
    )
    return pl.pallas_call(
        kern,
        grid_spec=grid_spec,
        out_shape=jax.ShapeDtypeStruct((n, d), F32),
        compiler_params=_cparams(("arbitrary",)),
        name="moe_combine",
    )(tabs['off'], tabs['len'], tabs['dst'], h, loc, gates, norm_g.reshape(1, d), y_lin)


def _routing_tables(cum, counts, bm, nblk):
    padded = (counts + bm - 1) // bm * bm
    pend = jnp.cumsum(padded)
    pstart = pend - padded
    blk = jnp.arange(nblk, dtype=jnp.int32)
    bexp_raw = jnp.sum((pend // bm)[None, :] <= blk[:, None], axis=1).astype(jnp.int32)
    nact = (pend[-1] // bm).astype(jnp.int32)
    last = jnp.minimum(bexp_raw[jnp.maximum(nact - 1, 0)], N_EXPERTS - 1)
    bexp = jnp.where(blk < nact, jnp.minimum(bexp_raw, N_EXPERTS - 1), last)
    prev = jnp.concatenate([jnp.full((1,), -1, jnp.int32), bexp[:-1]])
    first = jnp.logical_and(bexp != prev, blk < nact)
    later = jnp.logical_and(first[None, :], blk[None, :] > blk[:, None])
    nxt_blk = jnp.min(jnp.where(later, blk[None, :], nblk), axis=1)
    nxt = jnp.where(nxt_blk < nblk, bexp[jnp.minimum(nxt_blk, nblk - 1)], -1)
    cum_full = jnp.concatenate([cum, counts[None, :]], axis=0)
    run_len = cum_full[1:] - cum_full[:-1]
    run_off = jnp.cumsum(run_len, axis=1) - run_len
    tabs = {
        'off': run_off.reshape(-1), 'len': run_len.reshape(-1),
        'dst': (pstart[None, :] + cum_full[:-1]).reshape(-1),
        'zdst': pstart + counts, 'zlen': padded - counts,
    }
    blocks = (bexp, nact.reshape(1), first.astype(jnp.int32), nxt.astype(jnp.int32))
    return {k: v.astype(jnp.int32) for k, v in tabs.items()}, blocks


MOE_BLOCK_ROWS = 256
TOKEN_TILE = 256


def kernel(x, norm_mix_g, w_in, gla_gate_up, gla_gate_bias, gla_norm_g, w_branch_gla, w_branch_att, w_mix_out,
           norm_ffn_g, router_w, router_b, expert_w_gate_up, expert_b_gate_up, expert_w_down, expert_b_down,
           norm_final_g):
    batch, seq, d = x.shape
    assert norm_mix_g.shape[0] == 1, "single layer"
    n = batch * seq
    gla_k = gla_gate_up.shape[2]
    gla_v = w_branch_gla.shape[1]
    att_w = len(ATT_GROUPS) * ATT_HEADS_PER_GROUP * ATT_HEAD_DIM
    hw = ATT_HEADS_PER_GROUP * ATT_HEAD_DIM
    ngrp = len(ATT_GROUPS)
    assert att_w == ngrp * hw and all(s % hw == 0 for s in (gla_k, gla_v, d))
    assert ATT_GROUPS[0][1] == 1 and all(win // dil == ATT_BLOCK for win, dil in ATT_GROUPS)
    src, off = {}, 0
    for name, size in (('gq', gla_k), ('gk', gla_k), ('gv', gla_v), ('gr', gla_v), ('glr', GLA_GATE_RANK),
                       ('aq', att_w), ('ak', att_w), ('av', att_w), ('gate_gla', d), ('gate_att', d)):
        src[name] = (off, size)
        off += size
    assert off == w_in.shape[2]
    wt = w_in[0].T
    glr_at, glr_n = src['glr']
    assert glr_n == GLA_GATE_RANK

    def col(name, group=0):
        return src[name][0] + group * hw

    plain = (('gq', gla_k), ('gk', gla_k), ('gv', gla_v), ('gr', gla_v), ('gate_gla', d), ('gate_att', d),
             ('aq', hw), ('ak', hw), ('av', hw))
    layout, tile_cols = {}, []
    for name, size in plain:
        layout[name] = len(tile_cols) * hw
        tile_cols += [col(name) + t * hw for t in range(size // hw)]
    n_plain = len(tile_cols)
    tile_cols += [col(kind, g) for g in range(1, ngrp) for kind in ('aq', 'ak', 'av')]
    dilations = [dil for _, dil in ATT_GROUPS[1:]]

    x2 = x.reshape(n, d)
    proj, glr, *qkv_dil = _in_projection(x2, norm_mix_g[0], wt, glr_at, batch, seq, n_plain, dilations,
                                         tuple(tile_cols))
    y_gla = _gla(proj, glr, gla_gate_up[0], gla_gate_bias[0], gla_norm_g[0], batch, seq, layout)
    att = [_dilated_attention(proj.reshape(batch, 1, seq, proj.shape[1]), 1, layout['aq'])]
    att += [_dilated_attention(a, dil, 0) for a, dil in zip(qkv_dil, dilations)]
    tt = min(TOKEN_TILE, n)
    h, hn, gates, loc, cum, cnt = _merge(
        x2, y_gla, att, proj, layout, w_branch_gla[0].astype(BF16), w_branch_att[0].astype(BF16),
        w_mix_out[0].astype(BF16), norm_ffn_g[0], router_w[0], router_b[0], seq, tt)

    bm = MOE_BLOCK_ROWS
    rloc = TOP_K * tt
    nblk = -(-(TOP_K * n) // bm) + N_EXPERTS
    tabs, blocks = _routing_tables(cum[:, :, 0], cnt[:, 0], bm, nblk)
    x_lin = _dispatch(hn, loc, tabs, blocks[1], nblk * bm, rloc)
    y_lin = _experts(x_lin, *blocks, expert_w_gate_up[0], expert_b_gate_up[0], expert_w_down[0],
                     expert_b_down[0], bm)
    out = _combine(h, y_lin, loc, gates, tabs, norm_final_g, rloc)
    return out.reshape(batch, seq, d)
```

```python
import functools
import math

import jax
import jax.numpy as jnp
import numpy as np
from jax import lax
from jax.experimental import pallas as pl
from jax.experimental.pallas import tpu as pltpu

GLA_HEADS = 4
GLA_GATE_RANK = 16
GLA_TAU = 16.0
GLA_CHUNK = 64
ATT_GROUPS = ((128, 1), (512, 4), (2048, 16))
ATT_HEADS_PER_GROUP = 8
ATT_HEAD_DIM = 64
ATT_BLOCK = 128
ROT_DIM = ATT_HEAD_DIM // 4
ROPE_THETA = 500000.0
N_EXPERTS = 32
TOP_K = 4
SWIGLU_LIMIT = 7.0
SWIGLU_ALPHA = 1.702
EPS = 1e-5

LANES = 128
MXU_DIM = 256
VMEM_LIMIT = 56 * 1024 * 1024

F32 = jnp.float32
BF16 = jnp.bfloat16


def _cparams(sem):
    return pltpu.CompilerParams(dimension_semantics=sem, vmem_limit_bytes=VMEM_LIMIT)


def _dot(a, b):
    return jnp.dot(a, b, preferred_element_type=F32)


def _dot_nt(a, b):
    return lax.dot_general(a, b, (((1,), (1,)), ((), ())), preferred_element_type=F32)


def _dot_tn(a, b):
    return lax.dot_general(a, b, (((0,), (0,)), ((), ())), preferred_element_type=F32)


def _split2(a):
    hi = a.astype(BF16)
    lo = (a - hi.astype(F32)).astype(BF16)
    return hi, lo


def _inproj_kernel(x_ref, g_ref, wt_hbm, wglr_ref, cos_ref, sa_ref, sb_ref, proj_ref, glr_ref, *rest,
                   n_plain, dilations, q_scale, tn, tile_rows):
    dil_refs, (xn_ref, stage_ref, w_ref, wst_ref, wsem) = rest[:len(dilations)], rest[len(dilations):]
    nslab = tn // LANES

    @pl.when(pl.program_id(0) == 0)
    def _():
        def window(j):
            return pltpu.make_async_copy(wt_hbm.at[pl.ds(tile_rows[j], tn)], wst_ref.at[j % 2], wsem.at[j % 2])

        window(0).start()
        for j in range(len(tile_rows)):
            if j + 1 < len(tile_rows):
                window(j + 1).start()
            window(j).wait()
            for s in range(nslab):
                w_ref[:, j * tn + s * LANES:j * tn + (s + 1) * LANES] = \
                    jnp.transpose(wst_ref[j % 2, s * LANES:(s + 1) * LANES, :]).astype(BF16)

    x = x_ref[...]
    y = x * lax.rsqrt(jnp.mean(x * x, axis=-1, keepdims=True) + EPS) * g_ref[...]
    xn_ref[...] = y.astype(BF16)
    yh, yl = _split2(y)
    wh, wl = _split2(wglr_ref[...])
    glr_ref[...] = _dot_nt(yh, wh) + _dot_nt(yh, wl) + _dot_nt(yl, wh)

    tm = xn_ref.shape[0]
    half = ROT_DIM // 2

    def slabs(j, kind):
        acc = _dot(xn_ref[...], w_ref[:, j * tn:(j + 1) * tn])
        for s in range(nslab):
            t = acc[:, s * LANES:(s + 1) * LANES]
            if kind < 2:
                t = t * cos_ref[...] + pltpu.roll(t, LANES - half, 1) * sa_ref[...] \
                    + pltpu.roll(t, half, 1) * sb_ref[...]
                if kind == 0:
                    t = t * q_scale
            yield s, t

    for gi, dil in enumerate(dilations):
        for kind in range(3):
            j = n_plain + 3 * gi + kind
            buf = (3 * gi + kind) % stage_ref.shape[0]
            for s, t in slabs(j, kind):
                stage_ref[buf, s] = t
            for r in range(dil):
                for s in range(nslab):
                    dil_refs[gi][0, r, :, kind * tn + s * LANES:kind * tn + (s + 1) * LANES] = \
                        stage_ref[buf, s, pl.ds(r, tm // dil, stride=dil), :].astype(BF16)

    q0 = n_plain - 3
    for j in range(n_plain):
        for s, t in slabs(j, j - q0 if j >= q0 else 2):
            proj_ref[:, j * tn + s * LANES:j * tn + (s + 1) * LANES] = t.astype(BF16)


def _rope_tables(seq):
    half = ROT_DIM // 2
    inv_freq = ROPE_THETA ** (-np.arange(half, dtype=np.float32) * np.float32(2.0 / ROT_DIM))
    pos = jnp.arange(seq, dtype=F32)
    ang = pos[:, None] * jnp.asarray(inv_freq, F32)[None, :]
    cos = jnp.cos(ang)
    sin = jnp.sin(ang)
    lane = np.arange(LANES) % ATT_HEAD_DIM
    idx = np.where(lane < ROT_DIM, lane % half, 0)
    in_rot = jnp.asarray(lane < ROT_DIM)
    first = jnp.asarray(lane < half)
    second = jnp.asarray((lane >= half) & (lane < ROT_DIM))
    cos_t = jnp.where(in_rot[None, :], cos[:, idx], 1.0)
    sin_g = sin[:, idx]
    sa = jnp.where(first[None, :], -sin_g, 0.0)
    sb = jnp.where(second[None, :], sin_g, 0.0)
    return cos_t.astype(F32), sa.astype(F32), sb.astype(F32)


def _in_projection(x2, norm_g, wt, glr_row, batch, seq, n_plain, dilations, tile_rows):
    n, d = x2.shape
    tn = ATT_HEADS_PER_GROUP * ATT_HEAD_DIM
    tm = min(512, seq)
    ntile = len(tile_rows)
    assert n % tm == 0 and seq % tm == 0 and ntile == n_plain + 3 * len(dilations)
    assert all(tm % (dil * 16) == 0 for dil in dilations)
    assert glr_row % GLA_GATE_RANK == 0 and all(r % 8 == 0 and r + tn <= wt.shape[0] for r in tile_rows)
    cos_t, sa, sb = _rope_tables(seq)
    spt = seq // tm
    kern = functools.partial(_inproj_kernel, n_plain=n_plain, dilations=tuple(dilations),
                             q_scale=ATT_HEAD_DIM ** -0.5, tn=tn, tile_rows=tuple(tile_rows))
    out_specs = [
        pl.BlockSpec((tm, n_plain * tn), lambda i: (i, 0)),
        pl.BlockSpec((tm, GLA_GATE_RANK), lambda i: (i, 0)),
    ]
    out_shape = [
        jax.ShapeDtypeStruct((n, n_plain * tn), BF16),
        jax.ShapeDtypeStruct((n, GLA_GATE_RANK), F32),
    ]
    for dil in dilations:
        out_specs.append(pl.BlockSpec((1, dil, tm // dil, 3 * tn), lambda i: (i // spt, 0, i % spt, 0)))
        out_shape.append(jax.ShapeDtypeStruct((batch, dil, seq // dil, 3 * tn), BF16))
    const = lambda i: (0, 0)
    return pl.pallas_call(
        kern,
        grid=(n // tm,),
        in_specs=[
            pl.BlockSpec((tm, d), lambda i: (i, 0)),
            pl.BlockSpec((1, d), const),
            pl.BlockSpec(memory_space=pl.ANY),
            pl.BlockSpec((GLA_GATE_RANK, d), lambda i: (glr_row // GLA_GATE_RANK, 0)),
            pl.BlockSpec((tm, LANES), lambda i: (i % spt, 0)),
            pl.BlockSpec((tm, LANES), lambda i: (i % spt, 0)),
            pl.BlockSpec((tm, LANES), lambda i: (i % spt, 0)),
        ],
        out_specs=out_specs,
        out_shape=out_shape,
        scratch_shapes=[pltpu.VMEM((tm, d), BF16), pltpu.VMEM((2, tn // LANES, tm, LANES), F32),
                        pltpu.VMEM((d, ntile * tn), BF16),
                        pltpu.VMEM((2, tn, d), F32), pltpu.SemaphoreType.DMA((2,))],
        compiler_params=_cparams(("arbitrary",)),
        name="in_projection",
    )(x2, norm_g.reshape(1, d), wt, wt, cos_t, sa, sb)


def _gla_kernel(q_ref, k_ref, v_ref, r_ref, glr_ref, up_ref, bias_ref, ng_ref, tri_ref, y_ref, state_ref,
                *, dk, dv, ts):
    t = pl.program_id(2)

    @pl.when(t == 0)
    def _():
        state_ref[...] = jnp.zeros_like(state_ref)

    c = GLA_CHUNK
    hps = state_ref.shape[0]
    gh, gl = _split2(glr_ref[...])
    uh, ul = _split2(up_ref[...])
    z = _dot(gh, uh) + _dot(gh, ul) + _dot(gl, uh) + bias_ref[...]
    logdec = (jnp.minimum(z, 0.0) - jnp.log1p(jnp.exp(-jnp.abs(z)))) * (1.0 / GLA_TAU)
    ldh, ldl = _split2(logdec)
    tri = tri_ref[...]
    row = lax.broadcasted_iota(jnp.int32, (c, c), 0)
    col = lax.broadcasted_iota(jnp.int32, (c, c), 1)
    causal = col <= row
    scale = dk ** -0.5
    ng = ng_ref[...]
    states = [state_ref[hh] for hh in range(hps)]
    for ci in range(ts // c):
        sl = slice(ci * c, (ci + 1) * c)
        b2 = _dot(tri, ldh[sl]) + _dot(tri, ldl[sl])
        for hh in range(hps):
            ks = slice(hh * dk, (hh + 1) * dk)
            vs = slice(hh * dv, (hh + 1) * dv)
            b = b2[:, ks]
            bl = b[c - 1:c, :]
            eb = jnp.exp(b)
            enb = jnp.exp(-b)
            ebl = jnp.exp(bl)
            qf = q_ref[sl, ks].astype(F32)
            kf = k_ref[sl, ks].astype(F32)
            qe = (qf * scale * eb).astype(BF16)
            ke = (kf * enb).astype(BF16)
            kd = (kf * enb * ebl).astype(BF16)
            vb = v_ref[sl, vs]
            a = jnp.where(causal, _dot_nt(qe, ke), 0.0).astype(BF16)
            o = _dot(a, vb) + _dot(qe, states[hh].astype(BF16))
            dec = jnp.transpose(jnp.broadcast_to(ebl, (dk, dk)))
            dec_full = jnp.concatenate([dec] * (dv // dk), axis=1)
            states[hh] = dec_full * states[hh] + _dot_tn(kd, vb)
            o = o * lax.rsqrt(jnp.mean(o * o, axis=-1, keepdims=True) + EPS) * ng
            rf = r_ref[sl, vs].astype(F32)
            o = o * (rf * jax.nn.sigmoid(rf))
            y_ref[sl, vs] = o.astype(BF16)
    for hh in range(hps):
        state_ref[hh] = states[hh]


def _gla(proj, glr, gate_up, gate_bias, norm_g, batch, seq, layout):
    n = proj.shape[0]
    dk = (layout['gk'] - layout['gq']) // GLA_HEADS
    dv = (layout['gr'] - layout['gv']) // GLA_HEADS
    ts = min(512, seq)
    assert dk == LANES and dv % dk == 0 and seq % ts == 0 and ts % GLA_CHUNK == 0
    spt = seq // ts
    hps = GLA_HEADS
    wk, wv = hps * dk, hps * dv
    assert GLA_HEADS % hps == 0 and all(layout[s] % wk == 0 for s in ('gq', 'gk'))
    assert all(layout[s] % wv == 0 for s in ('gv', 'gr'))
    qb, kb = layout['gq'] // wk, layout['gk'] // wk
    vb, rb = layout['gv'] // wv, layout['gr'] // wv
    tri = jnp.asarray(np.tril(np.ones((GLA_CHUNK, GLA_CHUNK), np.float32)), BF16)
    kern = functools.partial(_gla_kernel, dk=dk, dv=dv, ts=ts)
    row = lambda b, h, t: b * spt + t
    return pl.pallas_call(
        kern,
        grid=(batch, GLA_HEADS // hps, spt),
        in_specs=[
            pl.BlockSpec((ts, wk), lambda b, h, t: (row(b, h, t), qb + h)),
            pl.BlockSpec((ts, wk), lambda b, h, t: (row(b, h, t), kb + h)),
            pl.BlockSpec((ts, wv), lambda b, h, t: (row(b, h, t), vb + h)),
            pl.BlockSpec((ts, wv), lambda b, h, t: (row(b, h, t), rb + h)),
            pl.BlockSpec((ts, GLA_GATE_RANK), lambda b, h, t: (row(b, h, t), 0)),
            pl.BlockSpec((GLA_GATE_RANK, wk), lambda b, h, t: (0, h)),
            pl.BlockSpec((1, wk), lambda b, h, t: (0, h)),
            pl.BlockSpec((1, dv), lambda b, h, t: (0, 0)),
            pl.BlockSpec((GLA_CHUNK, GLA_CHUNK), lambda b, h, t: (0, 0)),
        ],
        out_specs=pl.BlockSpec((ts, wv), lambda b, h, t: (row(b, h, t), h)),
        out_shape=jax.ShapeDtypeStruct((n, GLA_HEADS * dv), BF16),
        scratch_shapes=[pltpu.VMEM((hps, dk, dv), F32)],
        compiler_params=_cparams(("arbitrary", "arbitrary", "arbitrary")),
        name="gla",
    )(proj, proj, proj, proj, glr, gate_up, gate_bias.reshape(1, -1), norm_g.reshape(1, -1), tri)


LSE_REP = LANES // ATT_HEADS_PER_GROUP


def _att_kernel(q_ref, k_ref, v_ref, o_ref, lse_ref, *, nb, pairs, unroll):
    blk = ATT_BLOCK
    hd = ATT_HEAD_DIM
    row = lax.broadcasted_iota(jnp.int32, (2 * blk, 2 * blk), 0) % blk
    col = lax.broadcasted_iota(jnp.int32, (2 * blk, 2 * blk), 1)
    dist = row + blk - col
    band = jnp.logical_and(dist >= 0, dist <= blk)
    causal = band[:, blk:]
    lane = lax.broadcasted_iota(jnp.int32, (blk, LANES), 1)
    head0 = lane < hd
    lane_head = lane // LSE_REP
    neg = -jnp.inf

    def block(j, first):
        r0 = pl.multiple_of(j * blk, blk)
        lse = jnp.zeros((blk, LANES), F32)
        for p in range(pairs):
            cs = slice(p * LANES, (p + 1) * LANES)
            q2 = q_ref[0, 0, pl.ds(r0, blk), cs]
            zero = jnp.zeros_like(q2)
            qs = jnp.concatenate([jnp.where(head0, q2, zero), jnp.where(head0, zero, q2)], axis=0)
            if first:
                kk = k_ref[0, 0, pl.ds(r0, blk), cs]
                vv = v_ref[0, 0, pl.ds(r0, blk), cs]
                s = jnp.where(causal, _dot_nt(qs, kk), neg)
            else:
                k0 = pl.multiple_of(r0 - blk, blk)
                kk = k_ref[0, 0, pl.ds(k0, 2 * blk), cs]
                vv = v_ref[0, 0, pl.ds(k0, 2 * blk), cs]
                s = jnp.where(band, _dot_nt(qs, kk), neg)
            mx = jnp.max(s, axis=-1, keepdims=True)
            e = jnp.exp(s - mx).astype(BF16)
            one = jnp.ones_like(vv)
            hv = lax.broadcasted_iota(jnp.int32, vv.shape, 1) < hd
            out0 = _dot(e[:blk], jnp.where(hv, vv, one))
            out1 = _dot(e[blk:], jnp.where(hv, one, vv))
            num = jnp.where(head0, out0, out1)
            den_x = jnp.where(head0, out1, out0)
            den = pltpu.roll(den_x, hd, 1)
            o_ref[0, 0, pl.ds(r0, blk), cs] = (num / den).astype(o_ref.dtype)
            lse0 = mx[:blk] + jnp.log(jnp.where(head0, den, den_x))
            lse1 = mx[blk:] + jnp.log(jnp.where(head0, den_x, den))
            lse = jnp.where(lane_head == 2 * p, lse0, jnp.where(lane_head == 2 * p + 1, lse1, lse))
        lse_ref[0, 0, pl.ds(r0, blk), :] = lse

    block(0, True)

    def body(j, carry):
        block(j, False)
        return carry

    lax.fori_loop(1, nb, body, 0, unroll=unroll)


def _dilated_attention(qkv, dilation, col0):
    batch, dil, sub, width = qkv.shape
    hw = ATT_HEADS_PER_GROUP * ATT_HEAD_DIM
    assert dil == dilation and sub % ATT_BLOCK == 0 and col0 % hw == 0
    nb = sub // ATT_BLOCK
    pairs = hw // LANES
    assert 2 * pairs * LSE_REP == LANES
    kern = functools.partial(_att_kernel, nb=nb, pairs=pairs, unroll=1)

    def in_spec(kind):
        base = (col0 + kind * hw) // hw
        return pl.BlockSpec((1, 1, sub, hw), lambda b, r: (b, r, 0, base))

    return pl.pallas_call(
        kern,
        grid=(batch, dil),
        in_specs=[in_spec(0), in_spec(1), in_spec(2)],
        out_specs=[pl.BlockSpec((1, 1, sub, hw), lambda b, r: (b, r, 0, 0)),
                   pl.BlockSpec((1, 1, sub, LANES), lambda b, r: (b, r, 0, 0))],
        out_shape=[
            jax.ShapeDtypeStruct((batch, dil, sub, hw), BF16),
            jax.ShapeDtypeStruct((batch, dil, sub, LANES), F32),
        ],
        compiler_params=_cparams(("arbitrary", "arbitrary")),
        name=f"dilated_attention_d{dilation}",
    )(qkv, qkv, qkv)


def _merge_kernel(x_ref, yg_ref, o1_ref, o2_ref, o3_ref, l1_ref, l2_ref, l3_ref,
                  gg0_ref, gg1_ref, ga0_ref, ga1_ref, wbg_ref, wba_ref, wmo_ref, nf_ref, rw_ref, rb_ref,
                  spread_ref, h_ref, hn_ref, lg_ref, os_ref, ls_ref):
    def token_order(src_ref, stage_ref):
        dil, sub, w = src_ref.shape[1:]
        if dil == 1:
            return src_ref[0, 0].astype(F32)
        for r in range(dil):
            val = src_ref[0, r].astype(F32)
            for s in range(w // LANES):
                stage_ref[s, pl.ds(r, sub, stride=dil), :] = val[:, s * LANES:(s + 1) * LANES]
        return jnp.concatenate([stage_ref[s] for s in range(w // LANES)], axis=1)

    l1, o1 = token_order(l1_ref, None), token_order(o1_ref, None)
    l2, o2 = token_order(l2_ref, ls_ref.at[0]), token_order(o2_ref, os_ref.at[0])
    l3, o3 = token_order(l3_ref, ls_ref.at[1]), token_order(o3_ref, os_ref.at[1])
    m = jnp.maximum(jnp.maximum(l1, l2), l3)
    w1, w2, w3 = jnp.exp(l1 - m), jnp.exp(l2 - m), jnp.exp(l3 - m)
    inv = 1.0 / (w1 + w2 + w3)
    spread = spread_ref[...]

    def per_lane(w):
        hi, lo = _split2(w * inv)
        return _dot(hi, spread) + _dot(lo, spread)

    y_att = per_lane(w1) * o1 + per_lane(w2) * o2 + per_lane(w3) * o3

    m_gla = _dot(yg_ref[...], wbg_ref[...])
    m_att = _dot(y_att.astype(BF16), wba_ref[...])
    gg = jnp.concatenate([gg0_ref[...], gg1_ref[...]], axis=1).astype(F32)
    ga = jnp.concatenate([ga0_ref[...], ga1_ref[...]], axis=1).astype(F32)
    merged = (0.5 + 0.5 * jnp.tanh(0.5 * gg)) * m_gla + (0.5 + 0.5 * jnp.tanh(0.5 * ga)) * m_att
    h = x_ref[...] + _dot(merged.astype(BF16), wmo_ref[...])
    h_ref[...] = h
    hn = h * lax.rsqrt(jnp.mean(h * h, axis=-1, keepdims=True) + EPS) * nf_ref[...]
    hn_ref[...] = hn.astype(BF16)

    hh, hl = _split2(hn)
    rh, rl = _split2(rw_ref[...])
    lg_ref[...] = _dot_nt(rh, hh) + _dot_nt(rh, hl) + _dot_nt(rl, hh) + rb_ref[...]


def _route_kernel(lg_ref, triu_ref, ltri_ref, tg_ref, lc_ref, cum_ref, cnt_ref, carry_ref):
    ne = N_EXPERTS
    tiles, _, tt = tg_ref.shape

    @pl.when(pl.program_id(0) == 0)
    def _():
        carry_ref[...] = jnp.zeros_like(carry_ref)

    eid = lax.broadcasted_iota(jnp.int32, (ne, tt), 0)
    carry = carry_ref[...]
    for t in range(tiles):
        logit = lg_ref[:, t * tt:(t + 1) * tt]
        member = jnp.zeros((ne, tt), jnp.bool_)
        vals, idxs = [], []
        for _ in range(TOP_K):
            mx = jnp.max(logit, axis=0, keepdims=True)
            idx = jnp.min(jnp.where(logit == mx, eid, ne), axis=0, keepdims=True)
            sel = eid == idx
            member = jnp.logical_or(member, sel)
            logit = jnp.where(sel, -jnp.inf, logit)
            vals.append(mx)
            idxs.append(idx)
        ex = [jnp.exp(v - vals[0]) for v in vals]
        tot = ex[0] + ex[1] + ex[2] + ex[3]
        memf = jnp.where(member, 1.0, 0.0)
        lens = jnp.sum(memf, axis=1, keepdims=True)
        lens_b = jnp.broadcast_to(lens, (ne, LANES))
        off = _dot(ltri_ref[...], lens_b.astype(BF16))
        pos = _dot(memf.astype(BF16), triu_ref[...]) + off[:, 0:1]
        for k in range(TOP_K):
            tg_ref[t, k:k + 1, :] = ex[k] / tot
            lc_ref[t, k:k + 1, :] = jnp.sum(jnp.where(eid == idxs[k], pos, 0.0), axis=0,
                                            keepdims=True).astype(jnp.int32)
        cum_ref[t] = carry.astype(jnp.int32)
        carry = carry + lens_b
    carry_ref[...] = carry
    cnt_ref[...] = carry.astype(jnp.int32)


def _route(logits, tt):
    ne, n = logits.shape
    nt = n // tt
    assert tt <= 256, "per-tile expert counts go through a bf16 matmul operand: exact up to 256"
    tiles = 8 if nt % 8 == 0 else 1
    triu = jnp.asarray(np.triu(np.ones((tt, tt), np.float32), 1), BF16)
    ltri = jnp.asarray(np.tril(np.ones((ne, ne), np.float32), -1), BF16)
    const = lambda i: (0, 0)
    return pl.pallas_call(
        _route_kernel,
        grid=(nt // tiles,),
        in_specs=[pl.BlockSpec((ne, tiles * tt), lambda i: (0, i)),
                  pl.BlockSpec((tt, tt), const), pl.BlockSpec((ne, ne), const)],
        out_specs=[
            pl.BlockSpec((tiles, TOP_K, tt), lambda i: (i, 0, 0)),
            pl.BlockSpec((tiles, TOP_K, tt), lambda i: (i, 0, 0)),
            pl.BlockSpec((tiles, ne, LANES), lambda i: (i, 0, 0)),
            pl.BlockSpec((ne, LANES), const),
        ],
        out_shape=[
            jax.ShapeDtypeStruct((nt, TOP_K, tt), F32),
            jax.ShapeDtypeStruct((nt, TOP_K, tt), jnp.int32),
            jax.ShapeDtypeStruct((nt, ne, LANES), jnp.int32),
            jax.ShapeDtypeStruct((ne, LANES), jnp.int32),
        ],
        scratch_shapes=[pltpu.VMEM((ne, LANES), F32)],
        compiler_params=_cparams(("arbitrary",)),
        name="route_topk",
    )(logits, triu, ltri)


def _merge(x2, y_gla, att, proj, layout, wbg, wba, wmo, norm_ffn_g, router_w, router_b, seq, tm):
    n, d = x2.shape
    hw = ATT_HEADS_PER_GROUP * ATT_HEAD_DIM
    half = d // 2
    assert n % tm == 0 and seq % tm == 0
    assert layout['gate_gla'] % half == 0 and layout['gate_att'] % half == 0
    assert all(tm % (o.shape[1] * 16) == 0 for o, _ in att) and att[0][0].shape[1] == 1
    nt = n // tm
    spt = seq // tm
    ggb = layout['gate_gla'] // half
    gab = layout['gate_att'] // half
    row = lambda i: (i, 0)
    const = lambda i: (0, 0)

    def att_spec(a):
        dil = a.shape[1]
        return pl.BlockSpec((1, dil, tm // dil, a.shape[3]), lambda i: (i // spt, 0, i % spt, 0))

    (o1, l1), (o2, l2), (o3, l3) = att
    in_specs = [
        pl.BlockSpec((tm, d), row),
        pl.BlockSpec((tm, y_gla.shape[1]), row),
        att_spec(o1), att_spec(o2), att_spec(o3), att_spec(l1), att_spec(l2), att_spec(l3),
        pl.BlockSpec((tm, half), lambda i: (i, ggb)), pl.BlockSpec((tm, half), lambda i: (i, ggb + 1)),
        pl.BlockSpec((tm, half), lambda i: (i, gab)), pl.BlockSpec((tm, half), lambda i: (i, gab + 1)),
        pl.BlockSpec(wbg.shape, const), pl.BlockSpec(wba.shape, const), pl.BlockSpec(wmo.shape, const),
        pl.BlockSpec((1, d), const),
        pl.BlockSpec((N_EXPERTS, d), const),
        pl.BlockSpec((N_EXPERTS, 1), const),
        pl.BlockSpec((LANES, hw), const),
    ]
    out_specs = [
        pl.BlockSpec((tm, d), row),
        pl.BlockSpec((tm, d), row),
        pl.BlockSpec((N_EXPERTS, tm), lambda i: (0, i)),
    ]
    out_shape = [
        jax.ShapeDtypeStruct((n, d), F32),
        jax.ShapeDtypeStruct((n, d), BF16),
        jax.ShapeDtypeStruct((N_EXPERTS, n), F32),
    ]
    spread = jnp.asarray(np.arange(LANES)[:, None] == (np.arange(hw)[None, :] // ATT_HEAD_DIM) * LSE_REP, BF16)
    return pl.pallas_call(
        _merge_kernel,
        grid=(nt,),
        in_specs=in_specs,
        out_specs=out_specs,
        out_shape=out_shape,
        scratch_shapes=[pltpu.VMEM((2, hw // LANES, tm, LANES), F32), pltpu.VMEM((2, 1, tm, LANES), F32)],
        compiler_params=_cparams(("arbitrary",)),
        name="merge_router",
    )(x2, y_gla, o1, o2, o3, l1, l2, l3, proj, proj, proj, proj, wbg, wba, wmo,
      norm_ffn_g.reshape(1, d), router_w.T, router_b.reshape(N_EXPERTS, 1), spread)


LIN_SUB = 8


def _lin_pack(val):
    return [val[:, c * LANES:(c + 1) * LANES] for c in range(LIN_SUB)]


def _lin_unpack(slabs):
    return jnp.concatenate(slabs, axis=1)


def _dispatch_kernel(off_ref, len_ref, dst_ref, zdst_ref, zlen_ref, nact_ref, hn_ref, loc_ref, x_hbm,
                     xs_ref, zero_ref, sem, zsem, *, rloc):
    i = pl.program_id(0)
    nt = pl.num_programs(0)
    slot = i % 2
    ne = N_EXPERTS

    def drain(s):
        rows = rloc * LIN_SUB
        pltpu.make_async_copy(xs_ref.at[s], x_hbm.at[pl.ds(0, rows)], sem.at[s]).wait()

    @pl.when(i == 0)
    def _():
        zero_ref[...] = jnp.zeros_like(zero_ref)

        def fill_copy(e):
            rows = zlen_ref[e] * LIN_SUB
            dst = pl.multiple_of(zdst_ref[e] * LIN_SUB, 8)
            return rows, pltpu.make_async_copy(zero_ref.at[pl.ds(0, rows)], x_hbm.at[pl.ds(dst, rows)], zsem)

        def fill(e, carry):
            rows, copy = fill_copy(e)
            pl.when(rows > 0)(copy.start)
            return carry

        def fill_done(e, carry):
            rows, copy = fill_copy(e)
            pl.when(rows > 0)(copy.wait)
            return carry

        def block_copy(b):
            rows = zero_ref.shape[0]
            return pltpu.make_async_copy(zero_ref, x_hbm.at[pl.ds(pl.multiple_of(b * rows, rows), rows)], zsem)

        def fill_block(b, carry):
            block_copy(b).start()
            return carry

        def fill_block_done(b, carry):
            block_copy(b).wait()
            return carry

        nblk = x_hbm.shape[0] // zero_ref.shape[0]
        lax.fori_loop(0, ne, fill, 0)
        lax.fori_loop(nact_ref[0], nblk, fill_block, 0)
        lax.fori_loop(0, ne, fill_done, 0)
        lax.fori_loop(nact_ref[0], nblk, fill_block_done, 0)

    @pl.when(i >= 2)
    def _():
        drain(slot)

    loc = loc_ref[0]
    tt = loc.shape[1]
    row_id = lax.broadcasted_iota(jnp.int32, (rloc, tt), 0)
    hit = row_id == loc[0:1, :]
    for k in range(1, TOP_K):
        hit = jnp.logical_or(hit, row_id == loc[k:k + 1, :])
    onehot = jnp.where(hit, 1.0, 0.0).astype(BF16)
    xs = _dot(onehot, hn_ref[...])
    for c, slab in enumerate(_lin_pack(xs)):
        xs_ref[slot, pl.ds(c, rloc, stride=LIN_SUB), :] = slab

    def send(e, carry):
        rows = len_ref[i * ne + e] * LIN_SUB

        @pl.when(rows > 0)
        def _():
            src = pl.multiple_of(off_ref[i * ne + e] * LIN_SUB, 8)
            dst = pl.multiple_of(dst_ref[i * ne + e] * LIN_SUB, 8)
            pltpu.make_async_copy(xs_ref.at[slot, pl.ds(src, rows)], x_hbm.at[pl.ds(dst, rows)], sem.at[slot]).start()
        return carry

    lax.fori_loop(0, ne, send, 0)

    @pl.when(i == nt - 1)
    def _():
        @pl.when(i >= 1)
        def _():
            drain(1 - slot)
        drain(slot)


def _dispatch(hn, loc, tabs, nact, p_rows, rloc):
    n, d = hn.shape
    nt, _, tt = loc.shape
    assert d == LIN_SUB * LANES and p_rows % MOE_BLOCK_ROWS == 0
    kern = functools.partial(_dispatch_kernel, rloc=rloc)
    grid_spec = pltpu.PrefetchScalarGridSpec(
        num_scalar_prefetch=6,
        grid=(nt,),
        in_specs=[
            pl.BlockSpec((tt, d), lambda i, *_: (i, 0)),
            pl.BlockSpec((1, TOP_K, tt), lambda i, *_: (i, 0, 0)),
        ],
        out_specs=pl.BlockSpec(memory_space=pl.ANY),
        scratch_shapes=[
            pltpu.VMEM((2, rloc * LIN_SUB, LANES), F32),
            pltpu.VMEM((MOE_BLOCK_ROWS * LIN_SUB, LANES), F32),
            pltpu.SemaphoreType.DMA((2,)),
            pltpu.SemaphoreType.DMA(()),
        ],
    )
    return pl.pallas_call(
        kern,
        grid_spec=grid_spec,
        out_shape=jax.ShapeDtypeStruct((p_rows * LIN_SUB, LANES), F32),
        compiler_params=_cparams(("arbitrary",)),
        name="moe_dispatch",
    )(tabs['off'], tabs['len'], tabs['dst'], tabs['zdst'], tabs['zlen'], nact, hn, loc)


def _expert_kernel(bexp_ref, nact_ref, first_ref, next_ref, x_ref, wgu_hbm, wd_hbm, bgu_ref, bd_ref, perm_ref,
                   y_ref, wgu_f, wd_f, wgu_s, wd_s, sem, *, bm):
    grp = MXU_DIM
    ff2 = wgu_s.shape[1]
    halfg = grp // 2
    brows = bm * LIN_SUB
    bps = x_ref.shape[0] // brows

    def fetch(ex):
        return (pltpu.make_async_copy(wgu_hbm.at[ex], wgu_f, sem.at[0]),
                pltpu.make_async_copy(wd_hbm.at[ex], wd_f, sem.at[1]))

    @pl.when(pl.program_id(0) == 0)
    def _():
        for copy in fetch(bexp_ref[0]):
            copy.start()

    def block(b, r0):
        e = bexp_ref[b]
        active = b < nact_ref[0]

        @pl.when(jnp.logical_and(active, first_ref[b] == 1))
        def _():
            for copy in fetch(e):
                copy.wait()
            perm = perm_ref[...]
            for g in range(ff2 // grp):
                cols = slice(g * grp, (g + 1) * grp)
                wgu_s[:, cols] = _dot(wgu_f[:, cols].astype(BF16), perm).astype(BF16)
            wd_s[...] = wd_f[...].astype(BF16)
            nxt = next_ref[b]

            @pl.when(nxt >= 0)
            def _():
                for copy in fetch(nxt):
                    copy.start()

        @pl.when(active)
        def _():
            x = _lin_unpack([x_ref[pl.ds(r0 + c, bm, stride=LIN_SUB), :] for c in range(LIN_SUB)]).astype(BF16)
            acts = []
            for g in range(ff2 // grp):
                cols = slice(g * grp, (g + 1) * grp)
                hg = _dot(x, wgu_s[:, cols]) + bgu_ref[e, :, cols]
                gate = jnp.minimum(hg[:, :halfg], SWIGLU_LIMIT)
                up = jnp.clip(hg[:, halfg:], -SWIGLU_LIMIT, SWIGLU_LIMIT)
                acts.append(((up + 1.0) * (gate * jax.nn.sigmoid(SWIGLU_ALPHA * gate))).astype(BF16))
            act = jnp.concatenate(acts, axis=1)
            y = _dot(act, wd_s[...]) + bd_ref[e]
            for c, slab in enumerate(_lin_pack(y)):
                y_ref[pl.ds(r0 + c, bm, stride=LIN_SUB), :] = slab

        @pl.when(jnp.logical_not(active))
        def _():
            y_ref[pl.ds(r0, brows), :] = jnp.zeros((brows, y_ref.shape[1]), y_ref.dtype)

    for sub in range(bps):
        block(pl.program_id(0) * bps + sub, sub * brows)


def _gate_up_permutation():
    grp = MXU_DIM
    halfg = grp // 2
    out = np.arange(grp)
    src = np.where(out < halfg, 2 * out, 2 * (out - halfg) + 1)
    p = np.zeros((grp, grp), np.float32)
    p[src, out] = 1.0
    return jnp.asarray(p, BF16)


def _experts(x_lin, bexp, nact, first, nxt, w_gate_up, b_gate_up, w_down, b_down, bm):
    ne, d, ff2 = w_gate_up.shape
    ff = ff2 // 2
    grp = MXU_DIM
    assert ff2 % grp == 0 and d == LIN_SUB * LANES
    nblk = x_lin.shape[0] // (bm * LIN_SUB)
    bps = max(b for b in (4, 2, 1) if nblk % b == 0)
    srows = bps * bm * LIN_SUB
    bgu = b_gate_up.reshape(ne, ff2 // grp, grp // 2, 2).transpose(0, 1, 3, 2).reshape(ne, 1, ff2)
    grid_spec = pltpu.PrefetchScalarGridSpec(
        num_scalar_prefetch=4,
        grid=(nblk // bps,),
        in_specs=[
            pl.BlockSpec((srows, LANES), lambda i, *_: (i, 0)),
            pl.BlockSpec(memory_space=pl.ANY),
            pl.BlockSpec(memory_space=pl.ANY),
            pl.BlockSpec(memory_space=pltpu.VMEM),
            pl.BlockSpec(memory_space=pltpu.VMEM),
            pl.BlockSpec((grp, grp), lambda i, *_: (0, 0)),
        ],
        out_specs=pl.BlockSpec((srows, LANES), lambda i, *_: (i, 0)),
        scratch_shapes=[pltpu.VMEM((d, ff2), F32), pltpu.VMEM((ff, d), F32),
                        pltpu.VMEM((d, ff2), BF16), pltpu.VMEM((ff, d), BF16),
                        pltpu.SemaphoreType.DMA((2,))],
    )
    return pl.pallas_call(
        functools.partial(_expert_kernel, bm=bm),
        grid_spec=grid_spec,
        out_shape=jax.ShapeDtypeStruct(x_lin.shape, F32),
        compiler_params=_cparams(("arbitrary",)),
        name="moe_experts",
    )(bexp, nact, first, nxt, x_lin, w_gate_up, w_down, bgu, b_down.reshape(ne, 1, d), _gate_up_permutation())


def _combine_kernel(off_ref, len_ref, dst_ref, h_ref, loc_ref, gate_ref, g_ref, y_hbm, out_ref,
                    ybuf, sem, *, rloc):
    i = pl.program_id(0)
    nt = pl.num_programs(0)
    slot = i % 2
    ne = N_EXPERTS

    def fetch(step, s):
        def run(e, carry):
            rows = len_ref[step * ne + e] * LIN_SUB

            @pl.when(rows > 0)
            def _():
                src = pl.multiple_of(dst_ref[step * ne + e] * LIN_SUB, 8)
                dst = pl.multiple_of(off_ref[step * ne + e] * LIN_SUB, 8)
                pltpu.make_async_copy(y_hbm.at[pl.ds(src, rows)], ybuf.at[s, pl.ds(dst, rows)], sem.at[s]).start()
            return carry

        lax.fori_loop(0, ne, run, 0)

    @pl.when(i == 0)
    def _():
        fetch(0, 0)

    @pl.when(i + 1 < nt)
    def _():
        fetch(i + 1, 1 - slot)

    pltpu.make_async_copy(y_hbm.at[pl.ds(0, rloc * LIN_SUB)], ybuf.at[slot], sem.at[slot]).wait()

    y = _lin_unpack([ybuf[slot, pl.ds(c, rloc, stride=LIN_SUB), :] for c in range(LIN_SUB)]).astype(BF16)
    loc = loc_ref[0]
    gate = gate_ref[0]
    tt = loc.shape[1]
    row_id = lax.broadcasted_iota(jnp.int32, (rloc, tt), 0)
    w = jnp.zeros((rloc, tt), F32)
    for k in range(TOP_K):
        w = w + jnp.where(row_id == loc[k:k + 1, :], gate[k:k + 1, :], 0.0)
    h = h_ref[...] + _dot_tn(w.astype(BF16), y)
    out_ref[...] = h * lax.rsqrt(jnp.mean(h * h, axis=-1, keepdims=True) + EPS) * g_ref[...]


def _combine(h, y_lin, loc, gates, tabs, norm_g, rloc):
    n, d = h.shape
    nt, _, tt = loc.shape
    kern = functools.partial(_combine_kernel, rloc=rloc)
    grid_spec = pltpu.PrefetchScalarGridSpec(
        num_scalar_prefetch=3,
        grid=(nt,),
        in_specs=[
            pl.BlockSpec((tt, d), lambda i, *_: (i, 0)),
            pl.BlockSpec((1, TOP_K, tt), lambda i, *_: (i, 0, 0)),
            pl.BlockSpec((1, TOP_K, tt), lambda i, *_: (i, 0, 0)),
            pl.BlockSpec((1, d), lambda i, *_: (0, 0)),
            pl.BlockSpec(memory_space=pl.ANY),
        ],
        out_specs=pl.BlockSpec((tt, d), lambda i, *_: (i, 0)),
        scratch_shapes=[
            pltpu.VMEM((2, rloc * LIN_SUB, LANES), F32),
            pltpu.SemaphoreType.DMA((2,)),
        ],
    )
    return pl.pallas_call(
        kern,
        grid_spec=grid_spec,
        out_shape=jax.ShapeDtypeStruct((n, d), F32),
        compiler_params=_cparams(("arbitrary",)),
        name="moe_combine",
    )(tabs['off'], tabs['len'], tabs['dst'], h, loc, gates, norm_g.reshape(1, d), y_lin)


def _routing_tables(cum, counts, bm, nblk):
    padded = (counts + bm - 1) // bm * bm
    pend = jnp.cumsum(padded)
    pstart = pend - padded
    blk = jnp.arange(nblk, dtype=jnp.int32)
    bexp_raw = jnp.sum((pend // bm)[None, :] <= blk[:, None], axis=1).astype(jnp.int32)
    nact = (pend[-1] // bm).astype(jnp.int32)
    last = jnp.minimum(bexp_raw[jnp.maximum(nact - 1, 0)], N_EXPERTS - 1)
    bexp = jnp.where(blk < nact, jnp.minimum(bexp_raw, N_EXPERTS - 1), last)
    prev = jnp.concatenate([jnp.full((1,), -1, jnp.int32), bexp[:-1]])
    first = jnp.logical_and(bexp != prev, blk < nact)
    later = jnp.logical_and(first[None, :], blk[None, :] > blk[:, None])
    nxt_blk = jnp.min(jnp.where(later, blk[None, :], nblk), axis=1)
    nxt = jnp.where(nxt_blk < nblk, bexp[jnp.minimum(nxt_blk, nblk - 1)], -1)
    cum_full = jnp.concatenate([cum, counts[None, :]], axis=0)
    run_len = cum_full[1:] - cum_full[:-1]
    run_off = jnp.cumsum(run_len, axis=1) - run_len
    tabs = {
        'off': run_off.reshape(-1), 'len': run_len.reshape(-1),
        'dst': (pstart[None, :] + cum_full[:-1]).reshape(-1),
        'zdst': pstart + counts, 'zlen': padded - counts,
    }
    blocks = (bexp, nact.reshape(1), first.astype(jnp.int32), nxt.astype(jnp.int32))
    return {k: v.astype(jnp.int32) for k, v in tabs.items()}, blocks


MOE_BLOCK_ROWS = 256
TOKEN_TILE = 256


def kernel(x, norm_mix_g, w_in, gla_gate_up, gla_gate_bias, gla_norm_g, w_branch_gla, w_branch_att, w_mix_out,
           norm_ffn_g, router_w, router_b, expert_w_gate_up, expert_b_gate_up, expert_w_down, expert_b_down,
           norm_final_g):
    batch, seq, d = x.shape
    assert norm_mix_g.shape[0] == 1, "single layer"
    n = batch * seq
    gla_k = gla_gate_up.shape[2]
    gla_v = w_branch_gla.shape[1]
    att_w = len(ATT_GROUPS) * ATT_HEADS_PER_GROUP * ATT_HEAD_DIM
    hw = ATT_HEADS_PER_GROUP * ATT_HEAD_DIM
    ngrp = len(ATT_GROUPS)
    assert att_w == ngrp * hw and all(s % hw == 0 for s in (gla_k, gla_v, d))
    assert ATT_GROUPS[0][1] == 1 and all(win // dil == ATT_BLOCK for win, dil in ATT_GROUPS)
    src, off = {}, 0
    for name, size in (('gq', gla_k), ('gk', gla_k), ('gv', gla_v), ('gr', gla_v), ('glr', GLA_GATE_RANK),
                       ('aq', att_w), ('ak', att_w), ('av', att_w), ('gate_gla', d), ('gate_att', d)):
        src[name] = (off, size)
        off += size
    assert off == w_in.shape[2]
    wt = w_in[0].T
    glr_at, glr_n = src['glr']
    assert glr_n == GLA_GATE_RANK

    def col(name, group=0):
        return src[name][0] + group * hw

    plain = (('gq', gla_k), ('gk', gla_k), ('gv', gla_v), ('gr', gla_v), ('gate_gla', d), ('gate_att', d),
             ('aq', hw), ('ak', hw), ('av', hw))
    layout, tile_cols = {}, []
    for name, size in plain:
        layout[name] = len(tile_cols) * hw
        tile_cols += [col(name) + t * hw for t in range(size // hw)]
    n_plain = len(tile_cols)
    tile_cols += [col(kind, g) for g in range(1, ngrp) for kind in ('aq', 'ak', 'av')]
    dilations = [dil for _, dil in ATT_GROUPS[1:]]

    x2 = x.reshape(n, d)
    proj, glr, *qkv_dil = _in_projection(x2, norm_mix_g[0], wt, glr_at, batch, seq, n_plain, dilations,
                                         tuple(tile_cols))
    y_gla = _gla(proj, glr, gla_gate_up[0], gla_gate_bias[0], gla_norm_g[0], batch, seq, layout)
    att = [_dilated_attention(proj.reshape(batch, 1, seq, proj.shape[1]), 1, layout['aq'])]
    att += [_dilated_attention(a, dil, 0) for a, dil in zip(qkv_dil, dilations)]
    tt = min(TOKEN_TILE, n)
    h, hn, logits = _merge(
        x2, y_gla, att, proj, layout, w_branch_gla[0].astype(BF16), w_branch_att[0].astype(BF16),
        w_mix_out[0].astype(BF16), norm_ffn_g[0], router_w[0], router_b[0], seq, tt)
    gates, loc, cum, cnt = _route(logits, tt)

    bm = MOE_BLOCK_ROWS
    rloc = TOP_K * tt
    nblk = -(-(TOP_K * n) // bm) + N_EXPERTS
    tabs, blocks = _routing_tables(cum[:, :, 0], cnt[:, 0], bm, nblk)
    x_lin = _dispatch(hn, loc, tabs, blocks[1], nblk * bm, rloc)
    y_lin = _experts(x_lin, *blocks, expert_w_gate_up[0], expert_b_gate_up[0], expert_w_down[0],
                     expert_b_down[0], bm)
    out = _combine(h, y_lin, loc, gates, tabs, norm_final_g, rloc)
    return out.reshape(batch, seq, d)
```

```python
import functools
import math

import jax
import jax.numpy as jnp
import numpy as np
from jax import lax
from jax.experimental import pallas as pl
from jax.experimental.pallas import tpu as pltpu

GLA_HEADS = 4
GLA_GATE_RANK = 16
GLA_TAU = 16.0
GLA_CHUNK = 64
ATT_GROUPS = ((128, 1), (512, 4), (2048, 16))
ATT_HEADS_PER_GROUP = 8
ATT_HEAD_DIM = 64
ATT_BLOCK = 128
ROT_DIM = ATT_HEAD_DIM // 4
ROPE_THETA = 500000.0
N_EXPERTS = 32
TOP_K = 4
SWIGLU_LIMIT = 7.0
SWIGLU_ALPHA = 1.702
EPS = 1e-5

LANES = 128
MXU_DIM = 256
VMEM_LIMIT = 56 * 1024 * 1024

F32 = jnp.float32
BF16 = jnp.bfloat16


def _cparams(sem):
    return pltpu.CompilerParams(dimension_semantics=sem, vmem_limit_bytes=VMEM_LIMIT)


def _dot(a, b):
    return jnp.dot(a, b, preferred_element_type=F32)


def _dot_nt(a, b):
    return lax.dot_general(a, b, (((1,), (1,)), ((), ())), preferred_element_type=F32)


def _dot_tn(a, b):
    return lax.dot_general(a, b, (((0,), (0,)), ((), ())), preferred_element_type=F32)


def _split2(a):
    hi = a.astype(BF16)
    lo = (a - hi.astype(F32)).astype(BF16)
    return hi, lo


def _inproj_kernel(x_ref, g_ref, wt_hbm, wglr_ref, cos_ref, sa_ref, sb_ref, proj_ref, glr_ref, *rest,
                   n_plain, dilations, q_scale, tn, tile_rows):
    dil_refs, (xn_ref, stage_ref, w_ref, wst_ref, wsem) = rest[:len(dilations)], rest[len(dilations):]
    nslab = tn // LANES

    @pl.when(pl.program_id(0) == 0)
    def _():
        def window(j):
            return pltpu.make_async_copy(wt_hbm.at[pl.ds(tile_rows[j], tn)], wst_ref.at[j % 2], wsem.at[j % 2])

        window(0).start()
        for j in range(len(tile_rows)):
            if j + 1 < len(tile_rows):
                window(j + 1).start()
            window(j).wait()
            for s in range(nslab):
                w_ref[:, j * tn + s * LANES:j * tn + (s + 1) * LANES] = \
                    jnp.transpose(wst_ref[j % 2, s * LANES:(s + 1) * LANES, :]).astype(BF16)

    x = x_ref[...]
    y = x * lax.rsqrt(jnp.mean(x * x, axis=-1, keepdims=True) + EPS) * g_ref[...]
    xn_ref[...] = y.astype(BF16)
    yh, yl = _split2(y)
    wh, wl = _split2(wglr_ref[...])
    glr_ref[...] = _dot_nt(yh, wh) + _dot_nt(yh, wl) + _dot_nt(yl, wh)

    tm = xn_ref.shape[0]
    half = ROT_DIM // 2

    def slabs(j, kind):
        acc = _dot(xn_ref[...], w_ref[:, j * tn:(j + 1) * tn])
        for s in range(nslab):
            t = acc[:, s * LANES:(s + 1) * LANES]
            if kind < 2:
                t = t * cos_ref[...] + pltpu.roll(t, LANES - half, 1) * sa_ref[...] \
                    + pltpu.roll(t, half, 1) * sb_ref[...]
                if kind == 0:
                    t = t * q_scale
            yield s, t

    for gi, dil in enumerate(dilations):
        for kind in range(3):
            j = n_plain + 3 * gi + kind
            buf = (3 * gi + kind) % stage_ref.shape[0]
            for s, t in slabs(j, kind):
                stage_ref[buf, s] = t
            for r in range(dil):
                for s in range(nslab):
                    dil_refs[gi][0, r, :, kind * tn + s * LANES:kind * tn + (s + 1) * LANES] = \
                        stage_ref[buf, s, pl.ds(r, tm // dil, stride=dil), :].astype(BF16)

    q0 = n_plain - 3
    for j in range(n_plain):
        for s, t in slabs(j, j - q0 if j >= q0 else 2):
            proj_ref[:, j * tn + s * LANES:j * tn + (s + 1) * LANES] = t.astype(BF16)


def _rope_tables(seq):
    half = ROT_DIM // 2
    inv_freq = ROPE_THETA ** (-np.arange(half, dtype=np.float32) * np.float32(2.0 / ROT_DIM))
    pos = jnp.arange(seq, dtype=F32)
    ang = pos[:, None] * jnp.asarray(inv_freq, F32)[None, :]
    cos = jnp.cos(ang)
    sin = jnp.sin(ang)
    lane = np.arange(LANES) % ATT_HEAD_DIM
    idx = np.where(lane < ROT_DIM, lane % half, 0)
    in_rot = jnp.asarray(lane < ROT_DIM)
    first = jnp.asarray(lane < half)
    second = jnp.asarray((lane >= half) & (lane < ROT_DIM))
    cos_t = jnp.where(in_rot[None, :], cos[:, idx], 1.0)
    sin_g = sin[:, idx]
    sa = jnp.where(first[None, :], -sin_g, 0.0)
    sb = jnp.where(second[None, :], sin_g, 0.0)
    return cos_t.astype(F32), sa.astype(F32), sb.astype(F32)


def _in_projection(x2, norm_g, wt, glr_row, batch, seq, n_plain, dilations, tile_rows):
    n, d = x2.shape
    tn = ATT_HEADS_PER_GROUP * ATT_HEAD_DIM
    tm = min(512, seq)
    ntile = len(tile_rows)
    assert n % tm == 0 and seq % tm == 0 and ntile == n_plain + 3 * len(dilations)
    assert all(tm % (dil * 16) == 0 for dil in dilations)
    assert glr_row % GLA_GATE_RANK == 0 and all(r % 8 == 0 and r + tn <= wt.shape[0] for r in tile_rows)
    cos_t, sa, sb = _rope_tables(seq)
    spt = seq // tm
    kern = functools.partial(_inproj_kernel, n_plain=n_plain, dilations=tuple(dilations),
                             q_scale=ATT_HEAD_DIM ** -0.5, tn=tn, tile_rows=tuple(tile_rows))
    out_specs = [
        pl.BlockSpec((tm, n_plain * tn), lambda i: (i, 0)),
        pl.BlockSpec((tm, GLA_GATE_RANK), lambda i: (i, 0)),
    ]
    out_shape = [
        jax.ShapeDtypeStruct((n, n_plain * tn), BF16),
        jax.ShapeDtypeStruct((n, GLA_GATE_RANK), F32),
    ]
    for dil in dilations:
        out_specs.append(pl.BlockSpec((1, dil, tm // dil, 3 * tn), lambda i: (i // spt, 0, i % spt, 0)))
        out_shape.append(jax.ShapeDtypeStruct((batch, dil, seq // dil, 3 * tn), BF16))
    const = lambda i: (0, 0)
    return pl.pallas_call(
        kern,
        grid=(n // tm,),
        in_specs=[
            pl.BlockSpec((tm, d), lambda i: (i, 0)),
            pl.BlockSpec((1, d), const),
            pl.BlockSpec(memory_space=pl.ANY),
            pl.BlockSpec((GLA_GATE_RANK, d), lambda i: (glr_row // GLA_GATE_RANK, 0)),
            pl.BlockSpec((tm, LANES), lambda i: (i % spt, 0)),
            pl.BlockSpec((tm, LANES), lambda i: (i % spt, 0)),
            pl.BlockSpec((tm, LANES), lambda i: (i % spt, 0)),
        ],
        out_specs=out_specs,
        out_shape=out_shape,
        scratch_shapes=[pltpu.VMEM((tm, d), BF16), pltpu.VMEM((2, tn // LANES, tm, LANES), F32),
                        pltpu.VMEM((d, ntile * tn), BF16),
                        pltpu.VMEM((2, tn, d), F32), pltpu.SemaphoreType.DMA((2,))],
        compiler_params=_cparams(("arbitrary",)),
        name="in_projection",
    )(x2, norm_g.reshape(1, d), wt, wt, cos_t, sa, sb)


def _gla_kernel(q_ref, k_ref, v_ref, r_ref, glr_ref, up_ref, bias_ref, ng_ref, tri_ref, y_ref, state_ref,
                *, dk, dv, ts):
    t = pl.program_id(2)

    @pl.when(t == 0)
    def _():
        state_ref[...] = jnp.zeros_like(state_ref)

    c = GLA_CHUNK
    hps = state_ref.shape[0]
    gh, gl = _split2(glr_ref[...])
    uh, ul = _split2(up_ref[...])
    z = _dot(gh, uh) + _dot(gh, ul) + _dot(gl, uh) + bias_ref[...]
    logdec = (jnp.minimum(z, 0.0) - jnp.log1p(jnp.exp(-jnp.abs(z)))) * (1.0 / GLA_TAU)
    ldh, ldl = _split2(logdec)
    tri = tri_ref[...]
    row = lax.broadcasted_iota(jnp.int32, (c, c), 0)
    col = lax.broadcasted_iota(jnp.int32, (c, c), 1)
    causal = col <= row
    scale = dk ** -0.5
    ng = ng_ref[...]
    states = [state_ref[hh] for hh in range(hps)]
    for ci in range(ts // c):
        sl = slice(ci * c, (ci + 1) * c)
        b2 = _dot(tri, ldh[sl]) + _dot(tri, ldl[sl])
        for hh in range(hps):
            ks = slice(hh * dk, (hh + 1) * dk)
            vs = slice(hh * dv, (hh + 1) * dv)
            b = b2[:, ks]
            bl = b[c - 1:c, :]
            eb = jnp.exp(b)
            enb = jnp.exp(-b)
            ebl = jnp.exp(bl)
            qf = q_ref[sl, ks].astype(F32)
            kf = k_ref[sl, ks].astype(F32)
            qe = (qf * scale * eb).astype(BF16)
            ke = (kf * enb).astype(BF16)
            kd = (kf * enb * ebl).astype(BF16)
            vb = v_ref[sl, vs]
            a = jnp.where(causal, _dot_nt(qe, ke), 0.0).astype(BF16)
            o = _dot(a, vb) + _dot(qe, states[hh].astype(BF16))
            dec = jnp.transpose(jnp.broadcast_to(ebl, (dk, dk)))
            dec_full = jnp.concatenate([dec] * (dv // dk), axis=1)
            states[hh] = dec_full * states[hh] + _dot_tn(kd, vb)
            o = o * lax.rsqrt(jnp.mean(o * o, axis=-1, keepdims=True) + EPS) * ng
            rf = r_ref[sl, vs].astype(F32)
            o = o * (rf * jax.nn.sigmoid(rf))
            y_ref[sl, vs] = o.astype(BF16)
    for hh in range(hps):
        state_ref[hh] = states[hh]


def _gla(proj, glr, gate_up, gate_bias, norm_g, batch, seq, layout):
    n = proj.shape[0]
    dk = (layout['gk'] - layout['gq']) // GLA_HEADS
    dv = (layout['gr'] - layout['gv']) // GLA_HEADS
    ts = min(512, seq)
    assert dk == LANES and dv % dk == 0 and seq % ts == 0 and ts % GLA_CHUNK == 0
    spt = seq // ts
    hps = GLA_HEADS
    wk, wv = hps * dk, hps * dv
    assert GLA_HEADS % hps == 0 and all(layout[s] % wk == 0 for s in ('gq', 'gk'))
    assert all(layout[s] % wv == 0 for s in ('gv', 'gr'))
    qb, kb = layout['gq'] // wk, layout['gk'] // wk
    vb, rb = layout['gv'] // wv, layout['gr'] // wv
    tri = jnp.asarray(np.tril(np.ones((GLA_CHUNK, GLA_CHUNK), np.float32)), BF16)
    kern = functools.partial(_gla_kernel, dk=dk, dv=dv, ts=ts)
    row = lambda b, h, t: b * spt + t
    return pl.pallas_call(
        kern,
        grid=(batch, GLA_HEADS // hps, spt),
        in_specs=[
            pl.BlockSpec((ts, wk), lambda b, h, t: (row(b, h, t), qb + h)),
            pl.BlockSpec((ts, wk), lambda b, h, t: (row(b, h, t), kb + h)),
            pl.BlockSpec((ts, wv), lambda b, h, t: (row(b, h, t), vb + h)),
            pl.BlockSpec((ts, wv), lambda b, h, t: (row(b, h, t), rb + h)),
            pl.BlockSpec((ts, GLA_GATE_RANK), lambda b, h, t: (row(b, h, t), 0)),
            pl.BlockSpec((GLA_GATE_RANK, wk), lambda b, h, t: (0, h)),
            pl.BlockSpec((1, wk), lambda b, h, t: (0, h)),
            pl.BlockSpec((1, dv), lambda b, h, t: (0, 0)),
            pl.BlockSpec((GLA_CHUNK, GLA_CHUNK), lambda b, h, t: (0, 0)),
        ],
        out_specs=pl.BlockSpec((ts, wv), lambda b, h, t: (row(b, h, t), h)),
        out_shape=jax.ShapeDtypeStruct((n, GLA_HEADS * dv), BF16),
        scratch_shapes=[pltpu.VMEM((hps, dk, dv), F32)],
        compiler_params=_cparams(("arbitrary", "arbitrary", "arbitrary")),
        name="gla",
    )(proj, proj, proj, proj, glr, gate_up, gate_bias.reshape(1, -1), norm_g.reshape(1, -1), tri)


LSE_REP = LANES // ATT_HEADS_PER_GROUP


def _att_kernel(q_ref, k_ref, v_ref, o_ref, lse_ref, *, nb, pairs, unroll):
    blk = ATT_BLOCK
    hd = ATT_HEAD_DIM
    row = lax.broadcasted_iota(jnp.int32, (2 * blk, 2 * blk), 0) % blk
    col = lax.broadcasted_iota(jnp.int32, (2 * blk, 2 * blk), 1)
    dist = row + blk - col
    band = jnp.logical_and(dist >= 0, dist <= blk)
    causal = band[:, blk:]
    lane = lax.broadcasted_iota(jnp.int32, (blk, LANES), 1)
    head0 = lane < hd
    lane_head = lane // LSE_REP
    neg = -jnp.inf

    def block(j, first):
        r0 = pl.multiple_of(j * blk, blk)
        lse = jnp.zeros((blk, LANES), F32)
        for p in range(pairs):
            cs = slice(p * LANES, (p + 1) * LANES)
            q2 = q_ref[0, 0, pl.ds(r0, blk), cs]
            zero = jnp.zeros_like(q2)
            qs = jnp.concatenate([jnp.where(head0, q2, zero), jnp.where(head0, zero, q2)], axis=0)
            if first:
                kk = k_ref[0, 0, pl.ds(r0, blk), cs]
                vv = v_ref[0, 0, pl.ds(r0, blk), cs]
                s = jnp.where(causal, _dot_nt(qs, kk), neg)
            else:
                k0 = pl.multiple_of(r0 - blk, blk)
                kk = k_ref[0, 0, pl.ds(k0, 2 * blk), cs]
                vv = v_ref[0, 0, pl.ds(k0, 2 * blk), cs]
                s = jnp.where(band, _dot_nt(qs, kk), neg)
            mx = jnp.max(s, axis=-1, keepdims=True)
            e = jnp.exp(s - mx).astype(BF16)
            one = jnp.ones_like(vv)
            hv = lax.broadcasted_iota(jnp.int32, vv.shape, 1) < hd
            out0 = _dot(e[:blk], jnp.where(hv, vv, one))
            out1 = _dot(e[blk:], jnp.where(hv, one, vv))
            num = jnp.where(head0, out0, out1)
            den_x = jnp.where(head0, out1, out0)
            den = pltpu.roll(den_x, hd, 1)
            o_ref[0, 0, pl.ds(r0, blk), cs] = (num / den).astype(o_ref.dtype)
            lse0 = mx[:blk] + jnp.log(jnp.where(head0, den, den_x))
            lse1 = mx[blk:] + jnp.log(jnp.where(head0, den_x, den))
            lse = jnp.where(lane_head == 2 * p, lse0, jnp.where(lane_head == 2 * p + 1, lse1, lse))
        lse_ref[0, 0, pl.ds(r0, blk), :] = lse

    block(0, True)

    def body(j, carry):
        block(j, False)
        return carry

    lax.fori_loop(1, nb, body, 0, unroll=unroll)


def _dilated_attention(qkv, dilation, col0):
    batch, dil, sub, width = qkv.shape
    hw = ATT_HEADS_PER_GROUP * ATT_HEAD_DIM
    assert dil == dilation and sub % ATT_BLOCK == 0 and col0 % hw == 0
    nb = sub // ATT_BLOCK
    pairs = hw // LANES
    assert 2 * pairs * LSE_REP == LANES
    kern = functools.partial(_att_kernel, nb=nb, pairs=pairs, unroll=2 if nb > 2 else 1)

    def in_spec(kind):
        base = (col0 + kind * hw) // hw
        return pl.BlockSpec((1, 1, sub, hw), lambda b, r: (b, r, 0, base))

    return pl.pallas_call(
        kern,
        grid=(batch, dil),
        in_specs=[in_spec(0), in_spec(1), in_spec(2)],
        out_specs=[pl.BlockSpec((1, 1, sub, hw), lambda b, r: (b, r, 0, 0)),
                   pl.BlockSpec((1, 1, sub, LANES), lambda b, r: (b, r, 0, 0))],
        out_shape=[
            jax.ShapeDtypeStruct((batch, dil, sub, hw), BF16),
            jax.ShapeDtypeStruct((batch, dil, sub, LANES), F32),
        ],
        compiler_params=_cparams(("arbitrary", "arbitrary")),
        name=f"dilated_attention_d{dilation}",
    )(qkv, qkv, qkv)


def _merge_kernel(x_ref, yg_ref, o1_ref, o2_ref, o3_ref, l1_ref, l2_ref, l3_ref,
                  gg0_ref, gg1_ref, ga0_ref, ga1_ref, wbg_ref, wba_ref, wmo_ref, nf_ref, rw_ref, rb_ref,
                  spread_ref, h_ref, hn_ref, lg_ref, os_ref, ls_ref):
    def token_order(src_ref, stage_ref):
        dil, sub, w = src_ref.shape[1:]
        if dil == 1:
            return src_ref[0, 0].astype(F32)
        for r in range(dil):
            val = src_ref[0, r].astype(F32)
            for s in range(w // LANES):
                stage_ref[s, pl.ds(r, sub, stride=dil), :] = val[:, s * LANES:(s + 1) * LANES]
        return jnp.concatenate([stage_ref[s] for s in range(w // LANES)], axis=1)

    l1, o1 = token_order(l1_ref, None), token_order(o1_ref, None)
    l2, o2 = token_order(l2_ref, ls_ref.at[0]), token_order(o2_ref, os_ref.at[0])
    l3, o3 = token_order(l3_ref, ls_ref.at[1]), token_order(o3_ref, os_ref.at[1])
    m = jnp.maximum(jnp.maximum(l1, l2), l3)
    w1, w2, w3 = jnp.exp(l1 - m), jnp.exp(l2 - m), jnp.exp(l3 - m)
    inv = 1.0 / (w1 + w2 + w3)
    spread = spread_ref[...]

    def per_lane(w):
        hi, lo = _split2(w * inv)
        return _dot(hi, spread) + _dot(lo, spread)

    y_att = per_lane(w1) * o1 + per_lane(w2) * o2 + per_lane(w3) * o3

    m_gla = _dot(yg_ref[...], wbg_ref[...])
    m_att = _dot(y_att.astype(BF16), wba_ref[...])
    gg = jnp.concatenate([gg0_ref[...], gg1_ref[...]], axis=1).astype(F32)
    ga = jnp.concatenate([ga0_ref[...], ga1_ref[...]], axis=1).astype(F32)
    merged = (0.5 + 0.5 * jnp.tanh(0.5 * gg)) * m_gla + (0.5 + 0.5 * jnp.tanh(0.5 * ga)) * m_att
    h = x_ref[...] + _dot(merged.astype(BF16), wmo_ref[...])
    h_ref[...] = h
    hn = h * lax.rsqrt(jnp.mean(h * h, axis=-1, keepdims=True) + EPS) * nf_ref[...]
    hn_ref[...] = hn.astype(BF16)

    hh, hl = _split2(hn)
    rh, rl = _split2(rw_ref[...])
    lg_ref[...] = _dot_nt(rh, hh) + _dot_nt(rh, hl) + _dot_nt(rl, hh) + rb_ref[...]


def _route_kernel(lg_ref, triu_ref, ltri_ref, tg_ref, lc_ref, cum_ref, cnt_ref, carry_ref):
    ne = N_EXPERTS
    tiles, _, tt = tg_ref.shape

    @pl.when(pl.program_id(0) == 0)
    def _():
        carry_ref[...] = jnp.zeros_like(carry_ref)

    eid = lax.broadcasted_iota(jnp.int32, (ne, tt), 0)
    carry = carry_ref[...]
    for t in range(tiles):
        logit = lg_ref[:, t * tt:(t + 1) * tt]
        member = jnp.zeros((ne, tt), jnp.bool_)
        vals, idxs = [], []
        for _ in range(TOP_K):
            mx = jnp.max(logit, axis=0, keepdims=True)
            idx = jnp.min(jnp.where(logit == mx, eid, ne), axis=0, keepdims=True)
            sel = eid == idx
            member = jnp.logical_or(member, sel)
            logit = jnp.where(sel, -jnp.inf, logit)
            vals.append(mx)
            idxs.append(idx)
        ex = [jnp.exp(v - vals[0]) for v in vals]
        tot = ex[0] + ex[1] + ex[2] + ex[3]
        memf = jnp.where(member, 1.0, 0.0)
        lens = jnp.sum(memf, axis=1, keepdims=True)
        lens_b = jnp.broadcast_to(lens, (ne, LANES))
        off = _dot(ltri_ref[...], lens_b.astype(BF16))
        pos = _dot(memf.astype(BF16), triu_ref[...]) + off[:, 0:1]
        for k in range(TOP_K):
            tg_ref[t, k:k + 1, :] = ex[k] / tot
            lc_ref[t, k:k + 1, :] = jnp.sum(jnp.where(eid == idxs[k], pos, 0.0), axis=0,
                                            keepdims=True).astype(jnp.int32)
        cum_ref[t] = carry.astype(jnp.int32)
        carry = carry + lens_b
    carry_ref[...] = carry
    cnt_ref[...] = carry.astype(jnp.int32)


def _route(logits, tt):
    ne, n = logits.shape
    nt = n // tt
    assert tt <= 256, "per-tile expert counts go through a bf16 matmul operand: exact up to 256"
    tiles = 8 if nt % 8 == 0 else 1
    triu = jnp.asarray(np.triu(np.ones((tt, tt), np.float32), 1), BF16)
    ltri = jnp.asarray(np.tril(np.ones((ne, ne), np.float32), -1), BF16)
    const = lambda i: (0, 0)
    return pl.pallas_call(
        _route_kernel,
        grid=(nt // tiles,),
        in_specs=[pl.BlockSpec((ne, tiles * tt), lambda i: (0, i)),
                  pl.BlockSpec((tt, tt), const), pl.BlockSpec((ne, ne), const)],
        out_specs=[
            pl.BlockSpec((tiles, TOP_K, tt), lambda i: (i, 0, 0)),
            pl.BlockSpec((tiles, TOP_K, tt), lambda i: (i, 0, 0)),
            pl.BlockSpec((tiles, ne, LANES), lambda i: (i, 0, 0)),
            pl.BlockSpec((ne, LANES), const),
        ],
        out_shape=[
            jax.ShapeDtypeStruct((nt, TOP_K, tt), F32),
            jax.ShapeDtypeStruct((nt, TOP_K, tt), jnp.int32),
            jax.ShapeDtypeStruct((nt, ne, LANES), jnp.int32),
            jax.ShapeDtypeStruct((ne, LANES), jnp.int32),
        ],
        scratch_shapes=[pltpu.VMEM((ne, LANES), F32)],
        compiler_params=_cparams(("arbitrary",)),
        name="route_topk",
    )(logits, triu, ltri)


def _merge(x2, y_gla, att, proj, layout, wbg, wba, wmo, norm_ffn_g, router_w, router_b, seq, tm):
    n, d = x2.shape
    hw = ATT_HEADS_PER_GROUP * ATT_HEAD_DIM
    half = d // 2
    assert n % tm == 0 and seq % tm == 0
    assert layout['gate_gla'] % half == 0 and layout['gate_att'] % half == 0
    assert all(tm % (o.shape[1] * 16) == 0 for o, _ in att) and att[0][0].shape[1] == 1
    nt = n // tm
    spt = seq // tm
    ggb = layout['gate_gla'] // half
    gab = layout['gate_att'] // half
    row = lambda i: (i, 0)
    const = lambda i: (0, 0)

    def att_spec(a):
        dil = a.shape[1]
        return pl.BlockSpec((1, dil, tm // dil, a.shape[3]), lambda i: (i // spt, 0, i % spt, 0))

    (o1, l1), (o2, l2), (o3, l3) = att
    in_specs = [
        pl.BlockSpec((tm, d), row),
        pl.BlockSpec((tm, y_gla.shape[1]), row),
        att_spec(o1), att_spec(o2), att_spec(o3), att_spec(l1), att_spec(l2), att_spec(l3),
        pl.BlockSpec((tm, half), lambda i: (i, ggb)), pl.BlockSpec((tm, half), lambda i: (i, ggb + 1)),
        pl.BlockSpec((tm, half), lambda i: (i, gab)), pl.BlockSpec((tm, half), lambda i: (i, gab + 1)),
        pl.BlockSpec(wbg.shape, const), pl.BlockSpec(wba.shape, const), pl.BlockSpec(wmo.shape, const),
        pl.BlockSpec((1, d), const),
        pl.BlockSpec((N_EXPERTS, d), const),
        pl.BlockSpec((N_EXPERTS, 1), const),
        pl.BlockSpec((LANES, hw), const),
    ]
    out_specs = [
        pl.BlockSpec((tm, d), row),
        pl.BlockSpec((tm, d), row),
        pl.BlockSpec((N_EXPERTS, tm), lambda i: (0, i)),
    ]
    out_shape = [
        jax.ShapeDtypeStruct((n, d), F32),
        jax.ShapeDtypeStruct((n, d), BF16),
        jax.ShapeDtypeStruct((N_EXPERTS, n), F32),
    ]
    spread = jnp.asarray(np.arange(LANES)[:, None] == (np.arange(hw)[None, :] // ATT_HEAD_DIM) * LSE_REP, BF16)
    return pl.pallas_call(
        _merge_kernel,
        grid=(nt,),
        in_specs=in_specs,
        out_specs=out_specs,
        out_shape=out_shape,
        scratch_shapes=[pltpu.VMEM((2, hw // LANES, tm, LANES), F32), pltpu.VMEM((2, 1, tm, LANES), F32)],
        compiler_params=_cparams(("arbitrary",)),
        name="merge_router",
    )(x2, y_gla, o1, o2, o3, l1, l2, l3, proj, proj, proj, proj, wbg, wba, wmo,
      norm_ffn_g.reshape(1, d), router_w.T, router_b.reshape(N_EXPERTS, 1), spread)


LIN_SUB = 8


def _lin_pack(val):
    return [val[:, c * LANES:(c + 1) * LANES] for c in range(LIN_SUB)]


def _lin_unpack(slabs):
    return jnp.concatenate(slabs, axis=1)


def _dispatch_kernel(off_ref, len_ref, dst_ref, zdst_ref, zlen_ref, nact_ref, hn_ref, loc_ref, x_hbm,
                     xs_ref, zero_ref, sem, zsem, *, rloc):
    i = pl.program_id(0)
    nt = pl.num_programs(0)
    slot = i % 2
    ne = N_EXPERTS

    def drain(s):
        rows = rloc * LIN_SUB
        pltpu.make_async_copy(xs_ref.at[s], x_hbm.at[pl.ds(0, rows)], sem.at[s]).wait()

    @pl.when(i == 0)
    def _():
        zero_ref[...] = jnp.zeros_like(zero_ref)

        def fill_copy(e):
            rows = zlen_ref[e] * LIN_SUB
            dst = pl.multiple_of(zdst_ref[e] * LIN_SUB, 8)
            return rows, pltpu.make_async_copy(zero_ref.at[pl.ds(0, rows)], x_hbm.at[pl.ds(dst, rows)], zsem)

        def fill(e, carry):
            rows, copy = fill_copy(e)
            pl.when(rows > 0)(copy.start)
            return carry

        def fill_done(e, carry):
            rows, copy = fill_copy(e)
            pl.when(rows > 0)(copy.wait)
            return carry

        def block_copy(b):
            rows = zero_ref.shape[0]
            return pltpu.make_async_copy(zero_ref, x_hbm.at[pl.ds(pl.multiple_of(b * rows, rows), rows)], zsem)

        def fill_block(b, carry):
            block_copy(b).start()
            return carry

        def fill_block_done(b, carry):
            block_copy(b).wait()
            return carry

        nblk = x_hbm.shape[0] // zero_ref.shape[0]
        lax.fori_loop(0, ne, fill, 0)
        lax.fori_loop(nact_ref[0], nblk, fill_block, 0)
        lax.fori_loop(0, ne, fill_done, 0)
        lax.fori_loop(nact_ref[0], nblk, fill_block_done, 0)

    @pl.when(i >= 2)
    def _():
        drain(slot)

    loc = loc_ref[0]
    tt = loc.shape[1]
    row_id = lax.broadcasted_iota(jnp.int32, (rloc, tt), 0)
    hit = row_id == loc[0:1, :]
    for k in range(1, TOP_K):
        hit = jnp.logical_or(hit, row_id == loc[k:k + 1, :])
    onehot = jnp.where(hit, 1.0, 0.0).astype(BF16)
    xs = _dot(onehot, hn_ref[...])
    for c, slab in enumerate(_lin_pack(xs)):
        xs_ref[slot, pl.ds(c, rloc, stride=LIN_SUB), :] = slab

    def send(e, carry):
        rows = len_ref[i * ne + e] * LIN_SUB

        @pl.when(rows > 0)
        def _():
            src = pl.multiple_of(off_ref[i * ne + e] * LIN_SUB, 8)
            dst = pl.multiple_of(dst_ref[i * ne + e] * LIN_SUB, 8)
            pltpu.make_async_copy(xs_ref.at[slot, pl.ds(src, rows)], x_hbm.at[pl.ds(dst, rows)], sem.at[slot]).start()
        return carry

    lax.fori_loop(0, ne, send, 0)

    @pl.when(i == nt - 1)
    def _():
        @pl.when(i >= 1)
        def _():
            drain(1 - slot)
        drain(slot)


def _dispatch(hn, loc, tabs, nact, p_rows, rloc):
    n, d = hn.shape
    nt, _, tt = loc.shape
    assert d == LIN_SUB * LANES and p_rows % MOE_BLOCK_ROWS == 0
    kern = functools.partial(_dispatch_kernel, rloc=rloc)
    grid_spec = pltpu.PrefetchScalarGridSpec(
        num_scalar_prefetch=6,
        grid=(nt,),
        in_specs=[
            pl.BlockSpec((tt, d), lambda i, *_: (i, 0)),
            pl.BlockSpec((1, TOP_K, tt), lambda i, *_: (i, 0, 0)),
        ],
        out_specs=pl.BlockSpec(memory_space=pl.ANY),
        scratch_shapes=[
            pltpu.VMEM((2, rloc * LIN_SUB, LANES), F32),
            pltpu.VMEM((MOE_BLOCK_ROWS * LIN_SUB, LANES), F32),
            pltpu.SemaphoreType.DMA((2,)),
            pltpu.SemaphoreType.DMA(()),
        ],
    )
    return pl.pallas_call(
        kern,
        grid_spec=grid_spec,
        out_shape=jax.ShapeDtypeStruct((p_rows * LIN_SUB, LANES), F32),
        compiler_params=_cparams(("arbitrary",)),
        name="moe_dispatch",
    )(tabs['off'], tabs['len'], tabs['dst'], tabs['zdst'], tabs['zlen'], nact, hn, loc)


def _expert_kernel(bexp_ref, nact_ref, first_ref, next_ref, x_ref, wgu_hbm, wd_hbm, bgu_ref, bd_ref, perm_ref,
                   y_ref, wgu_f, wd_f, wgu_s, wd_s, sem, *, bm):
    grp = MXU_DIM
    ff2 = wgu_s.shape[1]
    halfg = grp // 2
    brows = bm * LIN_SUB
    bps = x_ref.shape[0] // brows

    def fetch(ex):
        return (pltpu.make_async_copy(wgu_hbm.at[ex], wgu_f, sem.at[0]),
                pltpu.make_async_copy(wd_hbm.at[ex], wd_f, sem.at[1]))

    @pl.when(pl.program_id(0) == 0)
    def _():
        for copy in fetch(bexp_ref[0]):
            copy.start()

    def block(b, r0):
        e = bexp_ref[b]
        active = b < nact_ref[0]

        @pl.when(jnp.logical_and(active, first_ref[b] == 1))
        def _():
            for copy in fetch(e):
                copy.wait()
            perm = perm_ref[...]
            for g in range(ff2 // grp):
                cols = slice(g * grp, (g + 1) * grp)
                wgu_s[:, cols] = _dot(wgu_f[:, cols].astype(BF16), perm).astype(BF16)
            wd_s[...] = wd_f[...].astype(BF16)
            nxt = next_ref[b]

            @pl.when(nxt >= 0)
            def _():
                for copy in fetch(nxt):
                    copy.start()

        @pl.when(active)
        def _():
            x = _lin_unpack([x_ref[pl.ds(r0 + c, bm, stride=LIN_SUB), :] for c in range(LIN_SUB)]).astype(BF16)
            acts = []
            for g in range(ff2 // grp):
                cols = slice(g * grp, (g + 1) * grp)
                hg = _dot(x, wgu_s[:, cols]) + bgu_ref[e, :, cols]
                gate = jnp.minimum(hg[:, :halfg], SWIGLU_LIMIT)
                up = jnp.clip(hg[:, halfg:], -SWIGLU_LIMIT, SWIGLU_LIMIT)
                acts.append(((up + 1.0) * (gate * jax.nn.sigmoid(SWIGLU_ALPHA * gate))).astype(BF16))
            act = jnp.concatenate(acts, axis=1)
            y = _dot(act, wd_s[...]) + bd_ref[e]
            for c, slab in enumerate(_lin_pack(y)):
                y_ref[pl.ds(r0 + c, bm, stride=LIN_SUB), :] = slab

        @pl.when(jnp.logical_not(active))
        def _():
            y_ref[pl.ds(r0, brows), :] = jnp.zeros((brows, y_ref.shape[1]), y_ref.dtype)

    for sub in range(bps):
        block(pl.program_id(0) * bps + sub, sub * brows)


def _gate_up_permutation():
    grp = MXU_DIM
    halfg = grp // 2
    out = np.arange(grp)
    src = np.where(out < halfg, 2 * out, 2 * (out - halfg) + 1)
    p = np.zeros((grp, grp), np.float32)
    p[src, out] = 1.0
    return jnp.asarray(p, BF16)


def _experts(x_lin, bexp, nact, first, nxt, w_gate_up, b_gate_up, w_down, b_down, bm):
    ne, d, ff2 = w_gate_up.shape
    ff = ff2 // 2
    grp = MXU_DIM
    assert ff2 % grp == 0 and d == LIN_SUB * LANES
    nblk = x_lin.shape[0] // (bm * LIN_SUB)
    bps = max(b for b in (4, 2, 1) if nblk % b == 0)
    srows = bps * bm * LIN_SUB
    bgu = b_gate_up.reshape(ne, ff2 // grp, grp // 2, 2).transpose(0, 1, 3, 2).reshape(ne, 1, ff2)
    grid_spec = pltpu.PrefetchScalarGridSpec(
        num_scalar_prefetch=4,
        grid=(nblk // bps,),
        in_specs=[
            pl.BlockSpec((srows, LANES), lambda i, *_: (i, 0)),
            pl.BlockSpec(memory_space=pl.ANY),
            pl.BlockSpec(memory_space=pl.ANY),
            pl.BlockSpec(memory_space=pltpu.VMEM),
            pl.BlockSpec(memory_space=pltpu.VMEM),
            pl.BlockSpec((grp, grp), lambda i, *_: (0, 0)),
        ],
        out_specs=pl.BlockSpec((srows, LANES), lambda i, *_: (i, 0)),
        scratch_shapes=[pltpu.VMEM((d, ff2), F32), pltpu.VMEM((ff, d), F32),
                        pltpu.VMEM((d, ff2), BF16), pltpu.VMEM((ff, d), BF16),
                        pltpu.SemaphoreType.DMA((2,))],
    )
    return pl.pallas_call(
        functools.partial(_expert_kernel, bm=bm),
        grid_spec=grid_spec,
        out_shape=jax.ShapeDtypeStruct(x_lin.shape, F32),
        compiler_params=_cparams(("arbitrary",)),
        name="moe_experts",
    )(bexp, nact, first, nxt, x_lin, w_gate_up, w_down, bgu, b_down.reshape(ne, 1, d), _gate_up_permutation())


def _combine_kernel(off_ref, len_ref, dst_ref, h_ref, loc_ref, gate_ref, g_ref, y_hbm, out_ref,
                    ybuf, sem, *, rloc):
    i = pl.program_id(0)
    nt = pl.num_programs(0)
    slot = i % 2
    ne = N_EXPERTS

    def fetch(step, s):
        def run(e, carry):
            rows = len_ref[step * ne + e] * LIN_SUB

            @pl.when(rows > 0)
            def _():
                src = pl.multiple_of(dst_ref[step * ne + e] * LIN_SUB, 8)
                dst = pl.multiple_of(off_ref[step * ne + e] * LIN_SUB, 8)
                pltpu.make_async_copy(y_hbm.at[pl.ds(src, rows)], ybuf.at[s, pl.ds(dst, rows)], sem.at[s]).start()
            return carry

        lax.fori_loop(0, ne, run, 0)

    @pl.when(i == 0)
    def _():
        fetch(0, 0)

    @pl.when(i + 1 < nt)
    def _():
        fetch(i + 1, 1 - slot)

    pltpu.make_async_copy(y_hbm.at[pl.ds(0, rloc * LIN_SUB)], ybuf.at[slot], sem.at[slot]).wait()

    y = _lin_unpack([ybuf[slot, pl.ds(c, rloc, stride=LIN_SUB), :] for c in range(LIN_SUB)]).astype(BF16)
    loc = loc_ref[0]
    gate = gate_ref[0]
    tt = loc.shape[1]
    row_id = lax.broadcasted_iota(jnp.int32, (rloc, tt), 0)
    w = jnp.zeros((rloc, tt), F32)
    for k in range(TOP_K):
        w = w + jnp.where(row_id == loc[k:k + 1, :], gate[k:k + 1, :], 0.0)
    h = h_ref[...] + _dot_tn(w.astype(BF16), y)
    out_ref[...] = h * lax.rsqrt(jnp.mean(h * h, axis=-1, keepdims=True) + EPS) * g_ref[...]


def _combine(h, y_lin, loc, gates, tabs, norm_g, rloc):
    n, d = h.shape
    nt, _, tt = loc.shape
    kern = functools.partial(_combine_kernel, rloc=rloc)
    grid_spec = pltpu.PrefetchScalarGridSpec(
        num_scalar_prefetch=3,
        grid=(nt,),
        in_specs=[
            pl.BlockSpec((tt, d), lambda i, *_: (i, 0)),
            pl.BlockSpec((1, TOP_K, tt), lambda i, *_: (i, 0, 0)),
            pl.BlockSpec((1, TOP_K, tt), lambda i, *_: (i, 0, 0)),
            pl.BlockSpec((1, d), lambda i, *_: (0, 0)),
            pl.BlockSpec(memory_space=pl.ANY),
        ],
        out_specs=pl.BlockSpec((tt, d), lambda i, *_: (i, 0)),
        scratch_shapes=[
            pltpu.VMEM((2, rloc * LIN_SUB, LANES), F32),
            pltpu.SemaphoreType.DMA((2,)),
        ],
    )
    return pl.pallas_call(
        kern,
        grid_spec=grid_spec,
        out_shape=jax.ShapeDtypeStruct((n, d), F32),
        compiler_params=_cparams(("arbitrary",)),
        name="moe_combine",
    )(tabs['off'], tabs['len'], tabs['dst'], h, loc, gates, norm_g.reshape(1, d), y_lin)


def _routing_tables(cum, counts, bm, nblk):
    padded = (counts + bm - 1) // bm * bm
    pend = jnp.cumsum(padded)
    pstart = pend - padded
    blk = jnp.arange(nblk, dtype=jnp.int32)
    bexp_raw = jnp.sum((pend // bm)[None, :] <= blk[:, None], axis=1).astype(jnp.int32)
    nact = (pend[-1] // bm).astype(jnp.int32)
    last = jnp.minimum(bexp_raw[jnp.maximum(nact - 1, 0)], N_EXPERTS - 1)
    bexp = jnp.where(blk < nact, jnp.minimum(bexp_raw, N_EXPERTS - 1), last)
    prev = jnp.concatenate([jnp.full((1,), -1, jnp.int32), bexp[:-1]])
    first = jnp.logical_and(bexp != prev, blk < nact)
    later = jnp.logical_and(first[None, :], blk[None, :] > blk[:, None])
    nxt_blk = jnp.min(jnp.where(later, blk[None, :], nblk), axis=1)
    nxt = jnp.where(nxt_blk < nblk, bexp[jnp.minimum(nxt_blk, nblk - 1)], -1)
    cum_full = jnp.concatenate([cum, counts[None, :]], axis=0)
    run_len = cum_full[1:] - cum_full[:-1]
    run_off = jnp.cumsum(run_len, axis=1) - run_len
    tabs = {
        'off': run_off.reshape(-1), 'len': run_len.reshape(-1),
        'dst': (pstart[None, :] + cum_full[:-1]).reshape(-1),
        'zdst': pstart + counts, 'zlen': padded - counts,
    }
    blocks = (bexp, nact.reshape(1), first.astype(jnp.int32), nxt.astype(jnp.int32))
    return {k: v.astype(jnp.int32) for k, v in tabs.items()}, blocks


MOE_BLOCK_ROWS = 256
TOKEN_TILE = 256
MERGE_TILE = 512


def kernel(x, norm_mix_g, w_in, gla_gate_up, gla_gate_bias, gla_norm_g, w_branch_gla, w_branch_att, w_mix_out,
           norm_ffn_g, router_w, router_b, expert_w_gate_up, expert_b_gate_up, expert_w_down, expert_b_down,
           norm_final_g):
    batch, seq, d = x.shape
    assert norm_mix_g.shape[0] == 1, "single layer"
    n = batch * seq
    gla_k = gla_gate_up.shape[2]
    gla_v = w_branch_gla.shape[1]
    att_w = len(ATT_GROUPS) * ATT_HEADS_PER_GROUP * ATT_HEAD_DIM
    hw = ATT_HEADS_PER_GROUP * ATT_HEAD_DIM
    ngrp = len(ATT_GROUPS)
    assert att_w == ngrp * hw and all(s % hw == 0 for s in (gla_k, gla_v, d))
    assert ATT_GROUPS[0][1] == 1 and all(win // dil == ATT_BLOCK for win, dil in ATT_GROUPS)
    src, off = {}, 0
    for name, size in (('gq', gla_k), ('gk', gla_k), ('gv', gla_v), ('gr', gla_v), ('glr', GLA_GATE_RANK),
                       ('aq', att_w), ('ak', att_w), ('av', att_w), ('gate_gla', d), ('gate_att', d)):
        src[name] = (off, size)
        off += size
    assert off == w_in.shape[2]
    wt = w_in[0].T
    glr_at, glr_n = src['glr']
    assert glr_n == GLA_GATE_RANK

    def col(name, group=0):
        return src[name][0] + group * hw

    plain = (('gq', gla_k), ('gk', gla_k), ('gv', gla_v), ('gr', gla_v), ('gate_gla', d), ('gate_att', d),
             ('aq', hw), ('ak', hw), ('av', hw))
    layout, tile_cols = {}, []
    for name, size in plain:
        layout[name] = len(tile_cols) * hw
        tile_cols += [col(name) + t * hw for t in range(size // hw)]
    n_plain = len(tile_cols)
    tile_cols += [col(kind, g) for g in range(1, ngrp) for kind in ('aq', 'ak', 'av')]
    dilations = [dil for _, dil in ATT_GROUPS[1:]]

    x2 = x.reshape(n, d)
    proj, glr, *qkv_dil = _in_projection(x2, norm_mix_g[0], wt, glr_at, batch, seq, n_plain, dilations,
                                         tuple(tile_cols))
    y_gla = _gla(proj, glr, gla_gate_up[0], gla_gate_bias[0], gla_norm_g[0], batch, seq, layout)
    att = [_dilated_attention(proj.reshape(batch, 1, seq, proj.shape[1]), 1, layout['aq'])]
    att += [_dilated_attention(a, dil, 0) for a, dil in zip(qkv_dil, dilations)]
    tt = min(TOKEN_TILE, n)
    h, hn, logits = _merge(
        x2, y_gla, att, proj, layout, w_branch_gla[0].astype(BF16), w_branch_att[0].astype(BF16),
        w_mix_out[0].astype(BF16), norm_ffn_g[0], router_w[0], router_b[0], seq, min(MERGE_TILE, seq))
    gates, loc, cum, cnt = _route(logits, tt)

    bm = MOE_BLOCK_ROWS
    rloc = TOP_K * tt
    nblk = -(-(TOP_K * n) // bm) + N_EXPERTS
    tabs, blocks = _routing_tables(cum[:, :, 0], cnt[:, 0], bm, nblk)
    x_lin = _dispatch(hn, loc, tabs, blocks[1], nblk * bm, rloc)
    y_lin = _experts(x_lin, *blocks, expert_w_gate_up[0], expert_b_gate_up[0], expert_w_down[0],
                     expert_b_down[0], bm)
    out = _combine(h, y_lin, loc, gates, tabs, norm_final_g, rloc)
    return out.reshape(batch, seq, d)
```

```python
import functools

import jax
import jax.numpy as jnp
import numpy as np
from jax import lax
from jax.experimental import pallas as pl
from jax.experimental.pallas import tpu as pltpu

GLA_HEADS = 4
GLA_GATE_RANK = 16
GLA_TAU = 16.0
GLA_CHUNK = 64
ATT_GROUPS = ((128, 1), (512, 4), (2048, 16))
ATT_HEADS_PER_GROUP = 8
ATT_HEAD_DIM = 64
ATT_BLOCK = 128
ROT_DIM = ATT_HEAD_DIM // 4
ROPE_THETA = 500000.0
N_EXPERTS = 32
TOP_K = 4
SWIGLU_LIMIT = 7.0
SWIGLU_ALPHA = 1.702
EPS = 1e-5

LANES = 128
MXU_DIM = 256
VMEM_LIMIT = 56 * 1024 * 1024

F32 = jnp.float32
BF16 = jnp.bfloat16


def _cparams(sem):
    return pltpu.CompilerParams(dimension_semantics=sem, vmem_limit_bytes=VMEM_LIMIT)


def _dot(a, b):
    return jnp.dot(a, b, preferred_element_type=F32)


def _dot_nt(a, b):
    return lax.dot_general(a, b, (((1,), (1,)), ((), ())), preferred_element_type=F32)


def _dot_tn(a, b):
    return lax.dot_general(a, b, (((0,), (0,)), ((), ())), preferred_element_type=F32)


def _split2(a):
    hi = a.astype(BF16)
    lo = (a - hi.astype(F32)).astype(BF16)
    return hi, lo


def _inproj_kernel(x_ref, g_ref, wt_hbm, wglr_ref, cos_ref, sa_ref, sb_ref, proj_ref, glr_ref, *rest,
                   n_plain, dilations, q_scale, tn, tile_rows):
    dil_refs, (xn_ref, stage_ref, w_ref, wst_ref, wsem) = rest[:len(dilations)], rest[len(dilations):]
    nslab = tn // LANES

    @pl.when(pl.program_id(0) == 0)
    def _():
        def window(j):
            return pltpu.make_async_copy(wt_hbm.at[pl.ds(tile_rows[j], tn)], wst_ref.at[j % 2], wsem.at[j % 2])

        window(0).start()
        for j in range(len(tile_rows)):
            if j + 1 < len(tile_rows):
                window(j + 1).start()
            window(j).wait()
            for s in range(nslab):
                w_ref[:, j * tn + s * LANES:j * tn + (s + 1) * LANES] = \
                    jnp.transpose(wst_ref[j % 2, s * LANES:(s + 1) * LANES, :]).astype(BF16)

    x = x_ref[...]
    y = x * lax.rsqrt(jnp.mean(x * x, axis=-1, keepdims=True) + EPS) * g_ref[...]
    xn_ref[...] = y.astype(BF16)
    yh, yl = _split2(y)
    wh, wl = _split2(wglr_ref[...])
    glr_ref[...] = _dot_nt(yh, wh) + _dot_nt(yh, wl) + _dot_nt(yl, wh)

    tm = xn_ref.shape[0]
    half = ROT_DIM // 2

    def slabs(j, kind):
        acc = _dot(xn_ref[...], w_ref[:, j * tn:(j + 1) * tn])
        for s in range(nslab):
            t = acc[:, s * LANES:(s + 1) * LANES]
            if kind < 2:
                t = t * cos_ref[...] + pltpu.roll(t, LANES - half, 1) * sa_ref[...] \
                    + pltpu.roll(t, half, 1) * sb_ref[...]
                if kind == 0:
                    t = t * q_scale
            yield s, t

    for gi, dil in enumerate(dilations):
        for kind in range(3):
            j = n_plain + 3 * gi + kind
            buf = (3 * gi + kind) % stage_ref.shape[0]
            for s, t in slabs(j, kind):
                stage_ref[buf, s] = t
            for r in range(dil):
                for s in range(nslab):
                    dil_refs[gi][0, r, :, kind * tn + s * LANES:kind * tn + (s + 1) * LANES] = \
                        stage_ref[buf, s, pl.ds(r, tm // dil, stride=dil), :].astype(BF16)

    q0 = n_plain - 3
    for j in range(n_plain):
        for s, t in slabs(j, j - q0 if j >= q0 else 2):
            proj_ref[:, j * tn + s * LANES:j * tn + (s + 1) * LANES] = t.astype(BF16)


def _rope_tables(seq):
    half = ROT_DIM // 2
    inv_freq = ROPE_THETA ** (-np.arange(half, dtype=np.float32) * np.float32(2.0 / ROT_DIM))
    pos = jnp.arange(seq, dtype=F32)
    ang = pos[:, None] * jnp.asarray(inv_freq, F32)[None, :]
    cos = jnp.cos(ang)
    sin = jnp.sin(ang)
    lane = np.arange(LANES) % ATT_HEAD_DIM
    idx = np.where(lane < ROT_DIM, lane % half, 0)
    in_rot = jnp.asarray(lane < ROT_DIM)
    first = jnp.asarray(lane < half)
    second = jnp.asarray((lane >= half) & (lane < ROT_DIM))
    cos_t = jnp.where(in_rot[None, :], cos[:, idx], 1.0)
    sin_g = sin[:, idx]
    sa = jnp.where(first[None, :], -sin_g, 0.0)
    sb = jnp.where(second[None, :], sin_g, 0.0)
    return cos_t.astype(F32), sa.astype(F32), sb.astype(F32)


def _in_projection(x2, norm_g, wt, glr_row, batch, seq, n_plain, dilations, tile_rows):
    n, d = x2.shape
    tn = ATT_HEADS_PER_GROUP * ATT_HEAD_DIM
    tm = min(512, seq)
    ntile = len(tile_rows)
    assert n % tm == 0 and seq % tm == 0 and ntile == n_plain + 3 * len(dilations)
    assert all(tm % (dil * 16) == 0 for dil in dilations)
    assert glr_row % GLA_GATE_RANK == 0 and all(r % 8 == 0 and r + tn <= wt.shape[0] for r in tile_rows)
    cos_t, sa, sb = _rope_tables(seq)
    spt = seq // tm
    kern = functools.partial(_inproj_kernel, n_plain=n_plain, dilations=tuple(dilations),
                             q_scale=ATT_HEAD_DIM ** -0.5, tn=tn, tile_rows=tuple(tile_rows))
    out_specs = [
        pl.BlockSpec((tm, n_plain * tn), lambda i: (i, 0)),
        pl.BlockSpec((tm, GLA_GATE_RANK), lambda i: (i, 0)),
    ]
    out_shape = [
        jax.ShapeDtypeStruct((n, n_plain * tn), BF16),
        jax.ShapeDtypeStruct((n, GLA_GATE_RANK), F32),
    ]
    for dil in dilations:
        out_specs.append(pl.BlockSpec((1, dil, tm // dil, 3 * tn), lambda i: (i // spt, 0, i % spt, 0)))
        out_shape.append(jax.ShapeDtypeStruct((batch, dil, seq // dil, 3 * tn), BF16))
    const = lambda i: (0, 0)
    return pl.pallas_call(
        kern,
        grid=(n // tm,),
        in_specs=[
            pl.BlockSpec((tm, d), lambda i: (i, 0)),
            pl.BlockSpec((1, d), const),
            pl.BlockSpec(memory_space=pl.ANY),
            pl.BlockSpec((GLA_GATE_RANK, d), lambda i: (glr_row // GLA_GATE_RANK, 0)),
            pl.BlockSpec((tm, LANES), lambda i: (i % spt, 0)),
            pl.BlockSpec((tm, LANES), lambda i: (i % spt, 0)),
            pl.BlockSpec((tm, LANES), lambda i: (i % spt, 0)),
        ],
        out_specs=out_specs,
        out_shape=out_shape,
        scratch_shapes=[pltpu.VMEM((tm, d), BF16), pltpu.VMEM((2, tn // LANES, tm, LANES), F32),
                        pltpu.VMEM((d, ntile * tn), BF16),
                        pltpu.VMEM((2, tn, d), F32), pltpu.SemaphoreType.DMA((2,))],
        compiler_params=_cparams(("arbitrary",)),
        name="in_projection",
    )(x2, norm_g.reshape(1, d), wt, wt, cos_t, sa, sb)


def _gla_kernel(q_ref, k_ref, v_ref, r_ref, glr_ref, up_ref, bias_ref, ng_ref, tri_ref, y_ref, state_ref,
                *, dk, dv, ts):
    t = pl.program_id(2)

    @pl.when(t == 0)
    def _():
        state_ref[...] = jnp.zeros_like(state_ref)

    c = GLA_CHUNK
    hps = state_ref.shape[0]
    gh, gl = _split2(glr_ref[...])
    uh, ul = _split2(up_ref[...])
    z = _dot(gh, uh) + _dot(gh, ul) + _dot(gl, uh) + bias_ref[...]
    logdec = (jnp.minimum(z, 0.0) - jnp.log1p(jnp.exp(-jnp.abs(z)))) * (1.0 / GLA_TAU)
    ldh, ldl = _split2(logdec)
    tri = tri_ref[...]
    row = lax.broadcasted_iota(jnp.int32, (c, c), 0)
    col = lax.broadcasted_iota(jnp.int32, (c, c), 1)
    causal = col <= row
    scale = dk ** -0.5
    ng = ng_ref[...]
    states = [state_ref[hh] for hh in range(hps)]
    for ci in range(ts // c):
        sl = slice(ci * c, (ci + 1) * c)
        b2 = _dot(tri, ldh[sl]) + _dot(tri, ldl[sl])
        for hh in range(hps):
            ks = slice(hh * dk, (hh + 1) * dk)
            vs = slice(hh * dv, (hh + 1) * dv)
            b = b2[:, ks]
            bl = b[c - 1:c, :]
            eb = jnp.exp(b)
            enb = jnp.exp(-b)
            ebl = jnp.exp(bl)
            qf = q_ref[sl, ks].astype(F32)
            kf = k_ref[sl, ks].astype(F32)
            qe = (qf * scale * eb).astype(BF16)
            ke = (kf * enb).astype(BF16)
            kd = (kf * enb * ebl).astype(BF16)
            vb = v_ref[sl, vs]
            a = jnp.where(causal, _dot_nt(qe, ke), 0.0).astype(BF16)
            o = _dot(a, vb) + _dot(qe, states[hh].astype(BF16))
            dec = jnp.transpose(jnp.broadcast_to(ebl, (dk, dk)))
            dec_full = jnp.concatenate([dec] * (dv // dk), axis=1)
            states[hh] = dec_full * states[hh] + _dot_tn(kd, vb)
            o = o * lax.rsqrt(jnp.mean(o * o, axis=-1, keepdims=True) + EPS) * ng
            rf = r_ref[sl, vs].astype(F32)
            o = o * (rf * jax.nn.sigmoid(rf))
            y_ref[sl, vs] = o.astype(BF16)
    for hh in range(hps):
        state_ref[hh] = states[hh]


def _gla(proj, glr, gate_up, gate_bias, norm_g, batch, seq, layout):
    n = proj.shape[0]
    dk = (layout['gk'] - layout['gq']) // GLA_HEADS
    dv = (layout['gr'] - layout['gv']) // GLA_HEADS
    ts = min(512, seq)
    assert dk == LANES and dv % dk == 0 and seq % ts == 0 and ts % GLA_CHUNK == 0
    spt = seq // ts
    hps = GLA_HEADS
    wk, wv = hps * dk, hps * dv
    assert GLA_HEADS % hps == 0 and all(layout[s] % wk == 0 for s in ('gq', 'gk'))
    assert all(layout[s] % wv == 0 for s in ('gv', 'gr'))
    qb, kb = layout['gq'] // wk, layout['gk'] // wk
    vb, rb = layout['gv'] // wv, layout['gr'] // wv
    tri = jnp.asarray(np.tril(np.ones((GLA_CHUNK, GLA_CHUNK), np.float32)), BF16)
    kern = functools.partial(_gla_kernel, dk=dk, dv=dv, ts=ts)
    row = lambda b, h, t: b * spt + t
    return pl.pallas_call(
        kern,
        grid=(batch, GLA_HEADS // hps, spt),
        in_specs=[
            pl.BlockSpec((ts, wk), lambda b, h, t: (row(b, h, t), qb + h)),
            pl.BlockSpec((ts, wk), lambda b, h, t: (row(b, h, t), kb + h)),
            pl.BlockSpec((ts, wv), lambda b, h, t: (row(b, h, t), vb + h)),
            pl.BlockSpec((ts, wv), lambda b, h, t: (row(b, h, t), rb + h)),
            pl.BlockSpec((ts, GLA_GATE_RANK), lambda b, h, t: (row(b, h, t), 0)),
            pl.BlockSpec((GLA_GATE_RANK, wk), lambda b, h, t: (0, h)),
            pl.BlockSpec((1, wk), lambda b, h, t: (0, h)),
            pl.BlockSpec((1, dv), lambda b, h, t: (0, 0)),
            pl.BlockSpec((GLA_CHUNK, GLA_CHUNK), lambda b, h, t: (0, 0)),
        ],
        out_specs=pl.BlockSpec((ts, wv), lambda b, h, t: (row(b, h, t), h)),
        out_shape=jax.ShapeDtypeStruct((n, GLA_HEADS * dv), BF16),
        scratch_shapes=[pltpu.VMEM((hps, dk, dv), F32)],
        compiler_params=_cparams(("arbitrary", "arbitrary", "arbitrary")),
        name="gla",
    )(proj, proj, proj, proj, glr, gate_up, gate_bias.reshape(1, -1), norm_g.reshape(1, -1), tri)


LSE_REP = LANES // ATT_HEADS_PER_GROUP


def _att_kernel(q_ref, k_ref, v_ref, o_ref, lse_ref, *, nb, pairs, unroll):
    blk = ATT_BLOCK
    hd = ATT_HEAD_DIM
    row = lax.broadcasted_iota(jnp.int32, (2 * blk, 2 * blk), 0) % blk
    col = lax.broadcasted_iota(jnp.int32, (2 * blk, 2 * blk), 1)
    dist = row + blk - col
    band = jnp.logical_and(dist >= 0, dist <= blk)
    causal = band[:, blk:]
    lane = lax.broadcasted_iota(jnp.int32, (blk, LANES), 1)
    head0 = lane < hd
    lane_head = lane // LSE_REP
    neg = -jnp.inf

    def block(j, first):
        r0 = pl.multiple_of(j * blk, blk)
        lse = jnp.zeros((blk, LANES), F32)
        for p in range(pairs):
            cs = slice(p * LANES, (p + 1) * LANES)
            q2 = q_ref[0, 0, pl.ds(r0, blk), cs]
            zero = jnp.zeros_like(q2)
            qs = jnp.concatenate([jnp.where(head0, q2, zero), jnp.where(head0, zero, q2)], axis=0)
            if first:
                kk = k_ref[0, 0, pl.ds(r0, blk), cs]
                vv = v_ref[0, 0, pl.ds(r0, blk), cs]
                s = jnp.where(causal, _dot_nt(qs, kk), neg)
            else:
                k0 = pl.multiple_of(r0 - blk, blk)
                kk = k_ref[0, 0, pl.ds(k0, 2 * blk), cs]
                vv = v_ref[0, 0, pl.ds(k0, 2 * blk), cs]
                s = jnp.where(band, _dot_nt(qs, kk), neg)
            mx = jnp.max(s, axis=-1, keepdims=True)
            e = jnp.exp(s - mx).astype(BF16)
            one = jnp.ones_like(vv)
            hv = lax.broadcasted_iota(jnp.int32, vv.shape, 1) < hd
            out0 = _dot(e[:blk], jnp.where(hv, vv, one))
            out1 = _dot(e[blk:], jnp.where(hv, one, vv))
            num = jnp.where(head0, out0, out1)
            den_x = jnp.where(head0, out1, out0)
            den = pltpu.roll(den_x, hd, 1)
            o_ref[0, 0, pl.ds(r0, blk), cs] = (num / den).astype(o_ref.dtype)
            lse0 = mx[:blk] + jnp.log(jnp.where(head0, den, den_x))
            lse1 = mx[blk:] + jnp.log(jnp.where(head0, den_x, den))
            lse = jnp.where(lane_head == 2 * p, lse0, jnp.where(lane_head == 2 * p + 1, lse1, lse))
        lse_ref[0, 0, pl.ds(r0, blk), :] = lse

    block(0, True)

    def body(j, carry):
        block(j, False)
        return carry

    lax.fori_loop(1, nb, body, 0, unroll=unroll)


def _dilated_attention(qkv, dilation, col0):
    batch, dil, sub, width = qkv.shape
    hw = ATT_HEADS_PER_GROUP * ATT_HEAD_DIM
    assert dil == dilation and sub % ATT_BLOCK == 0 and col0 % hw == 0
    nb = sub // ATT_BLOCK
    pairs = hw // LANES
    assert 2 * pairs * LSE_REP == LANES
    kern = functools.partial(_att_kernel, nb=nb, pairs=pairs, unroll=2 if nb > 2 else 1)

    def in_spec(kind):
        base = (col0 + kind * hw) // hw
        return pl.BlockSpec((1, 1, sub, hw), lambda b, r: (b, r, 0, base))

    return pl.pallas_call(
        kern,
        grid=(batch, dil),
        in_specs=[in_spec(0), in_spec(1), in_spec(2)],
        out_specs=[pl.BlockSpec((1, 1, sub, hw), lambda b, r: (b, r, 0, 0)),
                   pl.BlockSpec((1, 1, sub, LANES), lambda b, r: (b, r, 0, 0))],
        out_shape=[
            jax.ShapeDtypeStruct((batch, dil, sub, hw), BF16),
            jax.ShapeDtypeStruct((batch, dil, sub, LANES), F32),
        ],
        compiler_params=_cparams(("arbitrary", "arbitrary")),
        name=f"dilated_attention_d{dilation}",
    )(qkv, qkv, qkv)


def _merge_kernel(x_ref, yg_ref, o1_ref, o2_ref, o3_ref, l1_ref, l2_ref, l3_ref,
                  gg0_ref, gg1_ref, ga0_ref, ga1_ref, wbg_ref, wba_ref, wmo_ref, nf_ref, rw_ref, rb_ref,
                  spread_ref, h_ref, hn_ref, lg_ref, os_ref, ls_ref):
    def token_order(src_ref, stage_ref):
        dil, sub, w = src_ref.shape[1:]
        if dil == 1:
            return src_ref[0, 0].astype(F32)
        for r in range(dil):
            val = src_ref[0, r].astype(F32)
            for s in range(w // LANES):
                stage_ref[s, pl.ds(r, sub, stride=dil), :] = val[:, s * LANES:(s + 1) * LANES]
        return jnp.concatenate([stage_ref[s] for s in range(w // LANES)], axis=1)

    l1, o1 = token_order(l1_ref, None), token_order(o1_ref, None)
    l2, o2 = token_order(l2_ref, ls_ref.at[0]), token_order(o2_ref, os_ref.at[0])
    l3, o3 = token_order(l3_ref, ls_ref.at[1]), token_order(o3_ref, os_ref.at[1])
    m = jnp.maximum(jnp.maximum(l1, l2), l3)
    w1, w2, w3 = jnp.exp(l1 - m), jnp.exp(l2 - m), jnp.exp(l3 - m)
    inv = 1.0 / (w1 + w2 + w3)
    spread = spread_ref[...]

    def per_lane(w):
        hi, lo = _split2(w * inv)
        return _dot(hi, spread) + _dot(lo, spread)

    y_att = per_lane(w1) * o1 + per_lane(w2) * o2 + per_lane(w3) * o3

    m_gla = _dot(yg_ref[...], wbg_ref[...])
    m_att = _dot(y_att.astype(BF16), wba_ref[...])
    gg = jnp.concatenate([gg0_ref[...], gg1_ref[...]], axis=1).astype(F32)
    ga = jnp.concatenate([ga0_ref[...], ga1_ref[...]], axis=1).astype(F32)
    merged = (0.5 + 0.5 * jnp.tanh(0.5 * gg)) * m_gla + (0.5 + 0.5 * jnp.tanh(0.5 * ga)) * m_att
    h = x_ref[...] + _dot(merged.astype(BF16), wmo_ref[...])
    h_ref[...] = h
    hn = h * lax.rsqrt(jnp.mean(h * h, axis=-1, keepdims=True) + EPS) * nf_ref[...]
    hn_ref[...] = hn.astype(BF16)

    hh, hl = _split2(hn)
    rh, rl = _split2(rw_ref[...])
    lg_ref[...] = _dot_nt(rh, hh) + _dot_nt(rh, hl) + _dot_nt(rl, hh) + rb_ref[...]


def _route_kernel(lg_ref, triu_ref, ltri_ref, tg_ref, lc_ref, cum_ref, cnt_ref, carry_ref):
    ne = N_EXPERTS
    tiles, _, tt = tg_ref.shape

    @pl.when(pl.program_id(0) == 0)
    def _():
        carry_ref[...] = jnp.zeros_like(carry_ref)

    eid = lax.broadcasted_iota(jnp.int32, (ne, tt), 0)
    carry = carry_ref[...]
    for t in range(tiles):
        logit = lg_ref[:, t * tt:(t + 1) * tt]
        member = jnp.zeros((ne, tt), jnp.bool_)
        vals, idxs = [], []
        for _ in range(TOP_K):
            mx = jnp.max(logit, axis=0, keepdims=True)
            idx = jnp.min(jnp.where(logit == mx, eid, ne), axis=0, keepdims=True)
            sel = eid == idx
            member = jnp.logical_or(member, sel)
            logit = jnp.where(sel, -jnp.inf, logit)
            vals.append(mx)
            idxs.append(idx)
        ex = [jnp.exp(v - vals[0]) for v in vals]
        tot = ex[0] + ex[1] + ex[2] + ex[3]
        memf = jnp.where(member, 1.0, 0.0)
        lens = jnp.sum(memf, axis=1, keepdims=True)
        lens_b = jnp.broadcast_to(lens, (ne, LANES))
        off = _dot(ltri_ref[...], lens_b.astype(BF16))
        pos = _dot(memf.astype(BF16), triu_ref[...]) + off[:, 0:1]
        for k in range(TOP_K):
            tg_ref[t, k:k + 1, :] = ex[k] / tot
            lc_ref[t, k:k + 1, :] = jnp.sum(jnp.where(eid == idxs[k], pos, 0.0), axis=0,
                                            keepdims=True).astype(jnp.int32)
        cum_ref[t] = carry.astype(jnp.int32)
        carry = carry + lens_b
    carry_ref[...] = carry
    cnt_ref[...] = carry.astype(jnp.int32)


def _route(logits, tt):
    ne, n = logits.shape
    nt = n // tt
    assert tt <= 256, "per-tile expert counts go through a bf16 matmul operand: exact up to 256"
    tiles = 8 if nt % 8 == 0 else 1
    triu = jnp.asarray(np.triu(np.ones((tt, tt), np.float32), 1), BF16)
    ltri = jnp.asarray(np.tril(np.ones((ne, ne), np.float32), -1), BF16)
    const = lambda i: (0, 0)
    return pl.pallas_call(
        _route_kernel,
        grid=(nt // tiles,),
        in_specs=[pl.BlockSpec((ne, tiles * tt), lambda i: (0, i)),
                  pl.BlockSpec((tt, tt), const), pl.BlockSpec((ne, ne), const)],
        out_specs=[
            pl.BlockSpec((tiles, TOP_K, tt), lambda i: (i, 0, 0)),
            pl.BlockSpec((tiles, TOP_K, tt), lambda i: (i, 0, 0)),
            pl.BlockSpec((tiles, ne, LANES), lambda i: (i, 0, 0)),
            pl.BlockSpec((ne, LANES), const),
        ],
        out_shape=[
            jax.ShapeDtypeStruct((nt, TOP_K, tt), F32),
            jax.ShapeDtypeStruct((nt, TOP_K, tt), jnp.int32),
            jax.ShapeDtypeStruct((nt, ne, LANES), jnp.int32),
            jax.ShapeDtypeStruct((ne, LANES), jnp.int32),
        ],
        scratch_shapes=[pltpu.VMEM((ne, LANES), F32)],
        compiler_params=_cparams(("arbitrary",)),
        name="route_topk",
    )(logits, triu, ltri)


def _merge(x2, y_gla, att, proj, layout, wbg, wba, wmo, norm_ffn_g, router_w, router_b, seq, tm):
    n, d = x2.shape
    hw = ATT_HEADS_PER_GROUP * ATT_HEAD_DIM
    half = d // 2
    assert n % tm == 0 and seq % tm == 0
    assert layout['gate_gla'] % half == 0 and layout['gate_att'] % half == 0
    assert all(tm % (o.shape[1] * 16) == 0 for o, _ in att) and att[0][0].shape[1] == 1
    nt = n // tm
    spt = seq // tm
    ggb = layout['gate_gla'] // half
    gab = layout['gate_att'] // half
    row = lambda i: (i, 0)
    const = lambda i: (0, 0)

    def att_spec(a):
        dil = a.shape[1]
        return pl.BlockSpec((1, dil, tm // dil, a.shape[3]), lambda i: (i // spt, 0, i % spt, 0))

    (o1, l1), (o2, l2), (o3, l3) = att
    in_specs = [
        pl.BlockSpec((tm, d), row),
        pl.BlockSpec((tm, y_gla.shape[1]), row),
        att_spec(o1), att_spec(o2), att_spec(o3), att_spec(l1), att_spec(l2), att_spec(l3),
        pl.BlockSpec((tm, half), lambda i: (i, ggb)), pl.BlockSpec((tm, half), lambda i: (i, ggb + 1)),
        pl.BlockSpec((tm, half), lambda i: (i, gab)), pl.BlockSpec((tm, half), lambda i: (i, gab + 1)),
        pl.BlockSpec(wbg.shape, const), pl.BlockSpec(wba.shape, const), pl.BlockSpec(wmo.shape, const),
        pl.BlockSpec((1, d), const),
        pl.BlockSpec((N_EXPERTS, d), const),
        pl.BlockSpec((N_EXPERTS, 1), const),
        pl.BlockSpec((LANES, hw), const),
    ]
    out_specs = [
        pl.BlockSpec((tm, d), row),
        pl.BlockSpec((tm, d), row),
        pl.BlockSpec((N_EXPERTS, tm), lambda i: (0, i)),
    ]
    out_shape = [
        jax.ShapeDtypeStruct((n, d), F32),
        jax.ShapeDtypeStruct((n, d), BF16),
        jax.ShapeDtypeStruct((N_EXPERTS, n), F32),
    ]
    spread = jnp.asarray(np.arange(LANES)[:, None] == (np.arange(hw)[None, :] // ATT_HEAD_DIM) * LSE_REP, BF16)
    return pl.pallas_call(
        _merge_kernel,
        grid=(nt,),
        in_specs=in_specs,
        out_specs=out_specs,
        out_shape=out_shape,
        scratch_shapes=[pltpu.VMEM((2, hw // LANES, tm, LANES), F32), pltpu.VMEM((2, 1, tm, LANES), F32)],
        compiler_params=_cparams(("arbitrary",)),
        name="merge_router",
    )(x2, y_gla, o1, o2, o3, l1, l2, l3, proj, proj, proj, proj, wbg, wba, wmo,
      norm_ffn_g.reshape(1, d), router_w.T, router_b.reshape(N_EXPERTS, 1), spread)


LIN_SUB = 8


def _lin_pack(val):
    return [val[:, c * LANES:(c + 1) * LANES] for c in range(LIN_SUB)]


def _lin_unpack(slabs):
    return jnp.concatenate(slabs, axis=1)


def _dispatch_kernel(off_ref, len_ref, dst_ref, zdst_ref, zlen_ref, nact_ref, hn_ref, loc_ref, x_hbm,
                     xs_ref, zero_ref, sem, zsem, *, rloc):
    i = pl.program_id(0)
    nt = pl.num_programs(0)
    slot = i % 2
    ne = N_EXPERTS

    def drain(s):
        rows = rloc * LIN_SUB
        pltpu.make_async_copy(xs_ref.at[s], x_hbm.at[pl.ds(0, rows)], sem.at[s]).wait()

    @pl.when(i == 0)
    def _():
        zero_ref[...] = jnp.zeros_like(zero_ref)

        def fill_copy(e):
            rows = zlen_ref[e] * LIN_SUB
            dst = pl.multiple_of(zdst_ref[e] * LIN_SUB, 8)
            return rows, pltpu.make_async_copy(zero_ref.at[pl.ds(0, rows)], x_hbm.at[pl.ds(dst, rows)], zsem)

        def fill(e, carry):
            rows, copy = fill_copy(e)
            pl.when(rows > 0)(copy.start)
            return carry

        def fill_done(e, carry):
            rows, copy = fill_copy(e)
            pl.when(rows > 0)(copy.wait)
            return carry

        def block_copy(b):
            rows = zero_ref.shape[0]
            return pltpu.make_async_copy(zero_ref, x_hbm.at[pl.ds(pl.multiple_of(b * rows, rows), rows)], zsem)

        def fill_block(b, carry):
            block_copy(b).start()
            return carry

        def fill_block_done(b, carry):
            block_copy(b).wait()
            return carry

        nblk = x_hbm.shape[0] // zero_ref.shape[0]
        lax.fori_loop(0, ne, fill, 0)
        lax.fori_loop(nact_ref[0], nblk, fill_block, 0)
        lax.fori_loop(0, ne, fill_done, 0)
        lax.fori_loop(nact_ref[0], nblk, fill_block_done, 0)

    @pl.when(i >= 2)
    def _():
        drain(slot)

    loc = loc_ref[0]
    tt = loc.shape[1]
    row_id = lax.broadcasted_iota(jnp.int32, (rloc, tt), 0)
    hit = row_id == loc[0:1, :]
    for k in range(1, TOP_K):
        hit = jnp.logical_or(hit, row_id == loc[k:k + 1, :])
    onehot = jnp.where(hit, 1.0, 0.0).astype(BF16)
    xs = _dot(onehot, hn_ref[...])
    for c, slab in enumerate(_lin_pack(xs)):
        xs_ref[slot, pl.ds(c, rloc, stride=LIN_SUB), :] = slab

    def send(e, carry):
        rows = len_ref[i * ne + e] * LIN_SUB

        @pl.when(rows > 0)
        def _():
            src = pl.multiple_of(off_ref[i * ne + e] * LIN_SUB, 8)
            dst = pl.multiple_of(dst_ref[i * ne + e] * LIN_SUB, 8)
            pltpu.make_async_copy(xs_ref.at[slot, pl.ds(src, rows)], x_hbm.at[pl.ds(dst, rows)], sem.at[slot]).start()
        return carry

    lax.fori_loop(0, ne, send, 0)

    @pl.when(i == nt - 1)
    def _():
        @pl.when(i >= 1)
        def _():
            drain(1 - slot)
        drain(slot)


def _dispatch(hn, loc, tabs, nact, p_rows, rloc):
    n, d = hn.shape
    nt, _, tt = loc.shape
    assert d == LIN_SUB * LANES and p_rows % MOE_BLOCK_ROWS == 0
    kern = functools.partial(_dispatch_kernel, rloc=rloc)
    grid_spec = pltpu.PrefetchScalarGridSpec(
        num_scalar_prefetch=6,
        grid=(nt,),
        in_specs=[
            pl.BlockSpec((tt, d), lambda i, *_: (i, 0)),
            pl.BlockSpec((1, TOP_K, tt), lambda i, *_: (i, 0, 0)),
        ],
        out_specs=pl.BlockSpec(memory_space=pl.ANY),
        scratch_shapes=[
            pltpu.VMEM((2, rloc * LIN_SUB, LANES), F32),
            pltpu.VMEM((MOE_BLOCK_ROWS * LIN_SUB, LANES), F32),
            pltpu.SemaphoreType.DMA((2,)),
            pltpu.SemaphoreType.DMA(()),
        ],
    )
    return pl.pallas_call(
        kern,
        grid_spec=grid_spec,
        out_shape=jax.ShapeDtypeStruct((p_rows * LIN_SUB, LANES), F32),
        compiler_params=_cparams(("arbitrary",)),
        name="moe_dispatch",
    )(tabs['off'], tabs['len'], tabs['dst'], tabs['zdst'], tabs['zlen'], nact, hn, loc)


def _expert_kernel(bexp_ref, nact_ref, first_ref, next_ref, x_ref, wgu_hbm, wd_hbm, bgu_ref, bd_ref, perm_ref,
                   y_ref, wgu_f, wd_f, wgu_s, wd_s, sem, *, bm):
    grp = MXU_DIM
    ff2 = wgu_s.shape[1]
    halfg = grp // 2
    brows = bm * LIN_SUB
    bps = x_ref.shape[0] // brows

    def fetch(ex):
        return (pltpu.make_async_copy(wgu_hbm.at[ex], wgu_f, sem.at[0]),
                pltpu.make_async_copy(wd_hbm.at[ex], wd_f, sem.at[1]))

    @pl.when(pl.program_id(0) == 0)
    def _():
        for copy in fetch(bexp_ref[0]):
            copy.start()

    def block(b, r0):
        e = bexp_ref[b]
        active = b < nact_ref[0]

        @pl.when(jnp.logical_and(active, first_ref[b] == 1))
        def _():
            for copy in fetch(e):
                copy.wait()
            perm = perm_ref[...]
            for g in range(ff2 // grp):
                cols = slice(g * grp, (g + 1) * grp)
                wgu_s[:, cols] = _dot(wgu_f[:, cols].astype(BF16), perm).astype(BF16)
            wd_s[...] = wd_f[...].astype(BF16)
            nxt = next_ref[b]

            @pl.when(nxt >= 0)
            def _():
                for copy in fetch(nxt):
                    copy.start()

        @pl.when(active)
        def _():
            x = _lin_unpack([x_ref[pl.ds(r0 + c, bm, stride=LIN_SUB), :] for c in range(LIN_SUB)]).astype(BF16)
            acts = []
            for g in range(ff2 // grp):
                cols = slice(g * grp, (g + 1) * grp)
                hg = _dot(x, wgu_s[:, cols]) + bgu_ref[e, :, cols]
                gate = jnp.minimum(hg[:, :halfg], SWIGLU_LIMIT)
                up = jnp.clip(hg[:, halfg:], -SWIGLU_LIMIT, SWIGLU_LIMIT)
                acts.append(((up + 1.0) * (gate * jax.nn.sigmoid(SWIGLU_ALPHA * gate))).astype(BF16))
            act = jnp.concatenate(acts, axis=1)
            y = _dot(act, wd_s[...]) + bd_ref[e]
            for c, slab in enumerate(_lin_pack(y)):
                y_ref[pl.ds(r0 + c, bm, stride=LIN_SUB), :] = slab

        @pl.when(jnp.logical_not(active))
        def _():
            y_ref[pl.ds(r0, brows), :] = jnp.zeros((brows, y_ref.shape[1]), y_ref.dtype)

    for sub in range(bps):
        block(pl.program_id(0) * bps + sub, sub * brows)


def _gate_up_permutation():
    grp = MXU_DIM
    halfg = grp // 2
    out = np.arange(grp)
    src = np.where(out < halfg, 2 * out, 2 * (out - halfg) + 1)
    p = np.zeros((grp, grp), np.float32)
    p[src, out] = 1.0
    return jnp.asarray(p, BF16)


def _experts(x_lin, bexp, nact, first, nxt, w_gate_up, b_gate_up, w_down, b_down, bm):
    ne, d, ff2 = w_gate_up.shape
    ff = ff2 // 2
    grp = MXU_DIM
    assert ff2 % grp == 0 and d == LIN_SUB * LANES
    nblk = x_lin.shape[0] // (bm * LIN_SUB)
    bps = max(b for b in (4, 2, 1) if nblk % b == 0)
    srows = bps * bm * LIN_SUB
    bgu = b_gate_up.reshape(ne, ff2 // grp, grp // 2, 2).transpose(0, 1, 3, 2).reshape(ne, 1, ff2)
    grid_spec = pltpu.PrefetchScalarGridSpec(
        num_scalar_prefetch=4,
        grid=(nblk // bps,),
        in_specs=[
            pl.BlockSpec((srows, LANES), lambda i, *_: (i, 0)),
            pl.BlockSpec(memory_space=pl.ANY),
            pl.BlockSpec(memory_space=pl.ANY),
            pl.BlockSpec(memory_space=pltpu.VMEM),
            pl.BlockSpec(memory_space=pltpu.VMEM),
            pl.BlockSpec((grp, grp), lambda i, *_: (0, 0)),
        ],
        out_specs=pl.BlockSpec((srows, LANES), lambda i, *_: (i, 0)),
        scratch_shapes=[pltpu.VMEM((d, ff2), F32), pltpu.VMEM((ff, d), F32),
                        pltpu.VMEM((d, ff2), BF16), pltpu.VMEM((ff, d), BF16),
                        pltpu.SemaphoreType.DMA((2,))],
    )
    return pl.pallas_call(
        functools.partial(_expert_kernel, bm=bm),
        grid_spec=grid_spec,
        out_shape=jax.ShapeDtypeStruct(x_lin.shape, F32),
        compiler_params=_cparams(("arbitrary",)),
        name="moe_experts",
    )(bexp, nact, first, nxt, x_lin, w_gate_up, w_down, bgu, b_down.reshape(ne, 1, d), _gate_up_permutation())


def _combine_kernel(off_ref, len_ref, dst_ref, h_ref, loc_ref, gate_ref, g_ref, y_hbm, out_ref,
                    ybuf, sem, *, rloc):
    i = pl.program_id(0)
    nt = pl.num_programs(0)
    slot = i % 2
    ne = N_EXPERTS

    def fetch(step, s):
        def run(e, carry):
            rows = len_ref[step * ne + e] * LIN_SUB

            @pl.when(rows > 0)
            def _():
                src = pl.multiple_of(dst_ref[step * ne + e] * LIN_SUB, 8)
                dst = pl.multiple_of(off_ref[step * ne + e] * LIN_SUB, 8)
                pltpu.make_async_copy(y_hbm.at[pl.ds(src, rows)], ybuf.at[s, pl.ds(dst, rows)], sem.at[s]).start()
            return carry

        lax.fori_loop(0, ne, run, 0)

    @pl.when(i == 0)
    def _():
        fetch(0, 0)

    @pl.when(i + 1 < nt)
    def _():
        fetch(i + 1, 1 - slot)

    pltpu.make_async_copy(y_hbm.at[pl.ds(0, rloc * LIN_SUB)], ybuf.at[slot], sem.at[slot]).wait()

    y = _lin_unpack([ybuf[slot, pl.ds(c, rloc, stride=LIN_SUB), :] for c in range(LIN_SUB)]).astype(BF16)
    loc = loc_ref[0]
    gate = gate_ref[0]
    tt = loc.shape[1]
    row_id = lax.broadcasted_iota(jnp.int32, (rloc, tt), 0)
    w = jnp.zeros((rloc, tt), F32)
    for k in range(TOP_K):
        w = w + jnp.where(row_id == loc[k:k + 1, :], gate[k:k + 1, :], 0.0)
    h = h_ref[...] + _dot_tn(w.astype(BF16), y)
    out_ref[...] = h * lax.rsqrt(jnp.mean(h * h, axis=-1, keepdims=True) + EPS) * g_ref[...]


def _combine(h, y_lin, loc, gates, tabs, norm_g, rloc):
    n, d = h.shape
    nt, _, tt = loc.shape
    kern = functools.partial(_combine_kernel, rloc=rloc)
    grid_spec = pltpu.PrefetchScalarGridSpec(
        num_scalar_prefetch=3,
        grid=(nt,),
        in_specs=[
            pl.BlockSpec((tt, d), lambda i, *_: (i, 0)),
            pl.BlockSpec((1, TOP_K, tt), lambda i, *_: (i, 0, 0)),
            pl.BlockSpec((1, TOP_K, tt), lambda i, *_: (i, 0, 0)),
            pl.BlockSpec((1, d), lambda i, *_: (0, 0)),
            pl.BlockSpec(memory_space=pl.ANY),
        ],
        out_specs=pl.BlockSpec((tt, d), lambda i, *_: (i, 0)),
        scratch_shapes=[
            pltpu.VMEM((2, rloc * LIN_SUB, LANES), F32),
            pltpu.SemaphoreType.DMA((2,)),
        ],
    )
    return pl.pallas_call(
        kern,
        grid_spec=grid_spec,
        out_shape=jax.ShapeDtypeStruct((n, d), F32),
        compiler_params=_cparams(("arbitrary",)),
        name="moe_combine",
    )(tabs['off'], tabs['len'], tabs['dst'], h, loc, gates, norm_g.reshape(1, d), y_lin)


def _routing_tables(cum, counts, bm, nblk):
    padded = (counts + bm - 1) // bm * bm
    pend = jnp.cumsum(padded)
    pstart = pend - padded
    blk = jnp.arange(nblk, dtype=jnp.int32)
    bexp_raw = jnp.sum((pend // bm)[None, :] <= blk[:, None], axis=1).astype(jnp.int32)
    nact = (pend[-1] // bm).astype(jnp.int32)
    last = jnp.minimum(bexp_raw[jnp.maximum(nact - 1, 0)], N_EXPERTS - 1)
    bexp = jnp.where(blk < nact, jnp.minimum(bexp_raw, N_EXPERTS - 1), last)
    prev = jnp.concatenate([jnp.full((1,), -1, jnp.int32), bexp[:-1]])
    first = jnp.logical_and(bexp != prev, blk < nact)
    later = jnp.logical_and(first[None, :], blk[None, :] > blk[:, None])
    nxt_blk = jnp.min(jnp.where(later, blk[None, :], nblk), axis=1)
    nxt = jnp.where(nxt_blk < nblk, bexp[jnp.minimum(nxt_blk, nblk - 1)], -1)
    cum_full = jnp.concatenate([cum, counts[None, :]], axis=0)
    run_len = cum_full[1:] - cum_full[:-1]
    run_off = jnp.cumsum(run_len, axis=1) - run_len
    tabs = {
        'off': run_off.reshape(-1), 'len': run_len.reshape(-1),
        'dst': (pstart[None, :] + cum_full[:-1]).reshape(-1),
        'zdst': pstart + counts, 'zlen': padded - counts,
    }
    blocks = (bexp, nact.reshape(1), first.astype(jnp.int32), nxt.astype(jnp.int32))
    return {k: v.astype(jnp.int32) for k, v in tabs.items()}, blocks


MOE_BLOCK_ROWS = 256
TOKEN_TILE = 256
MERGE_TILE = 512


def kernel(x, norm_mix_g, w_in, gla_gate_up, gla_gate_bias, gla_norm_g, w_branch_gla, w_branch_att, w_mix_out,
           norm_ffn_g, router_w, router_b, expert_w_gate_up, expert_b_gate_up, expert_w_down, expert_b_down,
           norm_final_g):
    batch, seq, d = x.shape
    assert norm_mix_g.shape[0] == 1, "single layer"
    n = batch * seq
    gla_k = gla_gate_up.shape[2]
    gla_v = w_branch_gla.shape[1]
    att_w = len(ATT_GROUPS) * ATT_HEADS_PER_GROUP * ATT_HEAD_DIM
    hw = ATT_HEADS_PER_GROUP * ATT_HEAD_DIM
    ngrp = len(ATT_GROUPS)
    assert att_w == ngrp * hw and all(s % hw == 0 for s in (gla_k, gla_v, d))
    assert ATT_GROUPS[0][1] == 1 and all(win // dil == ATT_BLOCK for win, dil in ATT_GROUPS)
    src, off = {}, 0
    for name, size in (('gq', gla_k), ('gk', gla_k), ('gv', gla_v), ('gr', gla_v), ('glr', GLA_GATE_RANK),
                       ('aq', att_w), ('ak', att_w), ('av', att_w), ('gate_gla', d), ('gate_att', d)):
        src[name] = (off, size)
        off += size
    assert off == w_in.shape[2]
    wt = w_in[0].T
    glr_at, glr_n = src['glr']
    assert glr_n == GLA_GATE_RANK

    def col(name, group=0):
        return src[name][0] + group * hw

    plain = (('gq', gla_k), ('gk', gla_k), ('gv', gla_v), ('gr', gla_v), ('gate_gla', d), ('gate_att', d),
             ('aq', hw), ('ak', hw), ('av', hw))
    layout, tile_cols = {}, []
    for name, size in plain:
        layout[name] = len(tile_cols) * hw
        tile_cols += [col(name) + t * hw for t in range(size // hw)]
    n_plain = len(tile_cols)
    tile_cols += [col(kind, g) for g in range(1, ngrp) for kind in ('aq', 'ak', 'av')]
    dilations = [dil for _, dil in ATT_GROUPS[1:]]

    x2 = x.reshape(n, d)
    proj, glr, *qkv_dil = _in_projection(x2, norm_mix_g[0], wt, glr_at, batch, seq, n_plain, dilations,
                                         tuple(tile_cols))
    y_gla = _gla(proj, glr, gla_gate_up[0], gla_gate_bias[0], gla_norm_g[0], batch, seq, layout)
    att = [_dilated_attention(proj.reshape(batch, 1, seq, proj.shape[1]), 1, layout['aq'])]
    att += [_dilated_attention(a, dil, 0) for a, dil in zip(qkv_dil, dilations)]
    tt = min(TOKEN_TILE, n)
    h, hn, logits = _merge(
        x2, y_gla, att, proj, layout, w_branch_gla[0].astype(BF16), w_branch_att[0].astype(BF16),
        w_mix_out[0].astype(BF16), norm_ffn_g[0], router_w[0], router_b[0], seq, min(MERGE_TILE, seq))
    gates, loc, cum, cnt = _route(logits, tt)

    bm = MOE_BLOCK_ROWS
    rloc = TOP_K * tt
    nblk = -(-(TOP_K * n) // bm) + N_EXPERTS
    tabs, blocks = _routing_tables(cum[:, :, 0], cnt[:, 0], bm, nblk)
    x_lin = _dispatch(hn, loc, tabs, blocks[1], nblk * bm, rloc)
    y_lin = _experts(x_lin, *blocks, expert_w_gate_up[0], expert_b_gate_up[0], expert_w_down[0],
                     expert_b_down[0], bm)
    out = _combine(h, y_lin, loc, gates, tabs, norm_final_g, rloc)
    return out.reshape(batch, seq, d)
```

```python
import functools

import jax
import jax.numpy as jnp
import numpy as np
from jax import lax
from jax.experimental import pallas as pl
from jax.experimental.pallas import tpu as pltpu

GLA_HEADS = 4
GLA_GATE_RANK = 16
GLA_TAU = 16.0
GLA_CHUNK = 64
ATT_GROUPS = ((128, 1), (512, 4), (2048, 16))
ATT_HEADS_PER_GROUP = 8
ATT_HEAD_DIM = 64
ATT_BLOCK = 128
ROT_DIM = ATT_HEAD_DIM // 4
ROPE_THETA = 500000.0
N_EXPERTS = 32
TOP_K = 4
SWIGLU_LIMIT = 7.0
SWIGLU_ALPHA = 1.702
EPS = 1e-5

LANES = 128
MXU_DIM = 256
VMEM_LIMIT = 56 * 1024 * 1024

F32 = jnp.float32
BF16 = jnp.bfloat16


def _cparams(sem):
    return pltpu.CompilerParams(dimension_semantics=sem, vmem_limit_bytes=VMEM_LIMIT)


def _dot(a, b):
    return jnp.dot(a, b, preferred_element_type=F32)


def _dot_nt(a, b):
    return lax.dot_general(a, b, (((1,), (1,)), ((), ())), preferred_element_type=F32)


def _dot_tn(a, b):
    return lax.dot_general(a, b, (((0,), (0,)), ((), ())), preferred_element_type=F32)


def _split2(a):
    hi = a.astype(BF16)
    lo = (a - hi.astype(F32)).astype(BF16)
    return hi, lo


def _inproj_kernel(x_ref, g_ref, wt_hbm, wglr_ref, cos_ref, sa_ref, sb_ref, proj_ref, glr_ref, *rest,
                   n_plain, dilations, q_scale, tn, tile_rows):
    dil_refs, (xn_ref, stage_ref, w_ref, wst_ref, wsem) = rest[:len(dilations)], rest[len(dilations):]
    nslab = tn // LANES

    @pl.when(pl.program_id(0) == 0)
    def _():
        def window(j):
            return pltpu.make_async_copy(wt_hbm.at[pl.ds(tile_rows[j], tn)], wst_ref.at[j % 2], wsem.at[j % 2])

        window(0).start()
        for j in range(len(tile_rows)):
            if j + 1 < len(tile_rows):
                window(j + 1).start()
            window(j).wait()
            for s in range(nslab):
                w_ref[:, j * tn + s * LANES:j * tn + (s + 1) * LANES] = \
                    jnp.transpose(wst_ref[j % 2, s * LANES:(s + 1) * LANES, :]).astype(BF16)

    x = x_ref[...]
    y = x * lax.rsqrt(jnp.mean(x * x, axis=-1, keepdims=True) + EPS) * g_ref[...]
    xn_ref[...] = y.astype(BF16)
    yh, yl = _split2(y)
    wh, wl = _split2(wglr_ref[...])
    glr_ref[...] = _dot_nt(yh, wh) + _dot_nt(yh, wl) + _dot_nt(yl, wh)

    tm = xn_ref.shape[0]
    half = ROT_DIM // 2

    def slabs(j, kind):
        acc = _dot(xn_ref[...], w_ref[:, j * tn:(j + 1) * tn])
        for s in range(nslab):
            t = acc[:, s * LANES:(s + 1) * LANES]
            if kind < 2:
                t = t * cos_ref[...] + pltpu.roll(t, LANES - half, 1) * sa_ref[...] \
                    + pltpu.roll(t, half, 1) * sb_ref[...]
                if kind == 0:
                    t = t * q_scale
            yield s, t

    for gi, dil in enumerate(dilations):
        for kind in range(3):
            j = n_plain + 3 * gi + kind
            buf = (3 * gi + kind) % stage_ref.shape[0]
            for s, t in slabs(j, kind):
                stage_ref[buf, s] = t
            for r in range(dil):
                for s in range(nslab):
                    dil_refs[gi][0, r, :, kind * tn + s * LANES:kind * tn + (s + 1) * LANES] = \
                        stage_ref[buf, s, pl.ds(r, tm // dil, stride=dil), :].astype(BF16)

    q0 = n_plain - 3
    for j in range(n_plain):
        for s, t in slabs(j, j - q0 if j >= q0 else 2):
            proj_ref[:, j * tn + s * LANES:j * tn + (s + 1) * LANES] = t.astype(BF16)


def _rope_tables(seq):
    half = ROT_DIM // 2
    inv_freq = ROPE_THETA ** (-np.arange(half, dtype=np.float32) * np.float32(2.0 / ROT_DIM))
    pos = jnp.arange(seq, dtype=F32)
    ang = pos[:, None] * jnp.asarray(inv_freq, F32)[None, :]
    cos = jnp.cos(ang)
    sin = jnp.sin(ang)
    lane = np.arange(LANES) % ATT_HEAD_DIM
    idx = np.where(lane < ROT_DIM, lane % half, 0)
    in_rot = jnp.asarray(lane < ROT_DIM)
    first = jnp.asarray(lane < half)
    second = jnp.asarray((lane >= half) & (lane < ROT_DIM))
    cos_t = jnp.where(in_rot[None, :], cos[:, idx], 1.0)
    sin_g = sin[:, idx]
    sa = jnp.where(first[None, :], -sin_g, 0.0)
    sb = jnp.where(second[None, :], sin_g, 0.0)
    return cos_t.astype(F32), sa.astype(F32), sb.astype(F32)


def _in_projection(x2, norm_g, wt, glr_row, batch, seq, n_plain, dilations, tile_rows):
    n, d = x2.shape
    tn = ATT_HEADS_PER_GROUP * ATT_HEAD_DIM
    tm = min(512, seq)
    ntile = len(tile_rows)
    assert n % tm == 0 and seq % tm == 0 and ntile == n_plain + 3 * len(dilations)
    assert all(tm % (dil * 16) == 0 for dil in dilations)
    assert glr_row % GLA_GATE_RANK == 0 and all(r % 8 == 0 and r + tn <= wt.shape[0] for r in tile_rows)
    cos_t, sa, sb = _rope_tables(seq)
    spt = seq // tm
    kern = functools.partial(_inproj_kernel, n_plain=n_plain, dilations=tuple(dilations),
                             q_scale=ATT_HEAD_DIM ** -0.5, tn=tn, tile_rows=tuple(tile_rows))
    out_specs = [
        pl.BlockSpec((tm, n_plain * tn), lambda i: (i, 0)),
        pl.BlockSpec((tm, GLA_GATE_RANK), lambda i: (i, 0)),
    ]
    out_shape = [
        jax.ShapeDtypeStruct((n, n_plain * tn), BF16),
        jax.ShapeDtypeStruct((n, GLA_GATE_RANK), F32),
    ]
    for dil in dilations:
        out_specs.append(pl.BlockSpec((1, dil, tm // dil, 3 * tn), lambda i: (i // spt, 0, i % spt, 0)))
        out_shape.append(jax.ShapeDtypeStruct((batch, dil, seq // dil, 3 * tn), BF16))
    const = lambda i: (0, 0)
    return pl.pallas_call(
        kern,
        grid=(n // tm,),
        in_specs=[
            pl.BlockSpec((tm, d), lambda i: (i, 0)),
            pl.BlockSpec((1, d), const),
            pl.BlockSpec(memory_space=pl.ANY),
            pl.BlockSpec((GLA_GATE_RANK, d), lambda i: (glr_row // GLA_GATE_RANK, 0)),
            pl.BlockSpec((tm, LANES), lambda i: (i % spt, 0)),
            pl.BlockSpec((tm, LANES), lambda i: (i % spt, 0)),
            pl.BlockSpec((tm, LANES), lambda i: (i % spt, 0)),
        ],
        out_specs=out_specs,
        out_shape=out_shape,
        scratch_shapes=[pltpu.VMEM((tm, d), BF16), pltpu.VMEM((2, tn // LANES, tm, LANES), F32),
                        pltpu.VMEM((d, ntile * tn), BF16),
                        pltpu.VMEM((2, tn, d), F32), pltpu.SemaphoreType.DMA((2,))],
        compiler_params=_cparams(("arbitrary",)),
        name="in_projection",
    )(x2, norm_g.reshape(1, d), wt, wt, cos_t, sa, sb)


def _gla_kernel(q_ref, k_ref, v_ref, r_ref, glr_ref, up_ref, bias_ref, ng_ref, tri_ref, y_ref, state_ref,
                *, dk, dv, ts):
    t = pl.program_id(2)

    @pl.when(t == 0)
    def _():
        state_ref[...] = jnp.zeros_like(state_ref)

    c = GLA_CHUNK
    hps = state_ref.shape[0]
    gh, gl = _split2(glr_ref[...])
    uh, ul = _split2(up_ref[...])
    z = _dot(gh, uh) + _dot(gh, ul) + _dot(gl, uh) + bias_ref[...]
    logdec = (jnp.minimum(z, 0.0) - jnp.log1p(jnp.exp(-jnp.abs(z)))) * (1.0 / GLA_TAU)
    ldh, ldl = _split2(logdec)
    tri = tri_ref[...]
    row = lax.broadcasted_iota(jnp.int32, (c, c), 0)
    col = lax.broadcasted_iota(jnp.int32, (c, c), 1)
    causal = col <= row
    scale = dk ** -0.5
    ng = ng_ref[...]
    states = [state_ref[hh] for hh in range(hps)]
    for ci in range(ts // c):
        sl = slice(ci * c, (ci + 1) * c)
        b2 = _dot(tri, ldh[sl]) + _dot(tri, ldl[sl])
        for hh in range(hps):
            ks = slice(hh * dk, (hh + 1) * dk)
            vs = slice(hh * dv, (hh + 1) * dv)
            b = b2[:, ks]
            bl = b[c - 1:c, :]
            eb = jnp.exp(b)
            enb = jnp.exp(-b)
            ebl = jnp.exp(bl)
            qf = q_ref[sl, ks].astype(F32)
            kf = k_ref[sl, ks].astype(F32)
            qe = (qf * scale * eb).astype(BF16)
            ke = (kf * enb).astype(BF16)
            kd = (kf * enb * ebl).astype(BF16)
            vb = v_ref[sl, vs]
            a = jnp.where(causal, _dot_nt(qe, ke), 0.0).astype(BF16)
            o = _dot(jnp.concatenate([qe, a], axis=1),
                     jnp.concatenate([states[hh].astype(BF16), vb], axis=0))
            dec = jnp.transpose(jnp.broadcast_to(ebl, (dk, dk)))
            dec_full = jnp.concatenate([dec] * (dv // dk), axis=1)
            states[hh] = dec_full * states[hh] + _dot_tn(kd, vb)
            o = o * lax.rsqrt(jnp.mean(o * o, axis=-1, keepdims=True) + EPS) * ng
            rf = r_ref[sl, vs].astype(F32)
            o = o * (rf * jax.nn.sigmoid(rf))
            y_ref[sl, vs] = o.astype(BF16)
    for hh in range(hps):
        state_ref[hh] = states[hh]


def _gla(proj, glr, gate_up, gate_bias, norm_g, batch, seq, layout):
    n = proj.shape[0]
    dk = (layout['gk'] - layout['gq']) // GLA_HEADS
    dv = (layout['gr'] - layout['gv']) // GLA_HEADS
    ts = min(512, seq)
    assert dk == LANES and dv % dk == 0 and seq % ts == 0 and ts % GLA_CHUNK == 0
    spt = seq // ts
    hps = GLA_HEADS
    wk, wv = hps * dk, hps * dv
    assert GLA_HEADS % hps == 0 and all(layout[s] % wk == 0 for s in ('gq', 'gk'))
    assert all(layout[s] % wv == 0 for s in ('gv', 'gr'))
    qb, kb = layout['gq'] // wk, layout['gk'] // wk
    vb, rb = layout['gv'] // wv, layout['gr'] // wv
    tri = jnp.asarray(np.tril(np.ones((GLA_CHUNK, GLA_CHUNK), np.float32)), BF16)
    kern = functools.partial(_gla_kernel, dk=dk, dv=dv, ts=ts)
    row = lambda b, h, t: b * spt + t
    return pl.pallas_call(
        kern,
        grid=(batch, GLA_HEADS // hps, spt),
        in_specs=[
            pl.BlockSpec((ts, wk), lambda b, h, t: (row(b, h, t), qb + h)),
            pl.BlockSpec((ts, wk), lambda b, h, t: (row(b, h, t), kb + h)),
            pl.BlockSpec((ts, wv), lambda b, h, t: (row(b, h, t), vb + h)),
            pl.BlockSpec((ts, wv), lambda b, h, t: (row(b, h, t), rb + h)),
            pl.BlockSpec((ts, GLA_GATE_RANK), lambda b, h, t: (row(b, h, t), 0)),
            pl.BlockSpec((GLA_GATE_RANK, wk), lambda b, h, t: (0, h)),
            pl.BlockSpec((1, wk), lambda b, h, t: (0, h)),
            pl.BlockSpec((1, dv), lambda b, h, t: (0, 0)),
            pl.BlockSpec((GLA_CHUNK, GLA_CHUNK), lambda b, h, t: (0, 0)),
        ],
        out_specs=pl.BlockSpec((ts, wv), lambda b, h, t: (row(b, h, t), h)),
        out_shape=jax.ShapeDtypeStruct((n, GLA_HEADS * dv), BF16),
        scratch_shapes=[pltpu.VMEM((hps, dk, dv), F32)],
        compiler_params=_cparams(("arbitrary", "arbitrary", "arbitrary")),
        name="gla",
    )(proj, proj, proj, proj, glr, gate_up, gate_bias.reshape(1, -1), norm_g.reshape(1, -1), tri)


LSE_REP = LANES // ATT_HEADS_PER_GROUP


def _att_kernel(q_ref, k_ref, v_ref, o_ref, lse_ref, *, nb, pairs, unroll):
    blk = ATT_BLOCK
    hd = ATT_HEAD_DIM
    row = lax.broadcasted_iota(jnp.int32, (2 * blk, 2 * blk), 0) % blk
    col = lax.broadcasted_iota(jnp.int32, (2 * blk, 2 * blk), 1)
    dist = row + blk - col
    band = jnp.logical_and(dist >= 0, dist <= blk)
    causal = band[:, blk:]
    lane = lax.broadcasted_iota(jnp.int32, (blk, LANES), 1)
    head0 = lane < hd
    lane_head = lane // LSE_REP
    neg = -jnp.inf

    def block(j, first):
        r0 = pl.multiple_of(j * blk, blk)
        lse = jnp.zeros((blk, LANES), F32)
        for p in range(pairs):
            cs = slice(p * LANES, (p + 1) * LANES)
            q2 = q_ref[0, 0, pl.ds(r0, blk), cs]
            zero = jnp.zeros_like(q2)
            qs = jnp.concatenate([jnp.where(head0, q2, zero), jnp.where(head0, zero, q2)], axis=0)
            if first:
                kk = k_ref[0, 0, pl.ds(r0, blk), cs]
                vv = v_ref[0, 0, pl.ds(r0, blk), cs]
                s = jnp.where(causal, _dot_nt(qs, kk), neg)
            else:
                k0 = pl.multiple_of(r0 - blk, blk)
                kk = k_ref[0, 0, pl.ds(k0, 2 * blk), cs]
                vv = v_ref[0, 0, pl.ds(k0, 2 * blk), cs]
                s = jnp.where(band, _dot_nt(qs, kk), neg)
            mx = jnp.max(s, axis=-1, keepdims=True)
            e = jnp.exp(s - mx).astype(BF16)
            one = jnp.ones_like(vv)
            hv = lax.broadcasted_iota(jnp.int32, vv.shape, 1) < hd
            out0 = _dot(e[:blk], jnp.where(hv, vv, one))
            out1 = _dot(e[blk:], jnp.where(hv, one, vv))
            num = jnp.where(head0, out0, out1)
            den_x = jnp.where(head0, out1, out0)
            den = pltpu.roll(den_x, hd, 1)
            o_ref[0, 0, pl.ds(r0, blk), cs] = (num / den).astype(o_ref.dtype)
            lse0 = mx[:blk] + jnp.log(jnp.where(head0, den, den_x))
            lse1 = mx[blk:] + jnp.log(jnp.where(head0, den_x, den))
            lse = jnp.where(lane_head == 2 * p, lse0, jnp.where(lane_head == 2 * p + 1, lse1, lse))
        lse_ref[0, 0, pl.ds(r0, blk), :] = lse

    block(0, True)

    def body(j, carry):
        block(j, False)
        return carry

    lax.fori_loop(1, nb, body, 0, unroll=unroll)


def _dilated_attention(qkv, dilation, col0):
    batch, dil, sub, width = qkv.shape
    hw = ATT_HEADS_PER_GROUP * ATT_HEAD_DIM
    assert dil == dilation and sub % ATT_BLOCK == 0 and col0 % hw == 0
    nb = sub // ATT_BLOCK
    pairs = hw // LANES
    assert 2 * pairs * LSE_REP == LANES
    kern = functools.partial(_att_kernel, nb=nb, pairs=pairs, unroll=2 if nb > 2 else 1)

    def in_spec(kind):
        base = (col0 + kind * hw) // hw
        return pl.BlockSpec((1, 1, sub, hw), lambda b, r: (b, r, 0, base))

    return pl.pallas_call(
        kern,
        grid=(batch, dil),
        in_specs=[in_spec(0), in_spec(1), in_spec(2)],
        out_specs=[pl.BlockSpec((1, 1, sub, hw), lambda b, r: (b, r, 0, 0)),
                   pl.BlockSpec((1, 1, sub, LANES), lambda b, r: (b, r, 0, 0))],
        out_shape=[
            jax.ShapeDtypeStruct((batch, dil, sub, hw), BF16),
            jax.ShapeDtypeStruct((batch, dil, sub, LANES), F32),
        ],
        compiler_params=_cparams(("arbitrary", "arbitrary")),
        name=f"dilated_attention_d{dilation}",
    )(qkv, qkv, qkv)


def _merge_kernel(x_ref, yg_ref, o1_ref, o2_ref, o3_ref, l1_ref, l2_ref, l3_ref,
                  gg0_ref, gg1_ref, ga0_ref, ga1_ref, wbg_ref, wba_ref, wmo_ref, nf_ref, rw_ref, rb_ref,
                  spread_ref, h_ref, hn_ref, lg_ref, os_ref, ls_ref):
    def token_order(src_ref, stage_ref):
        dil, sub, w = src_ref.shape[1:]
        if dil == 1:
            return src_ref[0, 0].astype(F32)
        for r in range(dil):
            val = src_ref[0, r].astype(F32)
            for s in range(w // LANES):
                stage_ref[s, pl.ds(r, sub, stride=dil), :] = val[:, s * LANES:(s + 1) * LANES]
        return jnp.concatenate([stage_ref[s] for s in range(w // LANES)], axis=1)

    l1, o1 = token_order(l1_ref, None), token_order(o1_ref, None)
    l2, o2 = token_order(l2_ref, ls_ref.at[0]), token_order(o2_ref, os_ref.at[0])
    l3, o3 = token_order(l3_ref, ls_ref.at[1]), token_order(o3_ref, os_ref.at[1])
    m = jnp.maximum(jnp.maximum(l1, l2), l3)
    w1, w2, w3 = jnp.exp(l1 - m), jnp.exp(l2 - m), jnp.exp(l3 - m)
    inv = 1.0 / (w1 + w2 + w3)
    spread = spread_ref[...]

    def per_lane(w):
        hi, lo = _split2(w * inv)
        return _dot(hi, spread) + _dot(lo, spread)

    y_att = per_lane(w1) * o1 + per_lane(w2) * o2 + per_lane(w3) * o3

    m_gla = _dot(yg_ref[...], wbg_ref[...])
    m_att = _dot(y_att.astype(BF16), wba_ref[...])
    gg = jnp.concatenate([gg0_ref[...], gg1_ref[...]], axis=1).astype(F32)
    ga = jnp.concatenate([ga0_ref[...], ga1_ref[...]], axis=1).astype(F32)
    merged = (0.5 + 0.5 * jnp.tanh(0.5 * gg)) * m_gla + (0.5 + 0.5 * jnp.tanh(0.5 * ga)) * m_att
    h = x_ref[...] + _dot(merged.astype(BF16), wmo_ref[...])
    h_ref[...] = h
    hn = h * lax.rsqrt(jnp.mean(h * h, axis=-1, keepdims=True) + EPS) * nf_ref[...]
    hn_ref[...] = hn.astype(BF16)

    hh, hl = _split2(hn)
    rh, rl = _split2(rw_ref[...])
    lg_ref[...] = _dot_nt(rh, hh) + _dot_nt(rh, hl) + _dot_nt(rl, hh) + rb_ref[...]


def _route_kernel(lg_ref, triu_ref, ltri_ref, tg_ref, lc_ref, cum_ref, cnt_ref, carry_ref):
    ne = N_EXPERTS
    tiles, _, tt = tg_ref.shape

    @pl.when(pl.program_id(0) == 0)
    def _():
        carry_ref[...] = jnp.zeros_like(carry_ref)

    eid = lax.broadcasted_iota(jnp.int32, (ne, tt), 0)
    carry = carry_ref[...]
    for t in range(tiles):
        logit = lg_ref[:, t * tt:(t + 1) * tt]
        member = jnp.zeros((ne, tt), jnp.bool_)
        vals, idxs = [], []
        for _ in range(TOP_K):
            mx = jnp.max(logit, axis=0, keepdims=True)
            idx = jnp.min(jnp.where(logit == mx, eid, ne), axis=0, keepdims=True)
            sel = eid == idx
            member = jnp.logical_or(member, sel)
            logit = jnp.where(sel, -jnp.inf, logit)
            vals.append(mx)
            idxs.append(idx)
        ex = [jnp.exp(v - vals[0]) for v in vals]
        tot = ex[0] + ex[1] + ex[2] + ex[3]
        memf = jnp.where(member, 1.0, 0.0)
        lens = jnp.sum(memf, axis=1, keepdims=True)
        lens_b = jnp.broadcast_to(lens, (ne, LANES))
        off = _dot(ltri_ref[...], lens_b.astype(BF16))
        pos = _dot(memf.astype(BF16), triu_ref[...]) + off[:, 0:1]
        for k in range(TOP_K):
            tg_ref[t, k:k + 1, :] = ex[k] / tot
            lc_ref[t, k:k + 1, :] = jnp.sum(jnp.where(eid == idxs[k], pos, 0.0), axis=0,
                                            keepdims=True).astype(jnp.int32)
        cum_ref[t] = carry.astype(jnp.int32)
        carry = carry + lens_b
    carry_ref[...] = carry
    cnt_ref[...] = carry.astype(jnp.int32)


def _route(logits, tt):
    ne, n = logits.shape
    nt = n // tt
    assert tt <= 256, "per-tile expert counts go through a bf16 matmul operand: exact up to 256"
    tiles = 8 if nt % 8 == 0 else 1
    triu = jnp.asarray(np.triu(np.ones((tt, tt), np.float32), 1), BF16)
    ltri = jnp.asarray(np.tril(np.ones((ne, ne), np.float32), -1), BF16)
    const = lambda i: (0, 0)
    return pl.pallas_call(
        _route_kernel,
        grid=(nt // tiles,),
        in_specs=[pl.BlockSpec((ne, tiles * tt), lambda i: (0, i)),
                  pl.BlockSpec((tt, tt), const), pl.BlockSpec((ne, ne), const)],
        out_specs=[
            pl.BlockSpec((tiles, TOP_K, tt), lambda i: (i, 0, 0)),
            pl.BlockSpec((tiles, TOP_K, tt), lambda i: (i, 0, 0)),
            pl.BlockSpec((tiles, ne, LANES), lambda i: (i, 0, 0)),
            pl.BlockSpec((ne, LANES), const),
        ],
        out_shape=[
            jax.ShapeDtypeStruct((nt, TOP_K, tt), F32),
            jax.ShapeDtypeStruct((nt, TOP_K, tt), jnp.int32),
            jax.ShapeDtypeStruct((nt, ne, LANES), jnp.int32),
            jax.ShapeDtypeStruct((ne, LANES), jnp.int32),
        ],
        scratch_shapes=[pltpu.VMEM((ne, LANES), F32)],
        compiler_params=_cparams(("arbitrary",)),
        name="route_topk",
    )(logits, triu, ltri)


def _merge(x2, y_gla, att, proj, layout, wbg, wba, wmo, norm_ffn_g, router_w, router_b, seq, tm):
    n, d = x2.shape
    hw = ATT_HEADS_PER_GROUP * ATT_HEAD_DIM
    half = d // 2
    assert n % tm == 0 and seq % tm == 0
    assert layout['gate_gla'] % half == 0 and layout['gate_att'] % half == 0
    assert all(tm % (o.shape[1] * 16) == 0 for o, _ in att) and att[0][0].shape[1] == 1
    nt = n // tm
    spt = seq // tm
    ggb = layout['gate_gla'] // half
    gab = layout['gate_att'] // half
    row = lambda i: (i, 0)
    const = lambda i: (0, 0)

    def att_spec(a):
        dil = a.shape[1]
        return pl.BlockSpec((1, dil, tm // dil, a.shape[3]), lambda i: (i // spt, 0, i % spt, 0))

    (o1, l1), (o2, l2), (o3, l3) = att
    in_specs = [
        pl.BlockSpec((tm, d), row),
        pl.BlockSpec((tm, y_gla.shape[1]), row),
        att_spec(o1), att_spec(o2), att_spec(o3), att_spec(l1), att_spec(l2), att_spec(l3),
        pl.BlockSpec((tm, half), lambda i: (i, ggb)), pl.BlockSpec((tm, half), lambda i: (i, ggb + 1)),
        pl.BlockSpec((tm, half), lambda i: (i, gab)), pl.BlockSpec((tm, half), lambda i: (i, gab + 1)),
        pl.BlockSpec(wbg.shape, const), pl.BlockSpec(wba.shape, const), pl.BlockSpec(wmo.shape, const),
        pl.BlockSpec((1, d), const),
        pl.BlockSpec((N_EXPERTS, d), const),
        pl.BlockSpec((N_EXPERTS, 1), const),
        pl.BlockSpec((LANES, hw), const),
    ]
    out_specs = [
        pl.BlockSpec((tm, d), row),
        pl.BlockSpec((tm, d), row),
        pl.BlockSpec((N_EXPERTS, tm), lambda i: (0, i)),
    ]
    out_shape = [
        jax.ShapeDtypeStruct((n, d), F32),
        jax.ShapeDtypeStruct((n, d), BF16),
        jax.ShapeDtypeStruct((N_EXPERTS, n), F32),
    ]
    spread = jnp.asarray(np.arange(LANES)[:, None] == (np.arange(hw)[None, :] // ATT_HEAD_DIM) * LSE_REP, BF16)
    return pl.pallas_call(
        _merge_kernel,
        grid=(nt,),
        in_specs=in_specs,
        out_specs=out_specs,
        out_shape=out_shape,
        scratch_shapes=[pltpu.VMEM((2, hw // LANES, tm, LANES), F32), pltpu.VMEM((2, 1, tm, LANES), F32)],
        compiler_params=_cparams(("arbitrary",)),
        name="merge_router",
    )(x2, y_gla, o1, o2, o3, l1, l2, l3, proj, proj, proj, proj, wbg, wba, wmo,
      norm_ffn_g.reshape(1, d), router_w.T, router_b.reshape(N_EXPERTS, 1), spread)


LIN_SUB = 8


def _lin_pack(val):
    return [val[:, c * LANES:(c + 1) * LANES] for c in range(LIN_SUB)]


def _lin_unpack(slabs):
    return jnp.concatenate(slabs, axis=1)


def _dispatch_kernel(off_ref, len_ref, dst_ref, zdst_ref, zlen_ref, nact_ref, hn_ref, loc_ref, x_hbm,
                     xs_ref, zero_ref, sem, zsem, *, rloc):
    i = pl.program_id(0)
    nt = pl.num_programs(0)
    slot = i % 2
    ne = N_EXPERTS

    def drain(s):
        rows = rloc * LIN_SUB
        pltpu.make_async_copy(xs_ref.at[s], x_hbm.at[pl.ds(0, rows)], sem.at[s]).wait()

    @pl.when(i == 0)
    def _():
        zero_ref[...] = jnp.zeros_like(zero_ref)

        def fill_copy(e):
            rows = zlen_ref[e] * LIN_SUB
            dst = pl.multiple_of(zdst_ref[e] * LIN_SUB, 8)
            return rows, pltpu.make_async_copy(zero_ref.at[pl.ds(0, rows)], x_hbm.at[pl.ds(dst, rows)], zsem)

        def fill(e, carry):
            rows, copy = fill_copy(e)
            pl.when(rows > 0)(copy.start)
            return carry

        def fill_done(e, carry):
            rows, copy = fill_copy(e)
            pl.when(rows > 0)(copy.wait)
            return carry

        def block_copy(b):
            rows = zero_ref.shape[0]
            return pltpu.make_async_copy(zero_ref, x_hbm.at[pl.ds(pl.multiple_of(b * rows, rows), rows)], zsem)

        def fill_block(b, carry):
            block_copy(b).start()
            return carry

        def fill_block_done(b, carry):
            block_copy(b).wait()
            return carry

        nblk = x_hbm.shape[0] // zero_ref.shape[0]
        lax.fori_loop(0, ne, fill, 0)
        lax.fori_loop(nact_ref[0], nblk, fill_block, 0)
        lax.fori_loop(0, ne, fill_done, 0)
        lax.fori_loop(nact_ref[0], nblk, fill_block_done, 0)

    @pl.when(i >= 2)
    def _():
        drain(slot)

    loc = loc_ref[0]
    tt = loc.shape[1]
    row_id = lax.broadcasted_iota(jnp.int32, (rloc, tt), 0)
    hit = row_id == loc[0:1, :]
    for k in range(1, TOP_K):
        hit = jnp.logical_or(hit, row_id == loc[k:k + 1, :])
    onehot = jnp.where(hit, 1.0, 0.0).astype(BF16)
    xs = _dot(onehot, hn_ref[...])
    for c, slab in enumerate(_lin_pack(xs)):
        xs_ref[slot, pl.ds(c, rloc, stride=LIN_SUB), :] = slab

    def send(e, carry):
        rows = len_ref[i * ne + e] * LIN_SUB

        @pl.when(rows > 0)
        def _():
            src = pl.multiple_of(off_ref[i * ne + e] * LIN_SUB, 8)
            dst = pl.multiple_of(dst_ref[i * ne + e] * LIN_SUB, 8)
            pltpu.make_async_copy(xs_ref.at[slot, pl.ds(src, rows)], x_hbm.at[pl.ds(dst, rows)], sem.at[slot]).start()
        return carry

    lax.fori_loop(0, ne, send, 0)

    @pl.when(i == nt - 1)
    def _():
        @pl.when(i >= 1)
        def _():
            drain(1 - slot)
        drain(slot)


def _dispatch(hn, loc, tabs, nact, p_rows, rloc):
    n, d = hn.shape
    nt, _, tt = loc.shape
    assert d == LIN_SUB * LANES and p_rows % MOE_BLOCK_ROWS == 0
    kern = functools.partial(_dispatch_kernel, rloc=rloc)
    grid_spec = pltpu.PrefetchScalarGridSpec(
        num_scalar_prefetch=6,
        grid=(nt,),
        in_specs=[
            pl.BlockSpec((tt, d), lambda i, *_: (i, 0)),
            pl.BlockSpec((1, TOP_K, tt), lambda i, *_: (i, 0, 0)),
        ],
        out_specs=pl.BlockSpec(memory_space=pl.ANY),
        scratch_shapes=[
            pltpu.VMEM((2, rloc * LIN_SUB, LANES), F32),
            pltpu.VMEM((MOE_BLOCK_ROWS * LIN_SUB, LANES), F32),
            pltpu.SemaphoreType.DMA((2,)),
            pltpu.SemaphoreType.DMA(()),
        ],
    )
    return pl.pallas_call(
        kern,
        grid_spec=grid_spec,
        out_shape=jax.ShapeDtypeStruct((p_rows * LIN_SUB, LANES), F32),
        compiler_params=_cparams(("arbitrary",)),
        name="moe_dispatch",
    )(tabs['off'], tabs['len'], tabs['dst'], tabs['zdst'], tabs['zlen'], nact, hn, loc)


def _expert_kernel(bexp_ref, nact_ref, first_ref, next_ref, x_ref, wgu_hbm, wd_hbm, bgu_ref, bd_ref, perm_ref,
                   y_ref, wgu_f, wd_f, wgu_s, wd_s, sem, *, bm):
    grp = MXU_DIM
    ff2 = wgu_s.shape[1]
    halfg = grp // 2
    brows = bm * LIN_SUB
    bps = x_ref.shape[0] // brows

    def fetch(ex):
        return (pltpu.make_async_copy(wgu_hbm.at[ex], wgu_f, sem.at[0]),
                pltpu.make_async_copy(wd_hbm.at[ex], wd_f, sem.at[1]))

    @pl.when(pl.program_id(0) == 0)
    def _():
        for copy in fetch(bexp_ref[0]):
            copy.start()

    def block(b, r0):
        e = bexp_ref[b]
        active = b < nact_ref[0]

        @pl.when(jnp.logical_and(active, first_ref[b] == 1))
        def _():
            for copy in fetch(e):
                copy.wait()
            perm = perm_ref[...]
            for g in range(ff2 // grp):
                cols = slice(g * grp, (g + 1) * grp)
                wgu_s[:, cols] = _dot(wgu_f[:, cols].astype(BF16), perm).astype(BF16)
            wd_s[...] = wd_f[...].astype(BF16)
            nxt = next_ref[b]

            @pl.when(nxt >= 0)
            def _():
                for copy in fetch(nxt):
                    copy.start()

        @pl.when(active)
        def _():
            x = _lin_unpack([x_ref[pl.ds(r0 + c, bm, stride=LIN_SUB), :] for c in range(LIN_SUB)]).astype(BF16)
            acts = []
            for g in range(ff2 // grp):
                cols = slice(g * grp, (g + 1) * grp)
                hg = _dot(x, wgu_s[:, cols]) + bgu_ref[e, :, cols]
                gate = jnp.minimum(hg[:, :halfg], SWIGLU_LIMIT)
                up = jnp.clip(hg[:, halfg:], -SWIGLU_LIMIT, SWIGLU_LIMIT)
                acts.append(((up + 1.0) * (gate * jax.nn.sigmoid(SWIGLU_ALPHA * gate))).astype(BF16))
            act = jnp.concatenate(acts, axis=1)
            y = _dot(act, wd_s[...]) + bd_ref[e]
            for c, slab in enumerate(_lin_pack(y)):
                y_ref[pl.ds(r0 + c, bm, stride=LIN_SUB), :] = slab

        @pl.when(jnp.logical_not(active))
        def _():
            y_ref[pl.ds(r0, brows), :] = jnp.zeros((brows, y_ref.shape[1]), y_ref.dtype)

    for sub in range(bps):
        block(pl.program_id(0) * bps + sub, sub * brows)


def _gate_up_permutation():
    grp = MXU_DIM
    halfg = grp // 2
    out = np.arange(grp)
    src = np.where(out < halfg, 2 * out, 2 * (out - halfg) + 1)
    p = np.zeros((grp, grp), np.float32)
    p[src, out] = 1.0
    return jnp.asarray(p, BF16)


def _experts(x_lin, bexp, nact, first, nxt, w_gate_up, b_gate_up, w_down, b_down, bm):
    ne, d, ff2 = w_gate_up.shape
    ff = ff2 // 2
    grp = MXU_DIM
    assert ff2 % grp == 0 and d == LIN_SUB * LANES
    nblk = x_lin.shape[0] // (bm * LIN_SUB)
    bps = max(b for b in (4, 2, 1) if nblk % b == 0)
    srows = bps * bm * LIN_SUB
    bgu = b_gate_up.reshape(ne, ff2 // grp, grp // 2, 2).transpose(0, 1, 3, 2).reshape(ne, 1, ff2)
    grid_spec = pltpu.PrefetchScalarGridSpec(
        num_scalar_prefetch=4,
        grid=(nblk // bps,),
        in_specs=[
            pl.BlockSpec((srows, LANES), lambda i, *_: (i, 0)),
            pl.BlockSpec(memory_space=pl.ANY),
            pl.BlockSpec(memory_space=pl.ANY),
            pl.BlockSpec(memory_space=pltpu.VMEM),
            pl.BlockSpec(memory_space=pltpu.VMEM),
            pl.BlockSpec((grp, grp), lambda i, *_: (0, 0)),
        ],
        out_specs=pl.BlockSpec((srows, LANES), lambda i, *_: (i, 0)),
        scratch_shapes=[pltpu.VMEM((d, ff2), F32), pltpu.VMEM((ff, d), F32),
                        pltpu.VMEM((d, ff2), BF16), pltpu.VMEM((ff, d), BF16),
                        pltpu.SemaphoreType.DMA((2,))],
    )
    return pl.pallas_call(
        functools.partial(_expert_kernel, bm=bm),
        grid_spec=grid_spec,
        out_shape=jax.ShapeDtypeStruct(x_lin.shape, F32),
        compiler_params=_cparams(("arbitrary",)),
        name="moe_experts",
    )(bexp, nact, first, nxt, x_lin, w_gate_up, w_down, bgu, b_down.reshape(ne, 1, d), _gate_up_permutation())


def _combine_kernel(off_ref, len_ref, dst_ref, h_ref, loc_ref, gate_ref, g_ref, y_hbm, out_ref,
                    ybuf, sem, *, rloc):
    i = pl.program_id(0)
    nt = pl.num_programs(0)
    slot = i % 2
    ne = N_EXPERTS

    def fetch(step, s):
        def run(e, carry):
            rows = len_ref[step * ne + e] * LIN_SUB

            @pl.when(rows > 0)
            def _():
                src = pl.multiple_of(dst_ref[step * ne + e] * LIN_SUB, 8)
                dst = pl.multiple_of(off_ref[step * ne + e] * LIN_SUB, 8)
                pltpu.make_async_copy(y_hbm.at[pl.ds(src, rows)], ybuf.at[s, pl.ds(dst, rows)], sem.at[s]).start()
            return carry

        lax.fori_loop(0, ne, run, 0)

    @pl.when(i == 0)
    def _():
        fetch(0, 0)

    @pl.when(i + 1 < nt)
    def _():
        fetch(i + 1, 1 - slot)

    pltpu.make_async_copy(y_hbm.at[pl.ds(0, rloc * LIN_SUB)], ybuf.at[slot], sem.at[slot]).wait()

    y = _lin_unpack([ybuf[slot, pl.ds(c, rloc, stride=LIN_SUB), :] for c in range(LIN_SUB)]).astype(BF16)
    loc = loc_ref[0]
    gate = gate_ref[0]
    tt = loc.shape[1]
    row_id = lax.broadcasted_iota(jnp.int32, (rloc, tt), 0)
    w = jnp.zeros((rloc, tt), F32)
    for k in range(TOP_K):
        w = w + jnp.where(row_id == loc[k:k + 1, :], gate[k:k + 1, :], 0.0)
    h = h_ref[...] + _dot_tn(w.astype(BF16), y)
    out_ref[...] = h * lax.rsqrt(jnp.mean(h * h, axis=-1, keepdims=True) + EPS) * g_ref[...]


def _combine(h, y_lin, loc, gates, tabs, norm_g, rloc):
    n, d = h.shape
    nt, _, tt = loc.shape
    kern = functools.partial(_combine_kernel, rloc=rloc)
    grid_spec = pltpu.PrefetchScalarGridSpec(
        num_scalar_prefetch=3,
        grid=(nt,),
        in_specs=[
            pl.BlockSpec((tt, d), lambda i, *_: (i, 0)),
            pl.BlockSpec((1, TOP_K, tt), lambda i, *_: (i, 0, 0)),
            pl.BlockSpec((1, TOP_K, tt), lambda i, *_: (i, 0, 0)),
            pl.BlockSpec((1, d), lambda i, *_: (0, 0)),
            pl.BlockSpec(memory_space=pl.ANY),
        ],
        out_specs=pl.BlockSpec((tt, d), lambda i, *_: (i, 0)),
        scratch_shapes=[
            pltpu.VMEM((2, rloc * LIN_SUB, LANES), F32),
            pltpu.SemaphoreType.DMA((2,)),
        ],
    )
    return pl.pallas_call(
        kern,
        grid_spec=grid_spec,
        out_shape=jax.ShapeDtypeStruct((n, d), F32),
        compiler_params=_cparams(("arbitrary",)),
        name="moe_combine",
    )(tabs['off'], tabs['len'], tabs['dst'], h, loc, gates, norm_g.reshape(1, d), y_lin)


def _routing_tables(cum, counts, bm, nblk):
    padded = (counts + bm - 1) // bm * bm
    pend = jnp.cumsum(padded)
    pstart = pend - padded
    blk = jnp.arange(nblk, dtype=jnp.int32)
    bexp_raw = jnp.sum((pend // bm)[None, :] <= blk[:, None], axis=1).astype(jnp.int32)
    nact = (pend[-1] // bm).astype(jnp.int32)
    last = jnp.minimum(bexp_raw[jnp.maximum(nact - 1, 0)], N_EXPERTS - 1)
    bexp = jnp.where(blk < nact, jnp.minimum(bexp_raw, N_EXPERTS - 1), last)
    prev = jnp.concatenate([jnp.full((1,), -1, jnp.int32), bexp[:-1]])
    first = jnp.logical_and(bexp != prev, blk < nact)
    later = jnp.logical_and(first[None, :], blk[None, :] > blk[:, None])
    nxt_blk = jnp.min(jnp.where(later, blk[None, :], nblk), axis=1)
    nxt = jnp.where(nxt_blk < nblk, bexp[jnp.minimum(nxt_blk, nblk - 1)], -1)
    cum_full = jnp.concatenate([cum, counts[None, :]], axis=0)
    run_len = cum_full[1:] - cum_full[:-1]
    run_off = jnp.cumsum(run_len, axis=1) - run_len
    tabs = {
        'off': run_off.reshape(-1), 'len': run_len.reshape(-1),
        'dst': (pstart[None, :] + cum_full[:-1]).reshape(-1),
        'zdst': pstart + counts, 'zlen': padded - counts,
    }
    blocks = (bexp, nact.reshape(1), first.astype(jnp.int32), nxt.astype(jnp.int32))
    return {k: v.astype(jnp.int32) for k, v in tabs.items()}, blocks


MOE_BLOCK_ROWS = 256
TOKEN_TILE = 256
MERGE_TILE = 512


def kernel(x, norm_mix_g, w_in, gla_gate_up, gla_gate_bias, gla_norm_g, w_branch_gla, w_branch_att, w_mix_out,
           norm_ffn_g, router_w, router_b, expert_w_gate_up, expert_b_gate_up, expert_w_down, expert_b_down,
           norm_final_g):
    batch, seq, d = x.shape
    assert norm_mix_g.shape[0] == 1, "single layer"
    n = batch * seq
    gla_k = gla_gate_up.shape[2]
    gla_v = w_branch_gla.shape[1]
    att_w = len(ATT_GROUPS) * ATT_HEADS_PER_GROUP * ATT_HEAD_DIM
    hw = ATT_HEADS_PER_GROUP * ATT_HEAD_DIM
    ngrp = len(ATT_GROUPS)
    assert att_w == ngrp * hw and all(s % hw == 0 for s in (gla_k, gla_v, d))
    assert ATT_GROUPS[0][1] == 1 and all(win // dil == ATT_BLOCK for win, dil in ATT_GROUPS)
    src, off = {}, 0
    for name, size in (('gq', gla_k), ('gk', gla_k), ('gv', gla_v), ('gr', gla_v), ('glr', GLA_GATE_RANK),
                       ('aq', att_w), ('ak', att_w), ('av', att_w), ('gate_gla', d), ('gate_att', d)):
        src[name] = (off, size)
        off += size
    assert off == w_in.shape[2]
    wt = w_in[0].T
    glr_at, glr_n = src['glr']
    assert glr_n == GLA_GATE_RANK

    def col(name, group=0):
        return src[name][0] + group * hw

    plain = (('gq', gla_k), ('gk', gla_k), ('gv', gla_v), ('gr', gla_v), ('gate_gla', d), ('gate_att', d),
             ('aq', hw), ('ak', hw), ('av', hw))
    layout, tile_cols = {}, []
    for name, size in plain:
        layout[name] = len(tile_cols) * hw
        tile_cols += [col(name) + t * hw for t in range(size // hw)]
    n_plain = len(tile_cols)
    tile_cols += [col(kind, g) for g in range(1, ngrp) for kind in ('aq', 'ak', 'av')]
    dilations = [dil for _, dil in ATT_GROUPS[1:]]

    x2 = x.reshape(n, d)
    proj, glr, *qkv_dil = _in_projection(x2, norm_mix_g[0], wt, glr_at, batch, seq, n_plain, dilations,
                                         tuple(tile_cols))
    y_gla = _gla(proj, glr, gla_gate_up[0], gla_gate_bias[0], gla_norm_g[0], batch, seq, layout)
    att = [_dilated_attention(proj.reshape(batch, 1, seq, proj.shape[1]), 1, layout['aq'])]
    att += [_dilated_attention(a, dil, 0) for a, dil in zip(qkv_dil, dilations)]
    tt = min(TOKEN_TILE, n)
    h, hn, logits = _merge(
        x2, y_gla, att, proj, layout, w_branch_gla[0].astype(BF16), w_branch_att[0].astype(BF16),
        w_mix_out[0].astype(BF16), norm_ffn_g[0], router_w[0], router_b[0], seq, min(MERGE_TILE, seq))
    gates, loc, cum, cnt = _route(logits, tt)

    bm = MOE_BLOCK_ROWS
    rloc = TOP_K * tt
    nblk = -(-(TOP_K * n) // bm) + N_EXPERTS
    tabs, blocks = _routing_tables(cum[:, :, 0], cnt[:, 0], bm, nblk)
    x_lin = _dispatch(hn, loc, tabs, blocks[1], nblk * bm, rloc)
    y_lin = _experts(x_lin, *blocks, expert_w_gate_up[0], expert_b_gate_up[0], expert_w_down[0],
                     expert_b_down[0], bm)
    out = _combine(h, y_lin, loc, gates, tabs, norm_final_g, rloc)
    return out.reshape(batch, seq, d)
```
